```python
import jax, jax.numpy as jnp
from jax import lax
import numpy as np

D_MODEL = 1024
BATCH = 2
SEQ = 8192
DEPTH = 2
DEC_BATCH = 128
DEC_SEQ = 8
PAST_LEN = 8192
PAGE_SIZE = 128

N_HEADS = 8
QK_NOPE_DIM = 64
QK_ROPE_DIM = 32
V_HEAD_DIM = 64
Q_LORA_RANK = 256
KV_LORA_RANK = 128
ATTN_WIDTH = N_HEADS * V_HEAD_DIM
ROPE_BASE = 10000.0
SM_SCALE = (QK_NOPE_DIM + QK_ROPE_DIM) ** -0.5
Q_BLOCK = 128
POOL_WINDOWS = (2, 4, 8, 16)
N_POOL_GROUPS = len(POOL_WINDOWS)
POOL_WIDTH = D_MODEL // 2
POOL_GROUP_DIM = POOL_WIDTH // N_POOL_GROUPS
POOL_STATE_LEN = max(POOL_WINDOWS) - 1
N_EXPERTS = 16
N_EXPERT_GROUPS = 4
EXPERTS_PER_GROUP = N_EXPERTS // N_EXPERT_GROUPS
TOP_K = 2
D_EXPERT = D_MODEL // 2
LN_EPS = 1e-5
RMS_EPS = 1e-6
DEEPNORM_ALPHA = (2 * DEPTH) ** 0.25
DEEPNORM_BETA = (8 * DEPTH) ** -0.25
_S1 = Q_LORA_RANK
_S2 = _S1 + KV_LORA_RANK
_S3 = _S2 + QK_ROPE_DIM
_S4 = _S3 + POOL_WIDTH
_S5 = _S4 + D_MODEL
IN_SPLITS = (_S1, _S2, _S3, _S4, _S5)
IN_COLS = _S5 + D_MODEL

kernel_name = 'hybrid_mla_pool_grouped_moe_deepnorm_step'


def layer_norm(x, g, b):
    xf = x.astype(jnp.float32)
    mu = jnp.mean(xf, axis=-1, keepdims=True)
    var = jnp.mean(jnp.square(xf - mu), axis=-1, keepdims=True)
    return ((xf - mu) * lax.rsqrt(var + LN_EPS) * g + b).astype(x.dtype)


def rms_norm(x, g):
    xf = x.astype(jnp.float32)
    return (xf * lax.rsqrt(jnp.mean(xf * xf, axis=-1, keepdims=True) + RMS_EPS) * g).astype(x.dtype)


def apply_rope(x, pos):
    half = QK_ROPE_DIM // 2
    inv = ROPE_BASE ** (-jnp.arange(half, dtype=jnp.float32) / half)
    ang = pos.astype(jnp.float32)[:, None, None] * inv
    cos, sin = jnp.cos(ang), jnp.sin(ang)
    x1, x2 = jnp.split(x.astype(jnp.float32), 2, axis=-1)
    return jnp.concatenate([x1 * cos - x2 * sin, x1 * sin + x2 * cos], axis=-1).astype(x.dtype)


def latent_attention_block(q_lat, q_rope, c_kv, k_rope, q_pos, k_pos):
    s = (jnp.einsum('bthr,bsr->bhts', q_lat, c_kv, preferred_element_type=jnp.float32)
         + jnp.einsum('bthd,bsd->bhts', q_rope, k_rope, preferred_element_type=jnp.float32)) * SM_SCALE
    s = jnp.where(k_pos[None, None, None, :] <= q_pos[None, None, :, None], s, -jnp.inf)
    p = jax.nn.softmax(s, axis=-1).astype(c_kv.dtype)
    return jnp.einsum('bhts,bsr->bthr', p, c_kv)


def causal_latent_attention(q_lat, q_rope, c_kv, k_rope, q_pos, k_pos):
    b, t = q_lat.shape[:2]
    blk = Q_BLOCK if t % Q_BLOCK == 0 else t
    nb = t // blk
    ql = q_lat.reshape(b, nb, blk, N_HEADS, KV_LORA_RANK).swapaxes(0, 1)
    qr = q_rope.reshape(b, nb, blk, N_HEADS, QK_ROPE_DIM).swapaxes(0, 1)
    qp = q_pos.reshape(nb, blk)
    out = lax.map(lambda a: latent_attention_block(a[0], a[1], c_kv, k_rope, a[2], k_pos), (ql, qr, qp))
    return out.swapaxes(0, 1).reshape(b, t, N_HEADS, KV_LORA_RANK)


def multiscale_pool(u, prefix, q_pos):
    t = u.shape[1]
    ext = jnp.concatenate([prefix, u], axis=1)
    cs = jnp.cumsum(jnp.pad(ext.astype(jnp.float32), ((0, 0), (1, 0), (0, 0))), axis=1)
    L = POOL_STATE_LEN
    end = cs[:, L + 1:L + 1 + t]
    means = []
    for g, w in enumerate(POOL_WINDOWS):
        sl = slice(g * POOL_GROUP_DIM, (g + 1) * POOL_GROUP_DIM)
        start = cs[:, L + 1 - w:L + 1 - w + t, sl]
        cnt = jnp.minimum(w, q_pos + 1).astype(jnp.float32)[None, :, None]
        means.append((end[..., sl] - start) / cnt)
    pooled = (jnp.concatenate(means, axis=-1) - u.astype(jnp.float32)).astype(u.dtype)
    return pooled, ext[:, -L:]


def token_mixer(x, q_pos, past_c, past_kr, past_pos, pool_prefix, w_in, g_q_norm, g_kv_norm, w_q_up,
                w_k_up, w_v_up, w_pool_mix, pool_scale, w_branch_a, w_branch_b, w_out):
    b, t, _ = x.shape
    h = x @ w_in
    q_a, kv_a, kr, u, gate_a, gate_b = jnp.split(h, IN_SPLITS, axis=-1)
    q = jnp.einsum('btr,rhd->bthd', rms_norm(q_a, g_q_norm), w_q_up)
    q_nope = q[..., :QK_NOPE_DIM]
    q_rope = apply_rope(q[..., QK_NOPE_DIM:], q_pos)
    q_lat = jnp.einsum('bthd,rhd->bthr', q_nope, w_k_up)
    c_kv = rms_norm(kv_a, g_kv_norm)
    k_rope = apply_rope(kr[:, :, None, :], q_pos)[:, :, 0, :]
    keys_c = jnp.concatenate([past_c, c_kv], axis=1)
    keys_r = jnp.concatenate([past_kr, k_rope], axis=1)
    k_pos = jnp.concatenate([past_pos, q_pos])
    o_lat = causal_latent_attention(q_lat, q_rope, keys_c, keys_r, q_pos, k_pos)
    y_a = jnp.einsum('bthr,rhv->bthv', o_lat, w_v_up).reshape(b, t, ATTN_WIDTH)
    pooled, pool_tail = multiscale_pool(u, pool_prefix, q_pos)
    y_b = jnp.einsum('btgc,gcd->btgd', pooled.reshape(b, t, N_POOL_GROUPS, POOL_GROUP_DIM),
                     w_pool_mix).reshape(b, t, POOL_WIDTH) * pool_scale
    merged = jax.nn.sigmoid(gate_a) * (y_a @ w_branch_a) + jax.nn.sigmoid(gate_b) * (y_b @ w_branch_b)
    return merged @ w_out, c_kv, k_rope, pool_tail


def grouped_moe(x, w_router, router_bias, w_gate, w_up, w_down):
    b, t, d = x.shape
    xt = x.reshape(b * t, d)
    scores = jax.nn.sigmoid((xt @ w_router).astype(jnp.float32))
    grp = (scores + router_bias.astype(jnp.float32)).reshape(-1, N_EXPERT_GROUPS, EXPERTS_PER_GROUP)
    group_score = jnp.sum(lax.top_k(grp, 2)[0], axis=-1)
    g_mask = jnp.argmax(group_score, axis=-1)[:, None] == jnp.arange(N_EXPERT_GROUPS)
    sel = jnp.where(g_mask[:, :, None], grp, -jnp.inf).reshape(-1, N_EXPERTS)
    _, idx = lax.top_k(sel, TOP_K)
    wts = jnp.take_along_axis(scores, idx, axis=-1)
    wts = wts / jnp.sum(wts, axis=-1, keepdims=True)
    gate_mat = jnp.sum(jax.nn.one_hot(idx, N_EXPERTS, dtype=jnp.float32) * wts[..., None], axis=1)
    hid = jax.nn.silu(jnp.einsum('nd,edf->nef', xt, w_gate)) * jnp.einsum('nd,edf->nef', xt, w_up)
    hid = hid * gate_mat.astype(hid.dtype)[:, :, None]
    return jnp.einsum('nef,efd->nd', hid, w_down).reshape(b, t, d)


def setup_inputs(seed: int = 0) -> dict:
    key = jax.random.key(seed)
    ks = jax.random.split(key, 32)
    nrm = jax.random.normal
    n_pages = PAST_LEN // PAGE_SIZE
    n_used = DEC_BATCH * n_pages
    n_pool_pages = n_used + n_used // 4
    page_table = jax.random.permutation(ks[5], n_pool_pages)[:n_used].reshape(DEC_BATCH, n_pages).astype(jnp.int32)
    D = D_MODEL
    return {
        'x_prompt': nrm(ks[0], (BATCH, SEQ, D)),
        'x_sample': nrm(ks[1], (DEC_BATCH, DEC_SEQ, D)),
        'cache_kv_latent': nrm(ks[2], (DEPTH, n_pool_pages, PAGE_SIZE, KV_LORA_RANK)),
        'cache_k_rope': nrm(ks[3], (DEPTH, n_pool_pages, PAGE_SIZE, QK_ROPE_DIM)),
        'state_pool': nrm(ks[4], (DEPTH, DEC_BATCH, POOL_STATE_LEN, POOL_WIDTH)),
        'page_table': page_table,
        'w_in': nrm(ks[6], (DEPTH, D, IN_COLS)) * D ** -0.5,
        'g_q_norm': 1.0 + 0.02 * nrm(ks[7], (DEPTH, Q_LORA_RANK)),
        'g_kv_norm': 1.0 + 0.02 * nrm(ks[8], (DEPTH, KV_LORA_RANK)),
        'w_q_up': nrm(ks[9], (DEPTH, Q_LORA_RANK, N_HEADS, QK_NOPE_DIM + QK_ROPE_DIM)) * Q_LORA_RANK ** -0.5,
        'w_k_up': nrm(ks[10], (DEPTH, KV_LORA_RANK, N_HEADS, QK_NOPE_DIM)) * KV_LORA_RANK ** -0.5,
        'w_v_up': nrm(ks[11], (DEPTH, KV_LORA_RANK, N_HEADS, V_HEAD_DIM)) * KV_LORA_RANK ** -0.5,
        'w_pool_mix': nrm(ks[12], (DEPTH, N_POOL_GROUPS, POOL_GROUP_DIM, POOL_GROUP_DIM)) * POOL_GROUP_DIM ** -0.5,
        'pool_scale': 1.0 + 0.1 * nrm(ks[13], (DEPTH, POOL_WIDTH)),
        'w_branch_a': nrm(ks[14], (DEPTH, ATTN_WIDTH, D)) * ATTN_WIDTH ** -0.5 * DEEPNORM_BETA,
        'w_branch_b': nrm(ks[15], (DEPTH, POOL_WIDTH, D)) * POOL_WIDTH ** -0.5 * DEEPNORM_BETA,
        'w_out': nrm(ks[16], (DEPTH, D, D)) * D ** -0.5 * DEEPNORM_BETA,
        'ln1_g': 1.0 + 0.02 * nrm(ks[17], (DEPTH, D)),
        'ln1_b': 0.02 * nrm(ks[18], (DEPTH, D)),
        'w_router': nrm(ks[19], (D, N_EXPERTS)) * D ** -0.5,
        'router_bias': 0.01 * nrm(ks[20], (N_EXPERTS,)),
        'w_exp_gate': nrm(ks[21], (DEPTH, N_EXPERTS, D, D_EXPERT)) * D ** -0.5,
        'w_exp_up': nrm(ks[22], (DEPTH, N_EXPERTS, D, D_EXPERT)) * D ** -0.5,
        'w_exp_down': nrm(ks[23], (DEPTH, N_EXPERTS, D_EXPERT, D)) * D_EXPERT ** -0.5 * DEEPNORM_BETA,
        'ln2_g': 1.0 + 0.02 * nrm(ks[24], (DEPTH, D)),
        'ln2_b': 0.02 * nrm(ks[25], (DEPTH, D)),
    }


def reference(x_prompt, x_sample, cache_kv_latent, cache_k_rope, state_pool, page_table, w_in, g_q_norm,
              g_kv_norm, w_q_up, w_k_up, w_v_up, w_pool_mix, pool_scale, w_branch_a, w_branch_b, w_out,
              ln1_g, ln1_b, w_router, router_bias, w_exp_gate, w_exp_up, w_exp_down, ln2_g, ln2_b):
    bp, tp = x_prompt.shape[:2]
    bs, ts = x_sample.shape[:2]
    n_pages = page_table.shape[1]
    past_len = n_pages * PAGE_SIZE
    pos_p = jnp.arange(tp, dtype=jnp.int32)
    pos_s = past_len + jnp.arange(ts, dtype=jnp.int32)
    pos_past = jnp.arange(past_len, dtype=jnp.int32)
    empty_pos = jnp.zeros((0,), jnp.int32)
    x_p, x_s = x_prompt, x_sample
    lat_p, kr_p, pool_p, lat_s, kr_s, pool_s = [], [], [], [], [], []
    for l in range(DEPTH):
        lw = (w_in[l], g_q_norm[l], g_kv_norm[l], w_q_up[l], w_k_up[l], w_v_up[l], w_pool_mix[l],
              pool_scale[l], w_branch_a[l], w_branch_b[l], w_out[l])
        ew = (w_router, router_bias, w_exp_gate[l], w_exp_up[l], w_exp_down[l])
        mix, c_kv, k_rope, tail = token_mixer(
            x_p, pos_p, jnp.zeros((bp, 0, KV_LORA_RANK), x_p.dtype), jnp.zeros((bp, 0, QK_ROPE_DIM), x_p.dtype),
            empty_pos, jnp.zeros((bp, POOL_STATE_LEN, POOL_WIDTH), x_p.dtype), *lw)
        x_p = layer_norm(DEEPNORM_ALPHA * x_p + mix, ln1_g[l], ln1_b[l])
        x_p = layer_norm(DEEPNORM_ALPHA * x_p + grouped_moe(x_p, *ew), ln2_g[l], ln2_b[l])
        lat_p.append(c_kv)
        kr_p.append(k_rope)
        pool_p.append(tail)
        past_c = cache_kv_latent[l][page_table].reshape(bs, past_len, KV_LORA_RANK)
        past_kr = cache_k_rope[l][page_table].reshape(bs, past_len, QK_ROPE_DIM)
        mix, c_kv, k_rope, tail = token_mixer(x_s, pos_s, past_c, past_kr, pos_past, state_pool[l], *lw)
        x_s = layer_norm(DEEPNORM_ALPHA * x_s + mix, ln1_g[l], ln1_b[l])
        x_s = layer_norm(DEEPNORM_ALPHA * x_s + grouped_moe(x_s, *ew), ln2_g[l], ln2_b[l])
        lat_s.append(c_kv)
        kr_s.append(k_rope)
        pool_s.append(tail)
    return (x_p, x_s, jnp.stack(lat_p), jnp.stack(kr_p), jnp.stack(pool_p),
            jnp.stack(lat_s), jnp.stack(kr_s), jnp.stack(pool_s))
```

```python
import functools

import jax
import jax.numpy as jnp
from jax import lax
from jax.experimental import pallas as pl
from jax.experimental.pallas import tpu as pltpu

F32 = jnp.float32
BF16 = jnp.bfloat16

N_HEADS = 8
QK_NOPE_DIM = 64
QK_ROPE_DIM = 32
Q_LORA_RANK = 256
KV_LORA_RANK = 128
QK_CAT = KV_LORA_RANK + QK_ROPE_DIM
ROPE_BASE = 10000.0
SM_SCALE = (QK_NOPE_DIM + QK_ROPE_DIM) ** -0.5
POOL_WINDOWS = (2, 4, 8, 16)
POOL_PAD = 16
N_EXPERTS = 16
N_EXPERT_GROUPS = 4
EXPERTS_PER_GROUP = 4
LN_EPS = 1e-5
RMS_EPS = 1e-6

VMEM_LIMIT_BYTES = 56 * 1024 * 1024


def _params(*semantics):
    return pltpu.CompilerParams(dimension_semantics=semantics, vmem_limit_bytes=VMEM_LIMIT_BYTES)


def _const_spec(shape):
    zeros = (0,) * len(shape)
    return pl.BlockSpec(shape, lambda *_: zeros)


def _split_bf16(a):
    hi = a.astype(BF16)
    lo = (a - hi.astype(F32)).astype(BF16)
    return hi, lo


def _dot3(a, b, dims):
    a_hi, a_lo = _split_bf16(a)
    b_hi, b_lo = _split_bf16(b)
    d = functools.partial(lax.dot_general, dimension_numbers=dims, preferred_element_type=F32)
    return d(a_hi, b_hi) + (d(a_hi, b_lo) + d(a_lo, b_hi))


_NN = (((1,), (0,)), ((), ()))
_NT = (((1,), (1,)), ((), ()))


def _layer_norm(y, g, b):
    mu = jnp.mean(y, axis=-1, keepdims=True)
    d = y - mu
    var = jnp.mean(d * d, axis=-1, keepdims=True)
    return d * lax.rsqrt(var + LN_EPS) * g + b


def _rms_norm(y, g):
    return y * lax.rsqrt(jnp.mean(y * y, axis=-1, keepdims=True) + RMS_EPS) * g


def _fold_kernel(wqn_ref, wk_ref, wv_ref, wba_ref, wlat_ref, wva_ref):
    wlat_ref[...] = (_dot3(wqn_ref[0], wk_ref[0], _NN) * SM_SCALE).astype(BF16)
    wva_ref[...] = _dot3(wv_ref[0], wba_ref[0], _NN).astype(BF16)


def _fold_weights(wq_nope, wk_t, wv, wba):
    h, r, dn = wq_nope.shape
    d = wba.shape[-1]
    return pl.pallas_call(
        _fold_kernel,
        grid=(h,),
        in_specs=[pl.BlockSpec((1, r, dn), lambda i: (i, 0, 0)),
                  pl.BlockSpec((1, dn, KV_LORA_RANK), lambda i: (i, 0, 0)),
                  pl.BlockSpec((1, KV_LORA_RANK, wv.shape[-1]), lambda i: (i, 0, 0)),
                  pl.BlockSpec((1, wba.shape[1], d), lambda i: (i, 0, 0))],
        out_specs=[pl.BlockSpec((r, KV_LORA_RANK), lambda i: (0, i)),
                   pl.BlockSpec((KV_LORA_RANK, d), lambda i: (i, 0))],
        out_shape=[jax.ShapeDtypeStruct((r, h * KV_LORA_RANK), BF16),
                   jax.ShapeDtypeStruct((h * KV_LORA_RANK, d), BF16)],
        compiler_params=_params("arbitrary"),
        name="fold",
    )(wq_nope, wk_t, wv, wba)


def _proj_kernel(x_ref, w1_ref, wlat_ref, wqr_ref, gq_ref, gkv_ref, cos_ref, sin_ref,
                 q_ref, kvcat_ref, ckv_ref, krope_ref):
    x = x_ref[...].astype(BF16)
    h = jnp.dot(x, w1_ref[...], preferred_element_type=F32)
    qa = h[:, :Q_LORA_RANK]
    kva = h[:, Q_LORA_RANK:Q_LORA_RANK + KV_LORA_RANK]
    kr = h[:, Q_LORA_RANK + KV_LORA_RANK:]
    qn = _rms_norm(qa, gq_ref[...]).astype(BF16)
    ql = jnp.dot(qn, wlat_ref[...], preferred_element_type=F32)
    qr = jnp.dot(qn, wqr_ref[...], preferred_element_type=F32)
    cos = cos_ref[...]
    sin = sin_ref[...]
    half = QK_ROPE_DIM // 2
    width = qr.shape[1]
    lane = lax.broadcasted_iota(jnp.int32, qr.shape, 1)
    swapped = jnp.where((lane % QK_ROPE_DIM) < half,
                        pltpu.roll(qr, width - half, 1), pltpu.roll(qr, half, 1))
    qrr = qr * cos + swapped * sin
    for hh in range(N_HEADS):
        q_ref[hh, :, :KV_LORA_RANK] = ql[:, hh * KV_LORA_RANK:(hh + 1) * KV_LORA_RANK].astype(q_ref.dtype)
        q_ref[hh, :, KV_LORA_RANK:] = qrr[:, hh * QK_ROPE_DIM:(hh + 1) * QK_ROPE_DIM].astype(q_ref.dtype)
    ckv = _rms_norm(kva, gkv_ref[...])
    kr_swapped = jnp.concatenate([kr[:, half:], kr[:, :half]], axis=1)
    krope = kr * cos[:, :QK_ROPE_DIM] + kr_swapped * sin[:, :QK_ROPE_DIM]
    ckv_ref[...] = ckv
    krope_ref[...] = krope
    kvcat_ref[:, :KV_LORA_RANK] = ckv.astype(kvcat_ref.dtype)
    kvcat_ref[:, KV_LORA_RANK:] = krope.astype(kvcat_ref.dtype)


def _proj(x, w1, wlat, wqr, gq, gkv, cos_t, sin_t, *, tm, act_dtype):
    n, d = x.shape
    table_tiles = cos_t.shape[0] // tm
    return pl.pallas_call(
        _proj_kernel,
        grid=(n // tm,),
        in_specs=[pl.BlockSpec((tm, d), lambda i: (i, 0)),
                  _const_spec(w1.shape), _const_spec(wlat.shape), _const_spec(wqr.shape),
                  _const_spec(gq.shape), _const_spec(gkv.shape),
                  pl.BlockSpec((tm, cos_t.shape[1]), lambda i: (i % table_tiles, 0)),
                  pl.BlockSpec((tm, sin_t.shape[1]), lambda i: (i % table_tiles, 0))],
        out_specs=[pl.BlockSpec((N_HEADS, tm, QK_CAT), lambda i: (0, i, 0)),
                   pl.BlockSpec((tm, QK_CAT), lambda i: (i, 0)),
                   pl.BlockSpec((tm, KV_LORA_RANK), lambda i: (i, 0)),
                   pl.BlockSpec((tm, QK_ROPE_DIM), lambda i: (i, 0))],
        out_shape=[jax.ShapeDtypeStruct((N_HEADS, n, QK_CAT), act_dtype),
                   jax.ShapeDtypeStruct((n, QK_CAT), act_dtype),
                   jax.ShapeDtypeStruct((n, KV_LORA_RANK), F32),
                   jax.ShapeDtypeStruct((n, QK_ROPE_DIM), F32)],
        compiler_params=_params("arbitrary"),
        name="proj",
    )(x, w1, wlat, wqr, gq, gkv, cos_t, sin_t)


def _attn_prompt_kernel(q_ref, kv_ref, o_ref, m_ref, l_ref, acc_ref, *, tq):
    i = pl.program_id(1)
    rows = N_HEADS * tq
    q = q_ref[...].reshape(rows, QK_CAT)
    m_ref[...] = jnp.full(m_ref.shape, -jnp.inf, F32)
    l_ref[...] = jnp.zeros(l_ref.shape, F32)
    acc_ref[...] = jnp.zeros(acc_ref.shape, F32)

    def step(j, masked):
        k = kv_ref[pl.ds(pl.multiple_of(j * tq, tq), tq), :]
        s = lax.dot_general(q, k, _NT, preferred_element_type=F32)
        if masked:
            tok = lax.broadcasted_iota(jnp.int32, (N_HEADS, tq, tq), 1).reshape(rows, tq)
            key = lax.broadcasted_iota(jnp.int32, (rows, tq), 1)
            s = jnp.where(key <= tok, s, -jnp.inf)
        m_prev = m_ref[...]
        m_new = jnp.maximum(m_prev, jnp.max(s, axis=-1, keepdims=True))
        alpha = jnp.exp(m_prev - m_new)
        p = jnp.exp(s - m_new)
        l_ref[...] = alpha * l_ref[...] + jnp.sum(p, axis=-1, keepdims=True)
        pv = jnp.dot(p.astype(BF16), k[:, :KV_LORA_RANK], preferred_element_type=F32)
        acc_ref[...] = alpha * acc_ref[...] + pv
        m_ref[...] = m_new

    def body(j, carry):
        step(j, False)
        return carry

    lax.fori_loop(0, i, body, 0)
    step(i, True)
    o = acc_ref[...] / l_ref[...]
    for hh in range(N_HEADS):
        o_ref[:, hh * KV_LORA_RANK:(hh + 1) * KV_LORA_RANK] = o[hh * tq:(hh + 1) * tq].astype(o_ref.dtype)


def _attn_prompt(q, kvcat, *, batch, seq, tq):
    nq = seq // tq
    n = batch * seq
    rows = N_HEADS * tq
    return pl.pallas_call(
        functools.partial(_attn_prompt_kernel, tq=tq),
        grid=(batch, nq),
        in_specs=[pl.BlockSpec((N_HEADS, tq, QK_CAT), lambda b, i: (0, b * nq + i, 0)),
                  pl.BlockSpec((seq, QK_CAT), lambda b, i: (b, 0))],
        out_specs=pl.BlockSpec((tq, N_HEADS * KV_LORA_RANK), lambda b, i: (b * nq + i, 0)),
        out_shape=jax.ShapeDtypeStruct((n, N_HEADS * KV_LORA_RANK), BF16),
        scratch_shapes=[pltpu.VMEM((rows, 1), F32), pltpu.VMEM((rows, 1), F32),
                        pltpu.VMEM((rows, KV_LORA_RANK), F32)],
        compiler_params=_params("arbitrary", "arbitrary"),
        name="attn_prompt",
    )(q, kvcat)


def _attn_sample_kernel(pt_ref, q_ref, knew_ref, cache_c_ref, cache_r_ref, o_ref, kbuf, sem,
                        *, layer, n_pages, page, t_new):
    b = pl.program_id(0)
    nb = pl.num_programs(0)

    def page_copies(seq_idx, slot):
        copies = []
        for p in range(n_pages):
            pid = pt_ref[seq_idx, p]
            rows = pl.ds(p * page, page)
            copies.append(pltpu.make_async_copy(
                cache_c_ref.at[layer, pid], kbuf.at[slot, rows, pl.ds(0, KV_LORA_RANK)], sem.at[slot, 0]))
            copies.append(pltpu.make_async_copy(
                cache_r_ref.at[layer, pid], kbuf.at[slot, rows, pl.ds(KV_LORA_RANK, QK_ROPE_DIM)],
                sem.at[slot, 1]))
        return copies

    slot = b % 2

    @pl.when(b == 0)
    def _():
        for c in page_copies(0, 0):
            c.start()

    @pl.when(b + 1 < nb)
    def _():
        for c in page_copies(b + 1, 1 - slot):
            c.start()

    for c in page_copies(b, slot):
        c.wait()

    rows = N_HEADS * t_new
    q = q_ref[...].reshape(rows, QK_CAT).astype(BF16)
    kb = kbuf[slot].astype(BF16)
    kn = knew_ref[...].astype(BF16)
    s = lax.dot_general(q, kb, _NT, preferred_element_type=F32)
    sn = lax.dot_general(q, kn, _NT, preferred_element_type=F32)
    tok = lax.broadcasted_iota(jnp.int32, (N_HEADS, t_new, t_new), 1).reshape(rows, t_new)
    key = lax.broadcasted_iota(jnp.int32, (rows, t_new), 1)
    sn = jnp.where(key <= tok, sn, -jnp.inf)
    m = jnp.maximum(jnp.max(s, axis=-1, keepdims=True), jnp.max(sn, axis=-1, keepdims=True))
    p = jnp.exp(s - m)
    pn = jnp.exp(sn - m)
    l = jnp.sum(p, axis=-1, keepdims=True) + jnp.sum(pn, axis=-1, keepdims=True)
    o = (jnp.dot(p.astype(BF16), kb[:, :KV_LORA_RANK], preferred_element_type=F32)
         + jnp.dot(pn.astype(BF16), kn[:, :KV_LORA_RANK], preferred_element_type=F32)) / l
    for hh in range(N_HEADS):
        o_ref[:, hh * KV_LORA_RANK:(hh + 1) * KV_LORA_RANK] = o[hh * t_new:(hh + 1) * t_new]


def _attn_sample(page_table, q, knew, cache_c, cache_r, *, layer, t_new):
    n_seq, n_pages = page_table.shape
    page = cache_c.shape[2]
    past = n_pages * page
    grid_spec = pltpu.PrefetchScalarGridSpec(
        num_scalar_prefetch=1,
        grid=(n_seq,),
        in_specs=[pl.BlockSpec((N_HEADS, t_new, QK_CAT), lambda b, pt: (0, b, 0)),
                  pl.BlockSpec((t_new, QK_CAT), lambda b, pt: (b, 0)),
                  pl.BlockSpec(memory_space=pl.ANY),
                  pl.BlockSpec(memory_space=pl.ANY)],
        out_specs=pl.BlockSpec((t_new, N_HEADS * KV_LORA_RANK), lambda b, pt: (b, 0)),
        scratch_shapes=[pltpu.VMEM((2, past, QK_CAT), F32), pltpu.SemaphoreType.DMA((2, 2))],
    )
    return pl.pallas_call(
        functools.partial(_attn_sample_kernel, layer=layer, n_pages=n_pages, page=page, t_new=t_new),
        grid_spec=grid_spec,
        out_shape=jax.ShapeDtypeStruct((n_seq * t_new, N_HEADS * KV_LORA_RANK), F32),
        compiler_params=_params("arbitrary"),
        name="attn_sample",
    )(page_table, q, knew, cache_c, cache_r)


def _router_gates(x1, wr, rb):
    logits = _dot3(wr, x1, _NT)
    score = jax.nn.sigmoid(logits)
    biased = score + rb
    row = lambda a, e: a[e:e + 1, :]
    group_score = []
    for g in range(N_EXPERT_GROUPS):
        v = [row(biased, g * EXPERTS_PER_GROUP + k) for k in range(EXPERTS_PER_GROUP)]
        best = None
        for a in range(EXPERTS_PER_GROUP):
            for c in range(a + 1, EXPERTS_PER_GROUP):
                pair = v[a] + v[c]
                best = pair if best is None else jnp.maximum(best, pair)
        group_score.append(best)
    top = group_score[0]
    grp = jnp.zeros(top.shape, jnp.int32)
    for g in range(1, N_EXPERT_GROUPS):
        better = group_score[g] > top
        grp = jnp.where(better, g, grp)
        top = jnp.maximum(top, group_score[g])

    def pick(a, k):
        out = row(a, (N_EXPERT_GROUPS - 1) * EXPERTS_PER_GROUP + k)
        for g in range(N_EXPERT_GROUPS - 2, -1, -1):
            out = jnp.where(grp == g, row(a, g * EXPERTS_PER_GROUP + k), out)
        return out

    cand = [pick(biased, k) for k in range(EXPERTS_PER_GROUP)]
    aff = [pick(score, k) for k in range(EXPERTS_PER_GROUP)]
    first = jnp.zeros(top.shape, jnp.int32)
    best = cand[0]
    for k in range(1, EXPERTS_PER_GROUP):
        better = cand[k] > best
        first = jnp.where(better, k, first)
        best = jnp.maximum(best, cand[k])
    second = jnp.full(top.shape, -1, jnp.int32)
    best2 = jnp.full(top.shape, -jnp.inf, F32)
    for k in range(EXPERTS_PER_GROUP):
        better = (first != k) & ((second < 0) | (cand[k] > best2))
        second = jnp.where(better, k, second)
        best2 = jnp.where(better, cand[k], best2)
    chosen = [(first == k) | (second == k) for k in range(EXPERTS_PER_GROUP)]
    denom = None
    for k in range(EXPERTS_PER_GROUP):
        term = jnp.where(chosen[k], aff[k], 0.0)
        denom = term if denom is None else denom + term
    rows = []
    for g in range(N_EXPERT_GROUPS):
        for k in range(EXPERTS_PER_GROUP):
            rows.append(jnp.where((grp == g) & chosen[k], aff[k] / denom, 0.0))
    return jnp.concatenate(rows, axis=0)


def _mix_tail(x, o, pooled, gate_a, gate_b, wva_ref, wmix_ref, ps_ref, wbb_ref, wout_ref, g_ref, b_ref,
              wr_ref, rb_ref, x1_ref, gate_ref, *, alpha):
    gd = wmix_ref.shape[1]
    yb = jnp.concatenate(
        [jnp.dot(pooled[:, g * gd:(g + 1) * gd].astype(BF16), wmix_ref[g], preferred_element_type=F32)
         for g in range(len(POOL_WINDOWS))], axis=1) * ps_ref[...]
    ya = jnp.dot(o.astype(BF16), wva_ref[...], preferred_element_type=F32)
    yb = jnp.dot(yb.astype(BF16), wbb_ref[...], preferred_element_type=F32)
    merged = jax.nn.sigmoid(gate_a) * ya + jax.nn.sigmoid(gate_b) * yb
    mix = jnp.dot(merged.astype(BF16), wout_ref[...], preferred_element_type=F32)
    x1 = _layer_norm(alpha * x + mix, g_ref[...], b_ref[...])
    x1_ref[...] = x1
    gate_ref[...] = _router_gates(x1, wr_ref[...], rb_ref[...])


def _mix_prompt_kernel(x_ref, o_ref, w2_ref, wva_ref, wmix_ref, ps_ref, wbb_ref, wout_ref, g_ref, b_ref,
                       wr_ref, rb_ref, x1_ref, gate_ref, tail_ref, ext_ref, *, alpha, tiles_per_seq):
    tm = x_ref.shape[0]
    pw = ps_ref.shape[1]
    gd = pw // len(POOL_WINDOWS)
    x = x_ref[...]
    h2 = jnp.dot(x.astype(BF16), w2_ref[...], preferred_element_type=F32)
    u = h2[:, :pw]
    t = pl.program_id(0) % tiles_per_seq

    @pl.when(t == 0)
    def _():
        ext_ref[0:POOL_PAD, :] = jnp.zeros((POOL_PAD, pw), F32)

    @pl.when(t != 0)
    def _():
        ext_ref[0:POOL_PAD, :] = ext_ref[tm:tm + POOL_PAD, :]

    ext_ref[POOL_PAD:POOL_PAD + tm, :] = u
    tail_ref[0] = u[tm - POOL_PAD:, :]
    pos = t * tm + lax.broadcasted_iota(jnp.int32, (tm, 1), 0)
    pooled = []
    for g, w in enumerate(POOL_WINDOWS):
        acc = ext_ref[POOL_PAD:POOL_PAD + tm, g * gd:(g + 1) * gd]
        for j in range(1, w):
            acc = acc + ext_ref[POOL_PAD - j:POOL_PAD - j + tm, g * gd:(g + 1) * gd]
        cnt = jnp.minimum(w, pos + 1).astype(F32)
        pooled.append(acc / cnt - u[:, g * gd:(g + 1) * gd])
    pooled = jnp.concatenate(pooled, axis=1)
    d = x.shape[1]
    _mix_tail(x, o_ref[...], pooled, h2[:, pw:pw + d], h2[:, pw + d:], wva_ref, wmix_ref, ps_ref, wbb_ref,
              wout_ref, g_ref, b_ref, wr_ref, rb_ref, x1_ref, gate_ref, alpha=alpha)


def _mix_sample_kernel(x_ref, o_ref, pre_ref, w2_ref, wva_ref, wmix_ref, ps_ref, wbb_ref, wout_ref, g_ref,
                       b_ref, wr_ref, rb_ref, x1_ref, gate_ref, tail_ref, ext_ref, *, alpha, t_new, past):
    tm = x_ref.shape[0]
    sb = tm // t_new
    pw = ps_ref.shape[1]
    gd = pw // len(POOL_WINDOWS)
    x = x_ref[...]
    h2 = jnp.dot(x.astype(BF16), w2_ref[...], preferred_element_type=F32)
    u = h2[:, :pw]
    ext_ref[:, 0:POOL_PAD, :] = pre_ref[...]
    ext_ref[:, POOL_PAD:POOL_PAD + t_new, :] = u.reshape(sb, t_new, pw)
    tail_ref[...] = ext_ref[:, t_new:t_new + POOL_PAD, :]
    pos = past + lax.broadcasted_iota(jnp.int32, (sb, t_new, 1), 1)
    pooled = []
    for g, w in enumerate(POOL_WINDOWS):
        acc = ext_ref[:, POOL_PAD:POOL_PAD + t_new, g * gd:(g + 1) * gd]
        for j in range(1, w):
            acc = acc + ext_ref[:, POOL_PAD - j:POOL_PAD - j + t_new, g * gd:(g + 1) * gd]
        cnt = jnp.minimum(w, pos + 1).astype(F32)
        pooled.append((acc / cnt).reshape(tm, gd) - u[:, g * gd:(g + 1) * gd])
    pooled = jnp.concatenate(pooled, axis=1)
    d = x.shape[1]
    _mix_tail(x, o_ref[...], pooled, h2[:, pw:pw + d], h2[:, pw + d:], wva_ref, wmix_ref, ps_ref, wbb_ref,
              wout_ref, g_ref, b_ref, wr_ref, rb_ref, x1_ref, gate_ref, alpha=alpha)


def _mix(x, o, prefix, weights, *, tm, alpha, seq=None, t_new=None, past=None):
    n, d = x.shape
    w2, wva, wmix, ps, wbb, wout, g1, b1, wr, rb = weights
    pw = ps.shape[1]
    common_in = [_const_spec(a.shape) for a in weights]
    out_common = [pl.BlockSpec((tm, d), lambda i: (i, 0)),
                  pl.BlockSpec((N_EXPERTS, tm), lambda i: (0, i))]
    shape_common = [jax.ShapeDtypeStruct((n, d), F32), jax.ShapeDtypeStruct((N_EXPERTS, n), F32)]
    row_spec = lambda a: pl.BlockSpec((tm, a.shape[1]), lambda i: (i, 0))
    if prefix is None:
        tiles_per_seq = seq // tm
        return pl.pallas_call(
            functools.partial(_mix_prompt_kernel, alpha=alpha, tiles_per_seq=tiles_per_seq),
            grid=(n // tm,),
            in_specs=[row_spec(x), row_spec(o)] + common_in,
            out_specs=out_common + [pl.BlockSpec((1, POOL_PAD, pw), lambda i: (i // tiles_per_seq, 0, 0))],
            out_shape=shape_common + [jax.ShapeDtypeStruct((n // seq, POOL_PAD, pw), F32)],
            scratch_shapes=[pltpu.VMEM((tm + POOL_PAD, pw), F32)],
            compiler_params=_params("arbitrary"),
            name="mix_prompt",
        )(x, o, *weights)
    sb = tm // t_new
    return pl.pallas_call(
        functools.partial(_mix_sample_kernel, alpha=alpha, t_new=t_new, past=past),
        grid=(n // tm,),
        in_specs=[row_spec(x), row_spec(o), pl.BlockSpec((sb, POOL_PAD, pw), lambda i: (i, 0, 0))] + common_in,
        out_specs=out_common + [pl.BlockSpec((sb, POOL_PAD, pw), lambda i: (i, 0, 0))],
        out_shape=shape_common + [jax.ShapeDtypeStruct((n // t_new, POOL_PAD, pw), F32)],
        scratch_shapes=[pltpu.VMEM((sb, POOL_PAD + t_new, pw), F32)],
        compiler_params=_params("arbitrary"),
        name="mix_sample",
    )(x, o, prefix, *weights)


def _moe_kernel(x_ref, gate_ref, wg_ref, wu_ref, wd_ref, g_ref, b_ref, o_ref, acc_ref, *, alpha):
    e = pl.program_id(1)

    @pl.when(e == 0)
    def _():
        acc_ref[...] = jnp.zeros(acc_ref.shape, F32)

    x = x_ref[...]
    xb = x.astype(BF16)
    hg = jnp.dot(xb, wg_ref[0], preferred_element_type=F32)
    hu = jnp.dot(xb, wu_ref[0], preferred_element_type=F32)
    gate = gate_ref[...]
    lane = lax.broadcasted_iota(jnp.int32, gate.shape, 1)
    wt = jnp.sum(jnp.where(lane == e, gate, 0.0), axis=1, keepdims=True)
    hid = (jax.nn.silu(hg) * hu * wt).astype(BF16)
    acc_ref[...] += jnp.dot(hid, wd_ref[0], preferred_element_type=F32)

    @pl.when(e == pl.num_programs(1) - 1)
    def _():
        o_ref[...] = _layer_norm(alpha * x + acc_ref[...], g_ref[...], b_ref[...])


def _moe(x1, gate, wg, wu, wd, g2, b2, *, tm, alpha):
    n, d = x1.shape
    e, _, f = wg.shape
    return pl.pallas_call(
        functools.partial(_moe_kernel, alpha=alpha),
        grid=(n // tm, e),
        in_specs=[pl.BlockSpec((tm, d), lambda i, j: (i, 0)),
                  pl.BlockSpec((tm, e), lambda i, j: (i, 0)),
                  pl.BlockSpec((1, d, f), lambda i, j: (j, 0, 0)),
                  pl.BlockSpec((1, d, f), lambda i, j: (j, 0, 0)),
                  pl.BlockSpec((1, f, d), lambda i, j: (j, 0, 0)),
                  _const_spec(g2.shape), _const_spec(b2.shape)],
        out_specs=pl.BlockSpec((tm, d), lambda i, j: (i, 0)),
        out_shape=jax.ShapeDtypeStruct((n, d), F32),
        scratch_shapes=[pltpu.VMEM((tm, d), F32)],
        compiler_params=_params("arbitrary", "arbitrary"),
        name="moe",
    )(x1, gate, wg, wu, wd, g2, b2)


def _rope_tables(pos, reps):
    half = QK_ROPE_DIM // 2
    inv = ROPE_BASE ** (-jnp.arange(half, dtype=F32) / half)
    ang = pos.astype(F32)[:, None] * inv
    cos, sin = jnp.cos(ang), jnp.sin(ang)
    return (jnp.tile(jnp.concatenate([cos, cos], axis=1), (1, reps)),
            jnp.tile(jnp.concatenate([-sin, sin], axis=1), (1, reps)))


def kernel(x_prompt, x_sample, cache_kv_latent, cache_k_rope, state_pool, page_table, w_in, g_q_norm, g_kv_norm,
           w_q_up, w_k_up, w_v_up, w_pool_mix, pool_scale, w_branch_a, w_branch_b, w_out, ln1_g, ln1_b,
           w_router, router_bias, w_exp_gate, w_exp_up, w_exp_down, ln2_g, ln2_b):
    bp, tp, d = x_prompt.shape
    bs, ts, _ = x_sample.shape
    depth = w_in.shape[0]
    n_pages = page_table.shape[1]
    past = n_pages * cache_kv_latent.shape[2]
    alpha = (2 * depth) ** 0.25
    pw = pool_scale.shape[1]
    n_p, n_s = bp * tp, bs * ts
    s1 = Q_LORA_RANK + KV_LORA_RANK + QK_ROPE_DIM

    tm_p = min(512, tp)
    tq = min(256, tp)
    tm_mix_p = min(256, tp)
    tm_s = min(256, n_s)
    tm_moe_p = min(512, n_p)
    tm_moe_s = min(512, n_s)

    cos_p, sin_p = _rope_tables(jnp.arange(tp, dtype=jnp.int32), N_HEADS)
    cos_s, sin_s = _rope_tables(past + jnp.arange(ts, dtype=jnp.int32), N_HEADS)
    cos_s, sin_s = jnp.tile(cos_s, (tm_s // ts, 1)), jnp.tile(sin_s, (tm_s // ts, 1))
    prefix = jnp.pad(state_pool, ((0, 0), (0, 0), (POOL_PAD - state_pool.shape[2], 0), (0, 0)))
    wr_t = w_router.T
    rb = router_bias.reshape(N_EXPERTS, 1)

    x_p = x_prompt.reshape(n_p, d)
    x_s = x_sample.reshape(n_s, d)
    outs = [[] for _ in range(6)]
    for l in range(depth):
        wq = w_q_up[l]
        wq_nope = jnp.transpose(wq[:, :, :QK_NOPE_DIM], (1, 0, 2))
        wqr = (wq[:, :, QK_NOPE_DIM:] * SM_SCALE).reshape(Q_LORA_RANK, N_HEADS * QK_ROPE_DIM).astype(BF16)
        wk_t = jnp.transpose(w_k_up[l], (1, 2, 0))
        wv = jnp.transpose(w_v_up[l], (1, 0, 2))
        wba = w_branch_a[l].reshape(N_HEADS, -1, d)
        wlat, wva = _fold_weights(wq_nope, wk_t, wv, wba)
        w1 = w_in[l, :, :s1].astype(BF16)
        w2 = w_in[l, :, s1:].astype(BF16)
        gq = g_q_norm[l].reshape(1, -1)
        gkv = g_kv_norm[l].reshape(1, -1)
        mix_w = (w2, wva, w_pool_mix[l].astype(BF16), pool_scale[l].reshape(1, pw), w_branch_b[l].astype(BF16),
                 w_out[l].astype(BF16), ln1_g[l].reshape(1, d), ln1_b[l].reshape(1, d), wr_t, rb)
        wg, wu, wd = w_exp_gate[l].astype(BF16), w_exp_up[l].astype(BF16), w_exp_down[l].astype(BF16)
        g2, b2 = ln2_g[l].reshape(1, d), ln2_b[l].reshape(1, d)

        q, kvcat, ckv, krope = _proj(x_p, w1, wlat, wqr, gq, gkv, cos_p, sin_p, tm=tm_p, act_dtype=BF16)
        o = _attn_prompt(q, kvcat, batch=bp, seq=tp, tq=tq)
        x1, gate_t, tail = _mix(x_p, o, None, mix_w, tm=tm_mix_p, alpha=alpha, seq=tp)
        x_p = _moe(x1, gate_t.T, wg, wu, wd, g2, b2, tm=tm_moe_p, alpha=alpha)
        outs[0].append(ckv.reshape(bp, tp, -1))
        outs[1].append(krope.reshape(bp, tp, -1))
        outs[2].append(tail[:, 1:, :])

        q, kvcat, ckv, krope = _proj(x_s, w1, wlat, wqr, gq, gkv, cos_s, sin_s, tm=tm_s, act_dtype=F32)
        o = _attn_sample(page_table, q, kvcat, cache_kv_latent, cache_k_rope, layer=l, t_new=ts)
        x1, gate_t, tail = _mix(x_s, o, prefix[l], mix_w, tm=tm_s, alpha=alpha, t_new=ts, past=past)
        x_s = _moe(x1, gate_t.T, wg, wu, wd, g2, b2, tm=tm_moe_s, alpha=alpha)
        outs[3].append(ckv.reshape(bs, ts, -1))
        outs[4].append(krope.reshape(bs, ts, -1))
        outs[5].append(tail[:, 1:, :])

    return (x_p.reshape(bp, tp, d), x_s.reshape(bs, ts, d), jnp.stack(outs[0]), jnp.stack(outs[1]),
            jnp.stack(outs[2]), jnp.stack(outs[3]), jnp.stack(outs[4]), jnp.stack(outs[5]))
```

```python
import functools

import jax
import jax.numpy as jnp
from jax import lax
from jax.experimental import pallas as pl
from jax.experimental.pallas import tpu as pltpu

F32 = jnp.float32
BF16 = jnp.bfloat16

N_HEADS = 8
QK_NOPE_DIM = 64
QK_ROPE_DIM = 32
Q_LORA_RANK = 256
KV_LORA_RANK = 128
QK_CAT = KV_LORA_RANK + QK_ROPE_DIM
ROPE_BASE = 10000.0
SM_SCALE = (QK_NOPE_DIM + QK_ROPE_DIM) ** -0.5
LOG2_E = 1.4426950408889634
Q_SCALE = SM_SCALE * LOG2_E
POOL_WINDOWS = (2, 4, 8, 16)
POOL_PAD = 16
N_EXPERTS = 16
N_EXPERT_GROUPS = 4
EXPERTS_PER_GROUP = 4
LN_EPS = 1e-5
RMS_EPS = 1e-6

VMEM_LIMIT_BYTES = 56 * 1024 * 1024


def _params(*semantics):
    return pltpu.CompilerParams(dimension_semantics=semantics, vmem_limit_bytes=VMEM_LIMIT_BYTES)


def _const_spec(shape):
    zeros = (0,) * len(shape)
    return pl.BlockSpec(shape, lambda *_: zeros, pipeline_mode=pl.Buffered(1))


def _split_bf16(a):
    hi = a.astype(BF16)
    lo = (a - hi.astype(F32)).astype(BF16)
    return hi, lo


def _dot3(a, b, dims):
    a_hi, a_lo = _split_bf16(a)
    b_hi, b_lo = _split_bf16(b)
    d = functools.partial(lax.dot_general, dimension_numbers=dims, preferred_element_type=F32)
    return d(a_hi, b_hi) + (d(a_hi, b_lo) + d(a_lo, b_hi))


_NN = (((1,), (0,)), ((), ()))
_NT = (((1,), (1,)), ((), ()))


def _layer_norm(y, g, b):
    mu = jnp.mean(y, axis=-1, keepdims=True)
    d = y - mu
    var = jnp.mean(d * d, axis=-1, keepdims=True)
    return d * lax.rsqrt(var + LN_EPS) * g + b


def _rms_norm(y, g):
    return y * lax.rsqrt(jnp.mean(y * y, axis=-1, keepdims=True) + RMS_EPS) * g


def _fold_kernel(wqn_ref, wqn_t_ref, wk_ref, wk_t_ref, wv_ref, wba_ref, wlat_ref, wlat_t_ref, wva_ref):
    wlat_ref[...] = (_dot3(wqn_ref[0], wk_t_ref[0], _NN) * Q_SCALE).astype(BF16)
    wlat_t_ref[...] = (_dot3(wk_ref[0], wqn_t_ref[0], _NN) * Q_SCALE).astype(BF16)
    wva_ref[...] = _dot3(wv_ref[0], wba_ref[0], _NN).astype(BF16)


def _fold_weights(wq_nope, wk, wv, wba):
    h, r, dn = wq_nope.shape
    d = wba.shape[-1]
    head = lambda *blk: pl.BlockSpec((1,) + blk, lambda i: (i, 0, 0))
    return pl.pallas_call(
        _fold_kernel,
        grid=(h,),
        in_specs=[head(r, dn), head(dn, r), head(KV_LORA_RANK, dn), head(dn, KV_LORA_RANK),
                  head(KV_LORA_RANK, wv.shape[-1]), head(wba.shape[1], d)],
        out_specs=[pl.BlockSpec((r, KV_LORA_RANK), lambda i: (0, i)),
                   pl.BlockSpec((KV_LORA_RANK, r), lambda i: (i, 0)),
                   pl.BlockSpec((KV_LORA_RANK, d), lambda i: (i, 0))],
        out_shape=[jax.ShapeDtypeStruct((r, h * KV_LORA_RANK), BF16),
                   jax.ShapeDtypeStruct((h * KV_LORA_RANK, r), BF16),
                   jax.ShapeDtypeStruct((h * KV_LORA_RANK, d), BF16)],
        compiler_params=_params("arbitrary"),
        name="fold",
    )(wq_nope, jnp.swapaxes(wq_nope, 1, 2), wk, jnp.swapaxes(wk, 1, 2), wv, wba)


def _proj_common(x_ref, w1_ref, gq_ref, gkv_ref, cos_ref, sin_ref, ckv_ref, krope_ref):
    x = x_ref[...].astype(BF16)
    h = jnp.dot(x, w1_ref[...], preferred_element_type=F32)
    qa = h[:, :Q_LORA_RANK]
    kva = h[:, Q_LORA_RANK:Q_LORA_RANK + KV_LORA_RANK]
    kr = h[:, Q_LORA_RANK + KV_LORA_RANK:]
    qn = _rms_norm(qa, gq_ref[...])
    ckv = _rms_norm(kva, gkv_ref[...])
    half = QK_ROPE_DIM // 2
    kr_swapped = jnp.concatenate([kr[:, half:], kr[:, :half]], axis=1)
    krope = kr * cos_ref[:, :QK_ROPE_DIM] + kr_swapped * sin_ref[:, :QK_ROPE_DIM]
    ckv_ref[...] = ckv
    krope_ref[...] = krope
    return qn, ckv, krope


def _proj_sample_kernel(x_ref, w1_ref, wlat_ref, wqr_ref, gq_ref, gkv_ref, cos_ref, sin_ref,
                        q_ref, kvcat_ref, ckv_ref, krope_ref):
    qn, ckv, krope = _proj_common(x_ref, w1_ref, gq_ref, gkv_ref, cos_ref, sin_ref, ckv_ref, krope_ref)
    qn = qn.astype(BF16)
    ql = jnp.dot(qn, wlat_ref[...], preferred_element_type=F32)
    qr = jnp.dot(qn, wqr_ref[...], preferred_element_type=F32)
    half = QK_ROPE_DIM // 2
    width = qr.shape[1]
    lane = lax.broadcasted_iota(jnp.int32, qr.shape, 1)
    swapped = jnp.where((lane % QK_ROPE_DIM) < half,
                        pltpu.roll(qr, width - half, 1), pltpu.roll(qr, half, 1))
    qrr = qr * cos_ref[...] + swapped * sin_ref[...]
    for hh in range(N_HEADS):
        q_ref[hh, :, :KV_LORA_RANK] = ql[:, hh * KV_LORA_RANK:(hh + 1) * KV_LORA_RANK]
        q_ref[hh, :, KV_LORA_RANK:] = qrr[:, hh * QK_ROPE_DIM:(hh + 1) * QK_ROPE_DIM]
    kvcat_ref[:, :KV_LORA_RANK] = ckv
    kvcat_ref[:, KV_LORA_RANK:] = krope


def _proj_prompt_kernel(x_ref, w1_ref, wlat_t_ref, wqr_t_ref, gq_ref, gkv_ref, cos_ref, sin_ref,
                        cos_t_ref, sin_t_ref, qt_ref, kvcat_ref, ckv_t_ref, ckv_ref, krope_ref):
    qn, ckv, krope = _proj_common(x_ref, w1_ref, gq_ref, gkv_ref, cos_ref, sin_ref, ckv_ref, krope_ref)
    qn_t = qn.T.astype(BF16)
    ql_t = jnp.dot(wlat_t_ref[...], qn_t, preferred_element_type=F32)
    qr_t = jnp.dot(wqr_t_ref[...], qn_t, preferred_element_type=F32)
    half = QK_ROPE_DIM // 2
    pieces = []
    for hh in range(N_HEADS):
        base = hh * QK_ROPE_DIM
        pieces += [qr_t[base + half:base + QK_ROPE_DIM], qr_t[base:base + half]]
    qrr_t = qr_t * cos_t_ref[...] + jnp.concatenate(pieces, axis=0) * sin_t_ref[...]
    for hh in range(N_HEADS):
        qt_ref[hh, :KV_LORA_RANK, :] = ql_t[hh * KV_LORA_RANK:(hh + 1) * KV_LORA_RANK].astype(BF16)
        qt_ref[hh, KV_LORA_RANK:, :] = qrr_t[hh * QK_ROPE_DIM:(hh + 1) * QK_ROPE_DIM].astype(BF16)
    kvcat_ref[:, :KV_LORA_RANK] = ckv.astype(BF16)
    kvcat_ref[:, KV_LORA_RANK:] = krope.astype(BF16)
    ckv_t_ref[...] = ckv.T.astype(BF16)


def _proj_sample(x, w1, wlat, wqr, gq, gkv, cos_t, sin_t, *, tm):
    n, d = x.shape
    row = lambda width: pl.BlockSpec((tm, width), lambda i: (i, 0))
    return pl.pallas_call(
        _proj_sample_kernel,
        grid=(n // tm,),
        in_specs=[row(d), _const_spec(w1.shape), _const_spec(wlat.shape), _const_spec(wqr.shape),
                  _const_spec(gq.shape), _const_spec(gkv.shape),
                  _const_spec(cos_t.shape), _const_spec(sin_t.shape)],
        out_specs=[pl.BlockSpec((N_HEADS, tm, QK_CAT), lambda i: (0, i, 0)),
                   row(QK_CAT), row(KV_LORA_RANK), row(QK_ROPE_DIM)],
        out_shape=[jax.ShapeDtypeStruct((N_HEADS, n, QK_CAT), F32),
                   jax.ShapeDtypeStruct((n, QK_CAT), F32),
                   jax.ShapeDtypeStruct((n, KV_LORA_RANK), F32),
                   jax.ShapeDtypeStruct((n, QK_ROPE_DIM), F32)],
        compiler_params=_params("arbitrary"),
        name="proj_sample",
    )(x, w1, wlat, wqr, gq, gkv, cos_t, sin_t)


def _proj_prompt(x, w1, wlat_t, wqr_t, gq, gkv, cos, sin, cos_t, sin_t, *, tm):
    n, d = x.shape
    tiles = cos.shape[0] // tm
    row = lambda width: pl.BlockSpec((tm, width), lambda i: (i, 0))
    return pl.pallas_call(
        _proj_prompt_kernel,
        grid=(n // tm,),
        in_specs=[row(d), _const_spec(w1.shape), _const_spec(wlat_t.shape), _const_spec(wqr_t.shape),
                  _const_spec(gq.shape), _const_spec(gkv.shape),
                  pl.BlockSpec((tm, cos.shape[1]), lambda i: (i % tiles, 0)),
                  pl.BlockSpec((tm, sin.shape[1]), lambda i: (i % tiles, 0)),
                  pl.BlockSpec((cos_t.shape[0], tm), lambda i: (0, i % tiles)),
                  pl.BlockSpec((sin_t.shape[0], tm), lambda i: (0, i % tiles))],
        out_specs=[pl.BlockSpec((N_HEADS, QK_CAT, tm), lambda i: (0, 0, i)),
                   row(QK_CAT),
                   pl.BlockSpec((KV_LORA_RANK, tm), lambda i: (0, i)),
                   row(KV_LORA_RANK), row(QK_ROPE_DIM)],
        out_shape=[jax.ShapeDtypeStruct((N_HEADS, QK_CAT, n), BF16),
                   jax.ShapeDtypeStruct((n, QK_CAT), BF16),
                   jax.ShapeDtypeStruct((KV_LORA_RANK, n), BF16),
                   jax.ShapeDtypeStruct((n, KV_LORA_RANK), F32),
                   jax.ShapeDtypeStruct((n, QK_ROPE_DIM), F32)],
        compiler_params=_params("arbitrary"),
        name="proj_prompt",
    )(x, w1, wlat_t, wqr_t, gq, gkv, cos, sin, cos_t, sin_t)


SCORE_AHEAD = 2
SCORE_SLOTS = 4


def _col_reduce(x, op, reduce_rows):
    rows = x.shape[0]
    while rows >= 32:
        x = x.reshape(4, rows // 4, x.shape[1])
        x = op(op(x[0], x[1]), op(x[2], x[3]))
        rows //= 4
    return reduce_rows(x, axis=0, keepdims=True)


def _attn_prompt_kernel(qt_ref, k_ref, vt_ref, o_ref, m_ref, l_ref, acc_ref, s_ref, *, tq, tk):
    i = pl.program_id(1)
    n_full = (i * tq) // tk
    m_ref[...] = jnp.full(m_ref.shape, -jnp.inf, F32)
    l_ref[...] = jnp.zeros(l_ref.shape, F32)
    acc_ref[...] = jnp.zeros(acc_ref.shape, F32)

    def scores(j, hh):
        k = k_ref[pl.ds(pl.multiple_of(j * tk, tk), tk), :]
        s_ref[hh % SCORE_SLOTS] = jnp.dot(k, qt_ref[hh], preferred_element_type=F32)

    def step(j, masked):
        vt = vt_ref[:, pl.ds(pl.multiple_of(j * tk, tk), tk)]
        if masked:
            key = j * tk + lax.broadcasted_iota(jnp.int32, (tk, tq), 0)
            tok = i * tq + lax.broadcasted_iota(jnp.int32, (tk, tq), 1)
            visible = key <= tok
        for hh in range(N_HEADS):
            ahead = hh + SCORE_AHEAD
            if ahead < N_HEADS:
                scores(j, ahead)
            elif not masked:
                scores(j + 1, ahead - N_HEADS)
            s = s_ref[hh % SCORE_SLOTS]
            if masked:
                s = jnp.where(visible, s, -jnp.inf)
            m_prev = m_ref[hh]
            m_new = jnp.maximum(m_prev, _col_reduce(s, jnp.maximum, jnp.max))
            alpha = jnp.exp2(m_prev - m_new)
            p = jnp.exp2(s - m_new)
            l_ref[hh] = alpha * l_ref[hh] + _col_reduce(p, jnp.add, jnp.sum)
            pv = jnp.dot(vt, p.astype(BF16), preferred_element_type=F32)
            acc_ref[hh] = alpha * acc_ref[hh] + pv
            m_ref[hh] = m_new

    def body(j, carry):
        step(j, False)
        return carry

    for hh in range(SCORE_AHEAD):
        scores(0, hh)
    lax.fori_loop(0, n_full, body, 0)
    step(n_full, True)
    for hh in range(N_HEADS):
        o_t = acc_ref[hh] / l_ref[hh]
        o_ref[:, hh * KV_LORA_RANK:(hh + 1) * KV_LORA_RANK] = o_t.T.astype(o_ref.dtype)


def _attn_prompt(qt, kvcat, ckv_t, *, batch, seq, tq, tk):
    nq = seq // tq
    n = batch * seq
    return pl.pallas_call(
        functools.partial(_attn_prompt_kernel, tq=tq, tk=tk),
        grid=(batch, nq),
        in_specs=[pl.BlockSpec((N_HEADS, QK_CAT, tq), lambda b, i: (0, 0, b * nq + i)),
                  pl.BlockSpec((seq, QK_CAT), lambda b, i: (b, 0)),
                  pl.BlockSpec((KV_LORA_RANK, seq), lambda b, i: (0, b))],
        out_specs=pl.BlockSpec((tq, N_HEADS * KV_LORA_RANK), lambda b, i: (b * nq + i, 0)),
        out_shape=jax.ShapeDtypeStruct((n, N_HEADS * KV_LORA_RANK), BF16),
        scratch_shapes=[pltpu.VMEM((N_HEADS, 1, tq), F32), pltpu.VMEM((N_HEADS, 1, tq), F32),
                        pltpu.VMEM((N_HEADS, KV_LORA_RANK, tq), F32), pltpu.VMEM((SCORE_SLOTS, tk, tq), F32)],
        compiler_params=_params("arbitrary", "arbitrary"),
        name="attn_prompt",
    )(qt, kvcat, ckv_t)


def _attn_sample_kernel(pt_ref, q_ref, knew_ref, cache_c_ref, cache_rt_ref, o_ref, cbuf, rbuf, sem,
                        *, layer, n_pages, page, t_new):
    b = pl.program_id(0)
    nb = pl.num_programs(0)

    def page_copies(seq_idx, slot):
        copies = []
        for p in range(n_pages):
            pid = pt_ref[seq_idx, p]
            copies.append(pltpu.make_async_copy(
                cache_c_ref.at[layer, pid], cbuf.at[slot, pl.ds(p * page, page), :], sem.at[slot, 0]))
            copies.append(pltpu.make_async_copy(
                cache_rt_ref.at[layer, pid], rbuf.at[slot, :, pl.ds(p * page, page)], sem.at[slot, 1]))
        return copies

    slot = b % 2

    @pl.when(b == 0)
    def _():
        for c in page_copies(0, 0):
            c.start()

    @pl.when(b + 1 < nb)
    def _():
        for c in page_copies(b + 1, 1 - slot):
            c.start()

    for c in page_copies(b, slot):
        c.wait()

    rows = N_HEADS * t_new
    q = q_ref[...].reshape(rows, QK_CAT).astype(BF16)
    kc = cbuf[slot].astype(BF16)
    krt = rbuf[slot].astype(BF16)
    kn = knew_ref[...].astype(BF16)
    s = (lax.dot_general(q[:, :KV_LORA_RANK], kc, _NT, preferred_element_type=F32)
         + jnp.dot(q[:, KV_LORA_RANK:], krt, preferred_element_type=F32))
    sn = lax.dot_general(q, kn, _NT, preferred_element_type=F32)
    tok = lax.broadcasted_iota(jnp.int32, (N_HEADS, t_new, t_new), 1).reshape(rows, t_new)
    key = lax.broadcasted_iota(jnp.int32, (rows, t_new), 1)
    sn = jnp.where(key <= tok, sn, -jnp.inf)
    m = jnp.maximum(jnp.max(s, axis=-1, keepdims=True), jnp.max(sn, axis=-1, keepdims=True))
    p = jnp.exp2(s - m)
    pn = jnp.exp2(sn - m)
    l = jnp.sum(p, axis=-1, keepdims=True) + jnp.sum(pn, axis=-1, keepdims=True)
    o = (jnp.dot(p.astype(BF16), kc, preferred_element_type=F32)
         + jnp.dot(pn.astype(BF16), kn[:, :KV_LORA_RANK], preferred_element_type=F32)) / l
    for hh in range(N_HEADS):
        o_ref[:, hh * KV_LORA_RANK:(hh + 1) * KV_LORA_RANK] = o[hh * t_new:(hh + 1) * t_new]


def _attn_sample(page_table, q, knew, cache_c, cache_rt, *, layer, t_new):
    n_seq, n_pages = page_table.shape
    page = cache_c.shape[2]
    past = n_pages * page
    grid_spec = pltpu.PrefetchScalarGridSpec(
        num_scalar_prefetch=1,
        grid=(n_seq,),
        in_specs=[pl.BlockSpec((N_HEADS, t_new, QK_CAT), lambda b, pt: (0, b, 0)),
                  pl.BlockSpec((t_new, QK_CAT), lambda b, pt: (b, 0)),
                  pl.BlockSpec(memory_space=pl.ANY),
                  pl.BlockSpec(memory_space=pl.ANY)],
        out_specs=pl.BlockSpec((t_new, N_HEADS * KV_LORA_RANK), lambda b, pt: (b, 0)),
        scratch_shapes=[pltpu.VMEM((2, past, KV_LORA_RANK), F32), pltpu.VMEM((2, QK_ROPE_DIM, past), F32),
                        pltpu.SemaphoreType.DMA((2, 2))],
    )
    return pl.pallas_call(
        functools.partial(_attn_sample_kernel, layer=layer, n_pages=n_pages, page=page, t_new=t_new),
        grid_spec=grid_spec,
        out_shape=jax.ShapeDtypeStruct((n_seq * t_new, N_HEADS * KV_LORA_RANK), F32),
        compiler_params=_params("arbitrary"),
        name="attn_sample",
    )(page_table, q, knew, cache_c, cache_rt)


def _router_gates(x1, wr, rb):
    logits = _dot3(wr, x1, _NT)
    score = jax.nn.sigmoid(logits)
    biased = score + rb
    row = lambda a, e: a[e:e + 1, :]
    group_score = []
    for g in range(N_EXPERT_GROUPS):
        v = [row(biased, g * EXPERTS_PER_GROUP + k) for k in range(EXPERTS_PER_GROUP)]
        best = None
        for a in range(EXPERTS_PER_GROUP):
            for c in range(a + 1, EXPERTS_PER_GROUP):
                pair = v[a] + v[c]
                best = pair if best is None else jnp.maximum(best, pair)
        group_score.append(best)
    top = group_score[0]
    grp = jnp.zeros(top.shape, jnp.int32)
    for g in range(1, N_EXPERT_GROUPS):
        better = group_score[g] > top
        grp = jnp.where(better, g, grp)
        top = jnp.maximum(top, group_score[g])

    def pick(a, k):
        out = row(a, (N_EXPERT_GROUPS - 1) * EXPERTS_PER_GROUP + k)
        for g in range(N_EXPERT_GROUPS - 2, -1, -1):
            out = jnp.where(grp == g, row(a, g * EXPERTS_PER_GROUP + k), out)
        return out

    cand = [pick(biased, k) for k in range(EXPERTS_PER_GROUP)]
    aff = [pick(score, k) for k in range(EXPERTS_PER_GROUP)]
    first = jnp.zeros(top.shape, jnp.int32)
    best = cand[0]
    for k in range(1, EXPERTS_PER_GROUP):
        better = cand[k] > best
        first = jnp.where(better, k, first)
        best = jnp.maximum(best, cand[k])
    second = jnp.full(top.shape, -1, jnp.int32)
    best2 = jnp.full(top.shape, -jnp.inf, F32)
    for k in range(EXPERTS_PER_GROUP):
        better = (first != k) & ((second < 0) | (cand[k] > best2))
        second = jnp.where(better, k, second)
        best2 = jnp.where(better, cand[k], best2)
    chosen = [(first == k) | (second == k) for k in range(EXPERTS_PER_GROUP)]
    denom = None
    for k in range(EXPERTS_PER_GROUP):
        term = jnp.where(chosen[k], aff[k], 0.0)
        denom = term if denom is None else denom + term
    rows = []
    for g in range(N_EXPERT_GROUPS):
        for k in range(EXPERTS_PER_GROUP):
            rows.append(jnp.where((grp == g) & chosen[k], aff[k] / denom, 0.0))
    return jnp.concatenate(rows, axis=0)


def _mix_tail(x, o, pooled, gate_a, gate_b, wva_ref, wmix_ref, ps_ref, wbb_ref, wout_ref, g_ref, b_ref,
              wr_ref, rb_ref, x1_ref, gate_ref, *, alpha):
    gd = wmix_ref.shape[1]
    yb = jnp.concatenate(
        [jnp.dot(pooled[:, g * gd:(g + 1) * gd].astype(BF16), wmix_ref[g], preferred_element_type=F32)
         for g in range(len(POOL_WINDOWS))], axis=1) * ps_ref[...]
    ya = jnp.dot(o.astype(BF16), wva_ref[...], preferred_element_type=F32)
    yb = jnp.dot(yb.astype(BF16), wbb_ref[...], preferred_element_type=F32)
    merged = jax.nn.sigmoid(gate_a) * ya + jax.nn.sigmoid(gate_b) * yb
    mix = jnp.dot(merged.astype(BF16), wout_ref[...], preferred_element_type=F32)
    x1 = _layer_norm(alpha * x + mix, g_ref[...], b_ref[...])
    x1_ref[...] = x1
    gate_ref[...] = _router_gates(x1, wr_ref[...], rb_ref[...])


def _mix_prompt_kernel(x_ref, o_ref, w2_ref, wva_ref, wmix_ref, ps_ref, wbb_ref, wout_ref, g_ref, b_ref,
                       wr_ref, rb_ref, x1_ref, gate_ref, tail_ref, ext_ref, *, alpha, tiles_per_seq):
    tm = x_ref.shape[0]
    pw = ps_ref.shape[1]
    gd = pw // len(POOL_WINDOWS)
    x = x_ref[...]
    h2 = jnp.dot(x.astype(BF16), w2_ref[...], preferred_element_type=F32)
    u = h2[:, :pw]
    t = pl.program_id(0) % tiles_per_seq

    @pl.when(t == 0)
    def _():
        ext_ref[0:POOL_PAD, :] = jnp.zeros((POOL_PAD, pw), F32)

    @pl.when(t != 0)
    def _():
        ext_ref[0:POOL_PAD, :] = ext_ref[tm:tm + POOL_PAD, :]

    ext_ref[POOL_PAD:POOL_PAD + tm, :] = u
    tail_ref[0] = u[tm - POOL_PAD:, :]
    pos = t * tm + lax.broadcasted_iota(jnp.int32, (tm, 1), 0)
    pooled = []
    for g, w in enumerate(POOL_WINDOWS):
        acc = ext_ref[POOL_PAD:POOL_PAD + tm, g * gd:(g + 1) * gd]
        for j in range(1, w):
            acc = acc + ext_ref[POOL_PAD - j:POOL_PAD - j + tm, g * gd:(g + 1) * gd]
        cnt = jnp.minimum(w, pos + 1).astype(F32)
        pooled.append(acc / cnt - u[:, g * gd:(g + 1) * gd])
    pooled = jnp.concatenate(pooled, axis=1)
    d = x.shape[1]
    _mix_tail(x, o_ref[...], pooled, h2[:, pw:pw + d], h2[:, pw + d:], wva_ref, wmix_ref, ps_ref, wbb_ref,
              wout_ref, g_ref, b_ref, wr_ref, rb_ref, x1_ref, gate_ref, alpha=alpha)


def _mix_sample_kernel(x_ref, o_ref, pre_ref, w2_ref, wva_ref, wmix_ref, ps_ref, wbb_ref, wout_ref, g_ref,
                       b_ref, wr_ref, rb_ref, x1_ref, gate_ref, tail_ref, ext_ref, *, alpha, t_new, past):
    tm = x_ref.shape[0]
    sb = tm // t_new
    pw = ps_ref.shape[1]
    gd = pw // len(POOL_WINDOWS)
    x = x_ref[...]
    h2 = jnp.dot(x.astype(BF16), w2_ref[...], preferred_element_type=F32)
    u = h2[:, :pw]
    ext_ref[:, 0:POOL_PAD, :] = pre_ref[...]
    ext_ref[:, POOL_PAD:POOL_PAD + t_new, :] = u.reshape(sb, t_new, pw)
    tail_ref[...] = ext_ref[:, t_new:t_new + POOL_PAD, :]
    pos = past + lax.broadcasted_iota(jnp.int32, (sb, t_new, 1), 1)
    pooled = []
    for g, w in enumerate(POOL_WINDOWS):
        acc = ext_ref[:, POOL_PAD:POOL_PAD + t_new, g * gd:(g + 1) * gd]
        for j in range(1, w):
            acc = acc + ext_ref[:, POOL_PAD - j:POOL_PAD - j + t_new, g * gd:(g + 1) * gd]
        cnt = jnp.minimum(w, pos + 1).astype(F32)
        pooled.append((acc / cnt).reshape(tm, gd) - u[:, g * gd:(g + 1) * gd])
    pooled = jnp.concatenate(pooled, axis=1)
    d = x.shape[1]
    _mix_tail(x, o_ref[...], pooled, h2[:, pw:pw + d], h2[:, pw + d:], wva_ref, wmix_ref, ps_ref, wbb_ref,
              wout_ref, g_ref, b_ref, wr_ref, rb_ref, x1_ref, gate_ref, alpha=alpha)


def _mix(x, o, prefix, weights, *, tm, alpha, seq=None, t_new=None, past=None):
    n, d = x.shape
    w2, wva, wmix, ps, wbb, wout, g1, b1, wr, rb = weights
    pw = ps.shape[1]
    common_in = [_const_spec(a.shape) for a in weights]
    out_common = [pl.BlockSpec((tm, d), lambda i: (i, 0)),
                  pl.BlockSpec((N_EXPERTS, tm), lambda i: (0, i))]
    shape_common = [jax.ShapeDtypeStruct((n, d), F32), jax.ShapeDtypeStruct((N_EXPERTS, n), F32)]
    row_spec = lambda a: pl.BlockSpec((tm, a.shape[1]), lambda i: (i, 0))
    if prefix is None:
        tiles_per_seq = seq // tm
        return pl.pallas_call(
            functools.partial(_mix_prompt_kernel, alpha=alpha, tiles_per_seq=tiles_per_seq),
            grid=(n // tm,),
            in_specs=[row_spec(x), row_spec(o)] + common_in,
            out_specs=out_common + [pl.BlockSpec((1, POOL_PAD, pw), lambda i: (i // tiles_per_seq, 0, 0))],
            out_shape=shape_common + [jax.ShapeDtypeStruct((n // seq, POOL_PAD, pw), F32)],
            scratch_shapes=[pltpu.VMEM((tm + POOL_PAD, pw), F32)],
            compiler_params=_params("arbitrary"),
            name="mix_prompt",
        )(x, o, *weights)
    sb = tm // t_new
    return pl.pallas_call(
        functools.partial(_mix_sample_kernel, alpha=alpha, t_new=t_new, past=past),
        grid=(n // tm,),
        in_specs=[row_spec(x), row_spec(o), pl.BlockSpec((sb, POOL_PAD, pw), lambda i: (i, 0, 0))] + common_in,
        out_specs=out_common + [pl.BlockSpec((sb, POOL_PAD, pw), lambda i: (i, 0, 0))],
        out_shape=shape_common + [jax.ShapeDtypeStruct((n // t_new, POOL_PAD, pw), F32)],
        scratch_shapes=[pltpu.VMEM((sb, POOL_PAD + t_new, pw), F32)],
        compiler_params=_params("arbitrary"),
        name="mix_sample",
    )(x, o, prefix, *weights)


def _moe_kernel(x_ref, gate_ref, wg_ref, wu_ref, wd_ref, g_ref, b_ref, o_ref, acc_ref, *, alpha):
    e = pl.program_id(1)

    @pl.when(e == 0)
    def _():
        acc_ref[...] = jnp.zeros(acc_ref.shape, F32)

    x = x_ref[...]
    xb = x.astype(BF16)
    hg = jnp.dot(xb, wg_ref[0], preferred_element_type=F32)
    hu = jnp.dot(xb, wu_ref[0], preferred_element_type=F32)
    gate = gate_ref[...]
    lane = lax.broadcasted_iota(jnp.int32, gate.shape, 1)
    wt = jnp.sum(jnp.where(lane == e, gate, 0.0), axis=1, keepdims=True)
    hid = (jax.nn.silu(hg) * hu * wt).astype(BF16)
    acc_ref[...] += jnp.dot(hid, wd_ref[0], preferred_element_type=F32)

    @pl.when(e == pl.num_programs(1) - 1)
    def _():
        o_ref[...] = _layer_norm(alpha * x + acc_ref[...], g_ref[...], b_ref[...])


def _moe(x1, gate, wg, wu, wd, g2, b2, *, tm, alpha):
    n, d = x1.shape
    e, _, f = wg.shape
    return pl.pallas_call(
        functools.partial(_moe_kernel, alpha=alpha),
        grid=(n // tm, e),
        in_specs=[pl.BlockSpec((tm, d), lambda i, j: (i, 0)),
                  pl.BlockSpec((tm, e), lambda i, j: (i, 0)),
                  pl.BlockSpec((1, d, f), lambda i, j: (j, 0, 0)),
                  pl.BlockSpec((1, d, f), lambda i, j: (j, 0, 0)),
                  pl.BlockSpec((1, f, d), lambda i, j: (j, 0, 0)),
                  _const_spec(g2.shape), _const_spec(b2.shape)],
        out_specs=pl.BlockSpec((tm, d), lambda i, j: (i, 0)),
        out_shape=jax.ShapeDtypeStruct((n, d), F32),
        scratch_shapes=[pltpu.VMEM((tm, d), F32)],
        compiler_params=_params("arbitrary", "arbitrary"),
        name="moe",
    )(x1, gate, wg, wu, wd, g2, b2)


def _rope_tables(pos, reps):
    half = QK_ROPE_DIM // 2
    inv = ROPE_BASE ** (-jnp.arange(half, dtype=F32) / half)
    ang = pos.astype(F32)[:, None] * inv
    cos, sin = jnp.cos(ang), jnp.sin(ang)
    return (jnp.tile(jnp.concatenate([cos, cos], axis=1), (1, reps)),
            jnp.tile(jnp.concatenate([-sin, sin], axis=1), (1, reps)))


def kernel(x_prompt, x_sample, cache_kv_latent, cache_k_rope, state_pool, page_table, w_in, g_q_norm, g_kv_norm,
           w_q_up, w_k_up, w_v_up, w_pool_mix, pool_scale, w_branch_a, w_branch_b, w_out, ln1_g, ln1_b,
           w_router, router_bias, w_exp_gate, w_exp_up, w_exp_down, ln2_g, ln2_b):
    bp, tp, d = x_prompt.shape
    bs, ts, _ = x_sample.shape
    depth = w_in.shape[0]
    n_pages = page_table.shape[1]
    past = n_pages * cache_kv_latent.shape[2]
    alpha = (2 * depth) ** 0.25
    pw = pool_scale.shape[1]
    n_p, n_s = bp * tp, bs * ts
    s1 = Q_LORA_RANK + KV_LORA_RANK + QK_ROPE_DIM

    tm_p = min(512, tp)
    tq = min(512, tp)
    tk = min(512, tp)
    tm_mix_p = min(512, tp)
    tm_s = min(256, n_s)
    tm_moe_p = min(512, n_p)
    tm_moe_s = min(512, n_s)

    cos_p, sin_p = _rope_tables(jnp.arange(tp, dtype=jnp.int32), N_HEADS)
    cos_pt, sin_pt = cos_p.T, sin_p.T
    cos_p, sin_p = cos_p[:, :QK_ROPE_DIM], sin_p[:, :QK_ROPE_DIM]
    cos_s, sin_s = _rope_tables(past + jnp.arange(ts, dtype=jnp.int32), N_HEADS)
    cos_s, sin_s = jnp.tile(cos_s, (tm_s // ts, 1)), jnp.tile(sin_s, (tm_s // ts, 1))
    prefix = jnp.pad(state_pool, ((0, 0), (0, 0), (POOL_PAD - state_pool.shape[2], 0), (0, 0)))
    cache_rt = jnp.swapaxes(cache_k_rope, 2, 3)
    wr_t = w_router.T
    rb = router_bias.reshape(N_EXPERTS, 1)

    x_p = x_prompt.reshape(n_p, d)
    x_s = x_sample.reshape(n_s, d)
    outs = [[] for _ in range(6)]
    for l in range(depth):
        wq = w_q_up[l]
        wq_nope = jnp.transpose(wq[:, :, :QK_NOPE_DIM], (1, 0, 2))
        wqr = (wq[:, :, QK_NOPE_DIM:] * Q_SCALE).reshape(Q_LORA_RANK, N_HEADS * QK_ROPE_DIM).astype(BF16)
        wk = jnp.transpose(w_k_up[l], (1, 0, 2))
        wv = jnp.transpose(w_v_up[l], (1, 0, 2))
        wba = w_branch_a[l].reshape(N_HEADS, -1, d)
        wlat, wlat_t, wva = _fold_weights(wq_nope, wk, wv, wba)
        w1 = w_in[l, :, :s1].astype(BF16)
        w2 = w_in[l, :, s1:].astype(BF16)
        gq = g_q_norm[l].reshape(1, -1)
        gkv = g_kv_norm[l].reshape(1, -1)
        mix_w = (w2, wva, w_pool_mix[l].astype(BF16), pool_scale[l].reshape(1, pw), w_branch_b[l].astype(BF16),
                 w_out[l].astype(BF16), ln1_g[l].reshape(1, d), ln1_b[l].reshape(1, d), wr_t, rb)
        wg, wu, wd = w_exp_gate[l].astype(BF16), w_exp_up[l].astype(BF16), w_exp_down[l].astype(BF16)
        g2, b2 = ln2_g[l].reshape(1, d), ln2_b[l].reshape(1, d)

        qt, kvcat, ckv_t, ckv, krope = _proj_prompt(x_p, w1, wlat_t, wqr.T, gq, gkv, cos_p, sin_p, cos_pt, sin_pt,
                                                    tm=tm_p)
        o = _attn_prompt(qt, kvcat, ckv_t, batch=bp, seq=tp, tq=tq, tk=tk)
        x1, gate_t, tail = _mix(x_p, o, None, mix_w, tm=tm_mix_p, alpha=alpha, seq=tp)
        x_p = _moe(x1, gate_t.T, wg, wu, wd, g2, b2, tm=tm_moe_p, alpha=alpha)
        outs[0].append(ckv.reshape(bp, tp, -1))
        outs[1].append(krope.reshape(bp, tp, -1))
        outs[2].append(tail[:, 1:, :])

        q, kvcat, ckv, krope = _proj_sample(x_s, w1, wlat, wqr, gq, gkv, cos_s, sin_s, tm=tm_s)
        o = _attn_sample(page_table, q, kvcat, cache_kv_latent, cache_rt, layer=l, t_new=ts)
        x1, gate_t, tail = _mix(x_s, o, prefix[l], mix_w, tm=tm_s, alpha=alpha, t_new=ts, past=past)
        x_s = _moe(x1, gate_t.T, wg, wu, wd, g2, b2, tm=tm_moe_s, alpha=alpha)
        outs[3].append(ckv.reshape(bs, ts, -1))
        outs[4].append(krope.reshape(bs, ts, -1))
        outs[5].append(tail[:, 1:, :])

    return (x_p.reshape(bp, tp, d), x_s.reshape(bs, ts, d), jnp.stack(outs[0]), jnp.stack(outs[1]),
            jnp.stack(outs[2]), jnp.stack(outs[3]), jnp.stack(outs[4]), jnp.stack(outs[5]))
```

```python
import functools

import jax
import jax.numpy as jnp
from jax import lax
from jax.experimental import pallas as pl
from jax.experimental.pallas import tpu as pltpu

F32 = jnp.float32
BF16 = jnp.bfloat16

N_HEADS = 8
QK_NOPE_DIM = 64
QK_ROPE_DIM = 32
Q_LORA_RANK = 256
KV_LORA_RANK = 128
QK_CAT = KV_LORA_RANK + QK_ROPE_DIM
ROPE_BASE = 10000.0
SM_SCALE = (QK_NOPE_DIM + QK_ROPE_DIM) ** -0.5
LOG2_E = 1.4426950408889634
Q_SCALE = SM_SCALE * LOG2_E
POOL_WINDOWS = (2, 4, 8, 16)
POOL_PAD = 16
N_EXPERTS = 16
N_EXPERT_GROUPS = 4
EXPERTS_PER_GROUP = 4
PAIRS_PER_GROUP = 6
PAIR_LO = (0, 0, 0, 1, 1, 2)
PAIR_HI = (1, 2, 3, 2, 3, 3)
N_BUCKETS = N_EXPERT_GROUPS * PAIRS_PER_GROUP
META_ROWS = 8
LN_EPS = 1e-5
RMS_EPS = 1e-6

VMEM_LIMIT_BYTES = 56 * 1024 * 1024


def _params(*semantics):
    return pltpu.CompilerParams(dimension_semantics=semantics, vmem_limit_bytes=VMEM_LIMIT_BYTES)


def _const_spec(shape):
    zeros = (0,) * len(shape)
    return pl.BlockSpec(shape, lambda *_: zeros, pipeline_mode=pl.Buffered(1))


def _split_bf16(a):
    hi = a.astype(BF16)
    lo = (a - hi.astype(F32)).astype(BF16)
    return hi, lo


def _dot3(a, b, dims):
    a_hi, a_lo = _split_bf16(a)
    b_hi, b_lo = _split_bf16(b)
    d = functools.partial(lax.dot_general, dimension_numbers=dims, preferred_element_type=F32)
    return d(a_hi, b_hi) + (d(a_hi, b_lo) + d(a_lo, b_hi))


_NN = (((1,), (0,)), ((), ()))
_NT = (((1,), (1,)), ((), ()))


def _layer_norm(y, g, b):
    mu = jnp.mean(y, axis=-1, keepdims=True)
    d = y - mu
    var = jnp.mean(d * d, axis=-1, keepdims=True)
    return d * lax.rsqrt(var + LN_EPS) * g + b


def _rms_norm(y, g):
    return y * lax.rsqrt(jnp.mean(y * y, axis=-1, keepdims=True) + RMS_EPS) * g


def _fold_kernel(wqn_ref, wqn_t_ref, wk_ref, wk_t_ref, wv_ref, wba_ref, wlat_ref, wlat_t_ref, wva_ref):
    wlat_ref[...] = (_dot3(wqn_ref[0], wk_t_ref[0], _NN) * Q_SCALE).astype(BF16)
    wlat_t_ref[...] = (_dot3(wk_ref[0], wqn_t_ref[0], _NN) * Q_SCALE).astype(BF16)
    wva_ref[...] = _dot3(wv_ref[0], wba_ref[0], _NN).astype(BF16)


def _fold_weights(wq_nope, wk, wv, wba):
    h, r, dn = wq_nope.shape
    d = wba.shape[-1]
    head = lambda *blk: pl.BlockSpec((1,) + blk, lambda i: (i, 0, 0))
    return pl.pallas_call(
        _fold_kernel,
        grid=(h,),
        in_specs=[head(r, dn), head(dn, r), head(KV_LORA_RANK, dn), head(dn, KV_LORA_RANK),
                  head(KV_LORA_RANK, wv.shape[-1]), head(wba.shape[1], d)],
        out_specs=[pl.BlockSpec((r, KV_LORA_RANK), lambda i: (0, i)),
                   pl.BlockSpec((KV_LORA_RANK, r), lambda i: (i, 0)),
                   pl.BlockSpec((KV_LORA_RANK, d), lambda i: (i, 0))],
        out_shape=[jax.ShapeDtypeStruct((r, h * KV_LORA_RANK), BF16),
                   jax.ShapeDtypeStruct((h * KV_LORA_RANK, r), BF16),
                   jax.ShapeDtypeStruct((h * KV_LORA_RANK, d), BF16)],
        compiler_params=_params("arbitrary"),
        name="fold",
    )(wq_nope, jnp.swapaxes(wq_nope, 1, 2), wk, jnp.swapaxes(wk, 1, 2), wv, wba)


def _proj_common(x_ref, w1_ref, gq_ref, gkv_ref, cos_ref, sin_ref, ckv_ref, krope_ref):
    x = x_ref[...].astype(BF16)
    h = jnp.dot(x, w1_ref[...], preferred_element_type=F32)
    qa = h[:, :Q_LORA_RANK]
    kva = h[:, Q_LORA_RANK:Q_LORA_RANK + KV_LORA_RANK]
    kr = h[:, Q_LORA_RANK + KV_LORA_RANK:]
    qn = _rms_norm(qa, gq_ref[...])
    ckv = _rms_norm(kva, gkv_ref[...])
    half = QK_ROPE_DIM // 2
    kr_swapped = jnp.concatenate([kr[:, half:], kr[:, :half]], axis=1)
    krope = kr * cos_ref[:, :QK_ROPE_DIM] + kr_swapped * sin_ref[:, :QK_ROPE_DIM]
    ckv_ref[...] = ckv
    krope_ref[...] = krope
    return qn, ckv, krope


def _proj_sample_kernel(x_ref, w1_ref, wlat_ref, wqr_ref, gq_ref, gkv_ref, cos_ref, sin_ref,
                        q_ref, kvcat_ref, ckv_ref, krope_ref):
    qn, ckv, krope = _proj_common(x_ref, w1_ref, gq_ref, gkv_ref, cos_ref, sin_ref, ckv_ref, krope_ref)
    qn = qn.astype(BF16)
    ql = jnp.dot(qn, wlat_ref[...], preferred_element_type=F32)
    qr = jnp.dot(qn, wqr_ref[...], preferred_element_type=F32)
    half = QK_ROPE_DIM // 2
    width = qr.shape[1]
    lane = lax.broadcasted_iota(jnp.int32, qr.shape, 1)
    swapped = jnp.where((lane % QK_ROPE_DIM) < half,
                        pltpu.roll(qr, width - half, 1), pltpu.roll(qr, half, 1))
    qrr = qr * cos_ref[...] + swapped * sin_ref[...]
    for hh in range(N_HEADS):
        q_ref[hh, :, :KV_LORA_RANK] = ql[:, hh * KV_LORA_RANK:(hh + 1) * KV_LORA_RANK]
        q_ref[hh, :, KV_LORA_RANK:] = qrr[:, hh * QK_ROPE_DIM:(hh + 1) * QK_ROPE_DIM]
    kvcat_ref[:, :KV_LORA_RANK] = ckv
    kvcat_ref[:, KV_LORA_RANK:] = krope


def _proj_prompt_kernel(x_ref, w1_ref, wlat_t_ref, wqr_t_ref, gq_ref, gkv_ref, cos_ref, sin_ref,
                        cos_t_ref, sin_t_ref, qt_ref, kvcat_ref, ckv_t_ref, ckv_ref, krope_ref):
    qn, ckv, krope = _proj_common(x_ref, w1_ref, gq_ref, gkv_ref, cos_ref, sin_ref, ckv_ref, krope_ref)
    qn_t = qn.T.astype(BF16)
    ql_t = jnp.dot(wlat_t_ref[...], qn_t, preferred_element_type=F32)
    qr_t = jnp.dot(wqr_t_ref[...], qn_t, preferred_element_type=F32)
    half = QK_ROPE_DIM // 2
    pieces = []
    for hh in range(N_HEADS):
        base = hh * QK_ROPE_DIM
        pieces += [qr_t[base + half:base + QK_ROPE_DIM], qr_t[base:base + half]]
    qrr_t = qr_t * cos_t_ref[...] + jnp.concatenate(pieces, axis=0) * sin_t_ref[...]
    for hh in range(N_HEADS):
        qt_ref[hh, :KV_LORA_RANK, :] = ql_t[hh * KV_LORA_RANK:(hh + 1) * KV_LORA_RANK].astype(BF16)
        qt_ref[hh, KV_LORA_RANK:, :] = qrr_t[hh * QK_ROPE_DIM:(hh + 1) * QK_ROPE_DIM].astype(BF16)
    kvcat_ref[:, :KV_LORA_RANK] = ckv.astype(BF16)
    kvcat_ref[:, KV_LORA_RANK:] = krope.astype(BF16)
    ckv_t_ref[...] = ckv.T.astype(BF16)


def _proj_sample(x, x_off, n, w1, wlat, wqr, gq, gkv, cos_t, sin_t, *, tm):
    d = x.shape[1]
    row = lambda width: pl.BlockSpec((tm, width), lambda i: (i, 0))
    return pl.pallas_call(
        _proj_sample_kernel,
        grid=(n // tm,),
        in_specs=[pl.BlockSpec((tm, d), lambda i: (i + x_off, 0)), _const_spec(w1.shape), _const_spec(wlat.shape), _const_spec(wqr.shape),
                  _const_spec(gq.shape), _const_spec(gkv.shape),
                  _const_spec(cos_t.shape), _const_spec(sin_t.shape)],
        out_specs=[pl.BlockSpec((N_HEADS, tm, QK_CAT), lambda i: (0, i, 0)),
                   row(QK_CAT), row(KV_LORA_RANK), row(QK_ROPE_DIM)],
        out_shape=[jax.ShapeDtypeStruct((N_HEADS, n, QK_CAT), F32),
                   jax.ShapeDtypeStruct((n, QK_CAT), F32),
                   jax.ShapeDtypeStruct((n, KV_LORA_RANK), F32),
                   jax.ShapeDtypeStruct((n, QK_ROPE_DIM), F32)],
        compiler_params=_params("arbitrary"),
        name="proj_sample",
    )(x, w1, wlat, wqr, gq, gkv, cos_t, sin_t)


def _proj_prompt(x, n, w1, wlat_t, wqr_t, gq, gkv, cos, sin, cos_t, sin_t, *, tm):
    d = x.shape[1]
    tiles = cos.shape[0] // tm
    row = lambda width: pl.BlockSpec((tm, width), lambda i: (i, 0))
    return pl.pallas_call(
        _proj_prompt_kernel,
        grid=(n // tm,),
        in_specs=[row(d), _const_spec(w1.shape), _const_spec(wlat_t.shape), _const_spec(wqr_t.shape),
                  _const_spec(gq.shape), _const_spec(gkv.shape),
                  pl.BlockSpec((tm, cos.shape[1]), lambda i: (i % tiles, 0)),
                  pl.BlockSpec((tm, sin.shape[1]), lambda i: (i % tiles, 0)),
                  pl.BlockSpec((cos_t.shape[0], tm), lambda i: (0, i % tiles)),
                  pl.BlockSpec((sin_t.shape[0], tm), lambda i: (0, i % tiles))],
        out_specs=[pl.BlockSpec((N_HEADS, QK_CAT, tm), lambda i: (0, 0, i)),
                   row(QK_CAT),
                   pl.BlockSpec((KV_LORA_RANK, tm), lambda i: (0, i)),
                   row(KV_LORA_RANK), row(QK_ROPE_DIM)],
        out_shape=[jax.ShapeDtypeStruct((N_HEADS, QK_CAT, n), BF16),
                   jax.ShapeDtypeStruct((n, QK_CAT), BF16),
                   jax.ShapeDtypeStruct((KV_LORA_RANK, n), BF16),
                   jax.ShapeDtypeStruct((n, KV_LORA_RANK), F32),
                   jax.ShapeDtypeStruct((n, QK_ROPE_DIM), F32)],
        compiler_params=_params("arbitrary"),
        name="proj_prompt",
    )(x, w1, wlat_t, wqr_t, gq, gkv, cos, sin, cos_t, sin_t)


SCORE_AHEAD = 2
SCORE_SLOTS = 4


def _col_reduce(x, op, reduce_rows):
    rows = x.shape[0]
    while rows >= 32:
        x = x.reshape(4, rows // 4, x.shape[1])
        x = op(op(x[0], x[1]), op(x[2], x[3]))
        rows //= 4
    return reduce_rows(x, axis=0, keepdims=True)


def _attn_prompt_kernel(qt_ref, k_ref, vt_ref, o_ref, m_ref, l_ref, acc_ref, s_ref, *, tq, tk):
    i = pl.program_id(1)
    n_full = (i * tq) // tk
    m_ref[...] = jnp.full(m_ref.shape, -jnp.inf, F32)
    l_ref[...] = jnp.zeros(l_ref.shape, F32)
    acc_ref[...] = jnp.zeros(acc_ref.shape, F32)

    def scores(j, hh):
        k = k_ref[pl.ds(pl.multiple_of(j * tk, tk), tk), :]
        s_ref[hh % SCORE_SLOTS] = jnp.dot(k, qt_ref[hh], preferred_element_type=F32)

    def step(j, masked):
        vt = vt_ref[:, pl.ds(pl.multiple_of(j * tk, tk), tk)]
        if masked:
            key = j * tk + lax.broadcasted_iota(jnp.int32, (tk, tq), 0)
            tok = i * tq + lax.broadcasted_iota(jnp.int32, (tk, tq), 1)
            visible = key <= tok
        for hh in range(N_HEADS):
            ahead = hh + SCORE_AHEAD
            if ahead < N_HEADS:
                scores(j, ahead)
            elif not masked:
                scores(j + 1, ahead - N_HEADS)
            s = s_ref[hh % SCORE_SLOTS]
            if masked:
                s = jnp.where(visible, s, -jnp.inf)
            m_prev = m_ref[hh]
            m_new = jnp.maximum(m_prev, _col_reduce(s, jnp.maximum, jnp.max))
            alpha = jnp.exp2(m_prev - m_new)
            p = jnp.exp2(s - m_new)
            l_ref[hh] = alpha * l_ref[hh] + _col_reduce(p, jnp.add, jnp.sum)
            pv = jnp.dot(vt, p.astype(BF16), preferred_element_type=F32)
            acc_ref[hh] = alpha * acc_ref[hh] + pv
            m_ref[hh] = m_new

    def body(j, carry):
        step(j, False)
        return carry

    for hh in range(SCORE_AHEAD):
        scores(0, hh)
    lax.fori_loop(0, n_full, body, 0)
    step(n_full, True)
    for hh in range(N_HEADS):
        o_t = acc_ref[hh] / l_ref[hh]
        o_ref[:, hh * KV_LORA_RANK:(hh + 1) * KV_LORA_RANK] = o_t.T.astype(o_ref.dtype)


def _attn_prompt(qt, kvcat, ckv_t, *, batch, seq, tq, tk):
    nq = seq // tq
    n = batch * seq
    return pl.pallas_call(
        functools.partial(_attn_prompt_kernel, tq=tq, tk=tk),
        grid=(batch, nq),
        in_specs=[pl.BlockSpec((N_HEADS, QK_CAT, tq), lambda b, i: (0, 0, b * nq + i)),
                  pl.BlockSpec((seq, QK_CAT), lambda b, i: (b, 0)),
                  pl.BlockSpec((KV_LORA_RANK, seq), lambda b, i: (0, b))],
        out_specs=pl.BlockSpec((tq, N_HEADS * KV_LORA_RANK), lambda b, i: (b * nq + i, 0)),
        out_shape=jax.ShapeDtypeStruct((n, N_HEADS * KV_LORA_RANK), BF16),
        scratch_shapes=[pltpu.VMEM((N_HEADS, 1, tq), F32), pltpu.VMEM((N_HEADS, 1, tq), F32),
                        pltpu.VMEM((N_HEADS, KV_LORA_RANK, tq), F32), pltpu.VMEM((SCORE_SLOTS, tk, tq), F32)],
        compiler_params=_params("arbitrary", "arbitrary"),
        name="attn_prompt",
    )(qt, kvcat, ckv_t)


def _attn_sample_kernel(pt_ref, q_ref, knew_ref, cache_c_ref, cache_rt_ref, o_ref, cbuf, rbuf, sem,
                        *, layer, n_pages, page, t_new):
    b = pl.program_id(0)
    nb = pl.num_programs(0)

    def page_copies(seq_idx, slot):
        copies = []
        for p in range(n_pages):
            pid = pt_ref[seq_idx, p]
            copies.append(pltpu.make_async_copy(
                cache_c_ref.at[layer, pid], cbuf.at[slot, pl.ds(p * page, page), :], sem.at[slot, 0]))
            copies.append(pltpu.make_async_copy(
                cache_rt_ref.at[layer, pid], rbuf.at[slot, :, pl.ds(p * page, page)], sem.at[slot, 1]))
        return copies

    slot = b % 2

    @pl.when(b == 0)
    def _():
        for c in page_copies(0, 0):
            c.start()

    @pl.when(b + 1 < nb)
    def _():
        for c in page_copies(b + 1, 1 - slot):
            c.start()

    for c in page_copies(b, slot):
        c.wait()

    rows = N_HEADS * t_new
    q = q_ref[...].reshape(rows, QK_CAT).astype(BF16)
    kc = cbuf[slot].astype(BF16)
    krt = rbuf[slot].astype(BF16)
    kn = knew_ref[...].astype(BF16)
    s = (lax.dot_general(q[:, :KV_LORA_RANK], kc, _NT, preferred_element_type=F32)
         + jnp.dot(q[:, KV_LORA_RANK:], krt, preferred_element_type=F32))
    sn = lax.dot_general(q, kn, _NT, preferred_element_type=F32)
    tok = lax.broadcasted_iota(jnp.int32, (N_HEADS, t_new, t_new), 1).reshape(rows, t_new)
    key = lax.broadcasted_iota(jnp.int32, (rows, t_new), 1)
    sn = jnp.where(key <= tok, sn, -jnp.inf)
    m = jnp.maximum(jnp.max(s, axis=-1, keepdims=True), jnp.max(sn, axis=-1, keepdims=True))
    p = jnp.exp2(s - m)
    pn = jnp.exp2(sn - m)
    l = jnp.sum(p, axis=-1, keepdims=True) + jnp.sum(pn, axis=-1, keepdims=True)
    o = (jnp.dot(p.astype(BF16), kc, preferred_element_type=F32)
         + jnp.dot(pn.astype(BF16), kn[:, :KV_LORA_RANK], preferred_element_type=F32)) / l
    for hh in range(N_HEADS):
        o_ref[:, hh * KV_LORA_RANK:(hh + 1) * KV_LORA_RANK] = o[hh * t_new:(hh + 1) * t_new]


def _attn_sample(page_table, q, knew, cache_c, cache_rt, *, layer, t_new):
    n_seq, n_pages = page_table.shape
    page = cache_c.shape[2]
    past = n_pages * page
    grid_spec = pltpu.PrefetchScalarGridSpec(
        num_scalar_prefetch=1,
        grid=(n_seq,),
        in_specs=[pl.BlockSpec((N_HEADS, t_new, QK_CAT), lambda b, pt: (0, b, 0)),
                  pl.BlockSpec((t_new, QK_CAT), lambda b, pt: (b, 0)),
                  pl.BlockSpec(memory_space=pl.ANY),
                  pl.BlockSpec(memory_space=pl.ANY)],
        out_specs=pl.BlockSpec((t_new, N_HEADS * KV_LORA_RANK), lambda b, pt: (b, 0)),
        scratch_shapes=[pltpu.VMEM((2, past, KV_LORA_RANK), F32), pltpu.VMEM((2, QK_ROPE_DIM, past), F32),
                        pltpu.SemaphoreType.DMA((2, 2))],
    )
    return pl.pallas_call(
        functools.partial(_attn_sample_kernel, layer=layer, n_pages=n_pages, page=page, t_new=t_new),
        grid_spec=grid_spec,
        out_shape=jax.ShapeDtypeStruct((n_seq * t_new, N_HEADS * KV_LORA_RANK), F32),
        compiler_params=_params("arbitrary"),
        name="attn_sample",
    )(page_table, q, knew, cache_c, cache_rt)


def _router_meta(x1, wr, rb):
    logits = _dot3(wr, x1, _NT)
    score = jax.nn.sigmoid(logits)
    biased = score + rb
    row = lambda a, e: a[e:e + 1, :]
    group_score = []
    for g in range(N_EXPERT_GROUPS):
        v = [row(biased, g * EXPERTS_PER_GROUP + k) for k in range(EXPERTS_PER_GROUP)]
        best = None
        for a in range(EXPERTS_PER_GROUP):
            for c in range(a + 1, EXPERTS_PER_GROUP):
                pair = v[a] + v[c]
                best = pair if best is None else jnp.maximum(best, pair)
        group_score.append(best)
    top = group_score[0]
    grp = jnp.zeros(top.shape, jnp.int32)
    for g in range(1, N_EXPERT_GROUPS):
        better = group_score[g] > top
        grp = jnp.where(better, g, grp)
        top = jnp.maximum(top, group_score[g])

    def pick(a, k):
        out = row(a, (N_EXPERT_GROUPS - 1) * EXPERTS_PER_GROUP + k)
        for g in range(N_EXPERT_GROUPS - 2, -1, -1):
            out = jnp.where(grp == g, row(a, g * EXPERTS_PER_GROUP + k), out)
        return out

    cand = [pick(biased, k) for k in range(EXPERTS_PER_GROUP)]
    aff = [pick(score, k) for k in range(EXPERTS_PER_GROUP)]
    first = jnp.zeros(top.shape, jnp.int32)
    best = cand[0]
    for k in range(1, EXPERTS_PER_GROUP):
        better = cand[k] > best
        first = jnp.where(better, k, first)
        best = jnp.maximum(best, cand[k])
    second = jnp.full(top.shape, -1, jnp.int32)
    best2 = jnp.full(top.shape, -jnp.inf, F32)
    for k in range(EXPERTS_PER_GROUP):
        better = (first != k) & ((second < 0) | (cand[k] > best2))
        second = jnp.where(better, k, second)
        best2 = jnp.where(better, cand[k], best2)
    lo = jnp.minimum(first, second)
    hi = jnp.maximum(first, second)

    def take(vals, k_idx):
        out = vals[EXPERTS_PER_GROUP - 1]
        for k in range(EXPERTS_PER_GROUP - 2, -1, -1):
            out = jnp.where(k_idx == k, vals[k], out)
        return out

    aff_lo, aff_hi = take(aff, lo), take(aff, hi)
    denom = aff_lo + aff_hi
    pair = jnp.where(lo == 0, 0, jnp.where(lo == 1, 3, 5)) + (hi - lo - 1)
    bucket = (grp * PAIRS_PER_GROUP + pair).astype(F32)
    pad = jnp.zeros((META_ROWS - 3,) + top.shape[1:], F32)
    return jnp.concatenate([bucket, aff_lo / denom, aff_hi / denom, pad], axis=0)


def _mix_tail(x, o, pooled, gate_a, gate_b, wva_ref, wmix_ref, ps_ref, wbb_ref, wout_ref, g_ref, b_ref,
              wr_ref, rb_ref, x1_ref, gate_ref, *, alpha):
    gd = wmix_ref.shape[1]
    yb = jnp.concatenate(
        [jnp.dot(pooled[:, g * gd:(g + 1) * gd].astype(BF16), wmix_ref[g], preferred_element_type=F32)
         for g in range(len(POOL_WINDOWS))], axis=1) * ps_ref[...]
    ya = jnp.dot(o.astype(BF16), wva_ref[...], preferred_element_type=F32)
    yb = jnp.dot(yb.astype(BF16), wbb_ref[...], preferred_element_type=F32)
    merged = jax.nn.sigmoid(gate_a) * ya + jax.nn.sigmoid(gate_b) * yb
    mix = jnp.dot(merged.astype(BF16), wout_ref[...], preferred_element_type=F32)
    x1 = _layer_norm(alpha * x + mix, g_ref[...], b_ref[...])
    x1_ref[...] = x1
    gate_ref[...] = _router_meta(x1, wr_ref[...], rb_ref[...])


def _mix_prompt_kernel(*refs, alpha, tiles_per_seq, n_tiles):
    x1_ref, gate_ref = refs[-4], refs[-3]

    @pl.when(pl.program_id(0) < n_tiles)
    def _():
        _mix_prompt_tile(*refs, alpha=alpha, tiles_per_seq=tiles_per_seq)

    @pl.when(pl.program_id(0) >= n_tiles)
    def _():
        x1_ref[...] = jnp.zeros(x1_ref.shape, F32)
        gate_ref[...] = jnp.zeros(gate_ref.shape, F32)


def _mix_prompt_tile(x_ref, o_ref, w2_ref, wva_ref, wmix_ref, ps_ref, wbb_ref, wout_ref, g_ref, b_ref,
                     wr_ref, rb_ref, x1_ref, gate_ref, tail_ref, ext_ref, *, alpha, tiles_per_seq):
    tm = x_ref.shape[0]
    pw = ps_ref.shape[1]
    gd = pw // len(POOL_WINDOWS)
    x = x_ref[...]
    h2 = jnp.dot(x.astype(BF16), w2_ref[...], preferred_element_type=F32)
    u = h2[:, :pw]
    t = pl.program_id(0) % tiles_per_seq

    @pl.when(t == 0)
    def _():
        ext_ref[0:POOL_PAD, :] = jnp.zeros((POOL_PAD, pw), F32)

    @pl.when(t != 0)
    def _():
        ext_ref[0:POOL_PAD, :] = ext_ref[tm:tm + POOL_PAD, :]

    ext_ref[POOL_PAD:POOL_PAD + tm, :] = u
    tail_ref[0] = u[tm - POOL_PAD:, :]
    pos = t * tm + lax.broadcasted_iota(jnp.int32, (tm, 1), 0)
    pooled = []
    for g, w in enumerate(POOL_WINDOWS):
        acc = ext_ref[POOL_PAD:POOL_PAD + tm, g * gd:(g + 1) * gd]
        for j in range(1, w):
            acc = acc + ext_ref[POOL_PAD - j:POOL_PAD - j + tm, g * gd:(g + 1) * gd]
        cnt = jnp.minimum(w, pos + 1).astype(F32)
        pooled.append(acc / cnt - u[:, g * gd:(g + 1) * gd])
    pooled = jnp.concatenate(pooled, axis=1)
    d = x.shape[1]
    _mix_tail(x, o_ref[...], pooled, h2[:, pw:pw + d], h2[:, pw + d:], wva_ref, wmix_ref, ps_ref, wbb_ref,
              wout_ref, g_ref, b_ref, wr_ref, rb_ref, x1_ref, gate_ref, alpha=alpha)


def _mix_sample_kernel(x_ref, o_ref, pre_ref, x1_all_ref, meta_all_ref, w2_ref, wva_ref, wmix_ref, ps_ref, wbb_ref,
                       wout_ref, g_ref, b_ref, wr_ref, rb_ref, x1_ref, gate_ref, tail_ref, ext_ref,
                       *, alpha, t_new, past):
    del x1_all_ref, meta_all_ref
    tm = x_ref.shape[0]
    sb = tm // t_new
    pw = ps_ref.shape[1]
    gd = pw // len(POOL_WINDOWS)
    x = x_ref[...]
    h2 = jnp.dot(x.astype(BF16), w2_ref[...], preferred_element_type=F32)
    u = h2[:, :pw]
    ext_ref[:, 0:POOL_PAD, :] = pre_ref[...]
    ext_ref[:, POOL_PAD:POOL_PAD + t_new, :] = u.reshape(sb, t_new, pw)
    tail_ref[...] = ext_ref[:, t_new:t_new + POOL_PAD, :]
    pos = past + lax.broadcasted_iota(jnp.int32, (sb, t_new, 1), 1)
    pooled = []
    for g, w in enumerate(POOL_WINDOWS):
        acc = ext_ref[:, POOL_PAD:POOL_PAD + t_new, g * gd:(g + 1) * gd]
        for j in range(1, w):
            acc = acc + ext_ref[:, POOL_PAD - j:POOL_PAD - j + t_new, g * gd:(g + 1) * gd]
        cnt = jnp.minimum(w, pos + 1).astype(F32)
        pooled.append((acc / cnt).reshape(tm, gd) - u[:, g * gd:(g + 1) * gd])
    pooled = jnp.concatenate(pooled, axis=1)
    d = x.shape[1]
    _mix_tail(x, o_ref[...], pooled, h2[:, pw:pw + d], h2[:, pw + d:], wva_ref, wmix_ref, ps_ref, wbb_ref,
              wout_ref, g_ref, b_ref, wr_ref, rb_ref, x1_ref, gate_ref, alpha=alpha)


def _mix_prompt(x, x_off, o, weights, *, n_all, tm, alpha, seq):
    n = o.shape[0]
    d = x.shape[1]
    pw = weights[3].shape[1]
    tiles_per_seq = seq // tm
    n_tiles = n // tm
    last = n_tiles - 1
    return pl.pallas_call(
        functools.partial(_mix_prompt_kernel, alpha=alpha, tiles_per_seq=tiles_per_seq, n_tiles=n_tiles),
        grid=(n_all // tm,),
        in_specs=[pl.BlockSpec((tm, d), lambda i: (jnp.minimum(i, last) + x_off, 0)),
                  pl.BlockSpec((tm, o.shape[1]), lambda i: (jnp.minimum(i, last), 0))]
        + [_const_spec(a.shape) for a in weights],
        out_specs=[pl.BlockSpec((tm, d), lambda i: (i, 0)),
                   pl.BlockSpec((META_ROWS, tm), lambda i: (0, i)),
                   pl.BlockSpec((1, POOL_PAD, pw), lambda i: (jnp.minimum(i, last) // tiles_per_seq, 0, 0))],
        out_shape=[jax.ShapeDtypeStruct((n_all, d), F32), jax.ShapeDtypeStruct((META_ROWS, n_all), F32),
                   jax.ShapeDtypeStruct((n // seq, POOL_PAD, pw), F32)],
        scratch_shapes=[pltpu.VMEM((tm + POOL_PAD, pw), F32)],
        compiler_params=_params("arbitrary"),
        name="mix_prompt",
    )(x, o, *weights)


def _mix_sample(x, x_off, o, prefix, x1_all, meta_all, weights, *, row0, tm, alpha, t_new, past):
    n, d = o.shape
    pw = weights[3].shape[1]
    sb = tm // t_new
    off = row0 // tm
    any_spec = pl.BlockSpec(memory_space=pl.ANY)
    return pl.pallas_call(
        functools.partial(_mix_sample_kernel, alpha=alpha, t_new=t_new, past=past),
        grid=(n // tm,),
        in_specs=[pl.BlockSpec((tm, d), lambda i: (i + x_off, 0)), pl.BlockSpec((tm, d), lambda i: (i, 0)),
                  pl.BlockSpec((sb, POOL_PAD, pw), lambda i: (i, 0, 0)), any_spec, any_spec]
        + [_const_spec(a.shape) for a in weights],
        out_specs=[pl.BlockSpec((tm, d), lambda i: (i + off, 0)),
                   pl.BlockSpec((META_ROWS, tm), lambda i: (0, i + off)),
                   pl.BlockSpec((sb, POOL_PAD, pw), lambda i: (i, 0, 0))],
        out_shape=[jax.ShapeDtypeStruct(x1_all.shape, F32), jax.ShapeDtypeStruct(meta_all.shape, F32),
                   jax.ShapeDtypeStruct((n // t_new, POOL_PAD, pw), F32)],
        input_output_aliases={3: 0, 4: 1},
        scratch_shapes=[pltpu.VMEM((sb, POOL_PAD + t_new, pw), F32)],
        compiler_params=_params("arbitrary"),
        name="mix_sample",
    )(x, o, prefix, x1_all, meta_all, *weights)


GATHER_UNROLL = 8


def _gather_rows_kernel(idx_ref, tbl_ref, out_ref):
    tm = out_ref.shape[0]
    base = pl.program_id(1) * tm

    def body(g, carry):
        for u in range(GATHER_UNROLL):
            r = g * GATHER_UNROLL + u
            out_ref[pl.ds(r, 1), :] = tbl_ref[pl.ds(idx_ref[base + r], 1), :]
        return carry

    lax.fori_loop(0, tm // GATHER_UNROLL, body, 0)


def _gather_rows(table, idx, *, tm, cw):
    r, d = table.shape
    m = idx.shape[0]
    grid_spec = pltpu.PrefetchScalarGridSpec(
        num_scalar_prefetch=1,
        grid=(d // cw, m // tm),
        in_specs=[pl.BlockSpec((r, cw), lambda c, i, idx_ref: (0, c), pipeline_mode=pl.Buffered(1))],
        out_specs=pl.BlockSpec((tm, cw), lambda c, i, idx_ref: (i, c)),
    )
    return pl.pallas_call(
        _gather_rows_kernel,
        grid_spec=grid_spec,
        out_shape=jax.ShapeDtypeStruct((m, d), table.dtype),
        compiler_params=_params("arbitrary", "arbitrary"),
        name="gather_rows",
    )(idx, table)


def _moe_kernel(ea_ref, eb_ref, used_ref, x_ref, wt_ref, wga_ref, wua_ref, wda_ref, wgb_ref, wub_ref, wdb_ref,
                g_ref, b_ref, o_ref, *, alpha):
    del ea_ref, eb_ref

    @pl.when(pl.program_id(0) < used_ref[0])
    def _():
        x = x_ref[...]
        xb = x.astype(BF16)
        wt = wt_ref[...]

        def expert(wg_ref, wu_ref, wd_ref, col):
            hg = jnp.dot(xb, wg_ref[0], preferred_element_type=F32)
            hu = jnp.dot(xb, wu_ref[0], preferred_element_type=F32)
            hid = (jax.nn.silu(hg) * hu * col).astype(BF16)
            return jnp.dot(hid, wd_ref[0], preferred_element_type=F32)

        y = expert(wga_ref, wua_ref, wda_ref, wt[:, 0:1]) + expert(wgb_ref, wub_ref, wdb_ref, wt[:, 1:2])
        o_ref[...] = _layer_norm(alpha * x + y, g_ref[...], b_ref[...])

    @pl.when(pl.program_id(0) >= used_ref[0])
    def _():
        o_ref[...] = jnp.zeros(o_ref.shape, F32)


def _moe(xg, wts, tile_ea, tile_eb, n_used, wg, wu, wd, g2, b2, *, tm, alpha):
    p, d = xg.shape
    f = wg.shape[2]
    tile = lambda i, ea, eb, used: (jnp.minimum(i, used[0] - 1), 0)
    first = lambda i, ea, eb, used: (ea[i], 0, 0)
    second = lambda i, ea, eb, used: (eb[i], 0, 0)
    const = lambda shape: pl.BlockSpec(shape, lambda i, ea, eb, used: (0,) * len(shape),
                                       pipeline_mode=pl.Buffered(1))
    grid_spec = pltpu.PrefetchScalarGridSpec(
        num_scalar_prefetch=3,
        grid=(p // tm,),
        in_specs=[pl.BlockSpec((tm, d), tile), pl.BlockSpec((tm, wts.shape[1]), tile),
                  pl.BlockSpec((1, d, f), first), pl.BlockSpec((1, d, f), first), pl.BlockSpec((1, f, d), first),
                  pl.BlockSpec((1, d, f), second), pl.BlockSpec((1, d, f), second), pl.BlockSpec((1, f, d), second),
                  const(g2.shape), const(b2.shape)],
        out_specs=pl.BlockSpec((tm, d), lambda i, ea, eb, used: (i, 0)),
    )
    return pl.pallas_call(
        functools.partial(_moe_kernel, alpha=alpha),
        grid_spec=grid_spec,
        out_shape=jax.ShapeDtypeStruct((p, d), F32),
        compiler_params=_params("arbitrary"),
        name="moe",
    )(tile_ea, tile_eb, n_used, xg, wts, wg, wu, wd, wg, wu, wd, g2, b2)


def _routing_plan(meta, *, tm):
    n = meta.shape[1]
    n_tiles = (n + N_BUCKETS * (tm - 1) + tm - 1) // tm
    bucket = meta[0].astype(jnp.int32)
    onehot = (bucket[:, None] == jnp.arange(N_BUCKETS, dtype=jnp.int32)[None, :]).astype(jnp.int32)
    csum = jnp.cumsum(onehot, axis=0)
    rank = jnp.take_along_axis(csum, bucket[:, None], axis=1)[:, 0] - 1
    tiles_per_bucket = (csum[-1] + tm - 1) // tm
    tile_end = jnp.cumsum(tiles_per_bucket)
    n_used = tile_end[-1:]
    slot = (tile_end - tiles_per_bucket)[bucket] * tm + rank
    token_of_slot = jnp.zeros((n_tiles * tm,), jnp.int32).at[slot].set(jnp.arange(n, dtype=jnp.int32))
    tile_id = jnp.minimum(jnp.arange(n_tiles, dtype=jnp.int32), n_used - 1)
    tile_bucket = jnp.minimum(jnp.searchsorted(tile_end, tile_id, side="right"), N_BUCKETS - 1).astype(jnp.int32)
    group, pair = tile_bucket // PAIRS_PER_GROUP, tile_bucket % PAIRS_PER_GROUP
    tile_ea = group * EXPERTS_PER_GROUP + jnp.asarray(PAIR_LO, jnp.int32)[pair]
    tile_eb = group * EXPERTS_PER_GROUP + jnp.asarray(PAIR_HI, jnp.int32)[pair]
    wts = jnp.zeros((n_tiles * tm, 128), F32).at[:, :2].set(meta[1:3].T[token_of_slot])
    return slot, token_of_slot, tile_ea, tile_eb, n_used.astype(jnp.int32), wts


def _rope_tables(pos, reps):
    half = QK_ROPE_DIM // 2
    inv = ROPE_BASE ** (-jnp.arange(half, dtype=F32) / half)
    ang = pos.astype(F32)[:, None] * inv
    cos, sin = jnp.cos(ang), jnp.sin(ang)
    return (jnp.tile(jnp.concatenate([cos, cos], axis=1), (1, reps)),
            jnp.tile(jnp.concatenate([-sin, sin], axis=1), (1, reps)))


def kernel(x_prompt, x_sample, cache_kv_latent, cache_k_rope, state_pool, page_table, w_in, g_q_norm, g_kv_norm,
           w_q_up, w_k_up, w_v_up, w_pool_mix, pool_scale, w_branch_a, w_branch_b, w_out, ln1_g, ln1_b,
           w_router, router_bias, w_exp_gate, w_exp_up, w_exp_down, ln2_g, ln2_b):
    bp, tp, d = x_prompt.shape
    bs, ts, _ = x_sample.shape
    depth = w_in.shape[0]
    n_pages = page_table.shape[1]
    past = n_pages * cache_kv_latent.shape[2]
    alpha = (2 * depth) ** 0.25
    pw = pool_scale.shape[1]
    n_p, n_s = bp * tp, bs * ts
    s1 = Q_LORA_RANK + KV_LORA_RANK + QK_ROPE_DIM

    tm_p = min(512, tp)
    tq = min(512, tp)
    tk = min(512, tp)
    tm_mix_p = min(512, tp)
    tm_s = min(256, n_s)
    tm_moe = 256
    tm_gather = min(1024, n_s)
    n_all = n_p + n_s

    cos_p, sin_p = _rope_tables(jnp.arange(tp, dtype=jnp.int32), N_HEADS)
    cos_pt, sin_pt = cos_p.T, sin_p.T
    cos_p, sin_p = cos_p[:, :QK_ROPE_DIM], sin_p[:, :QK_ROPE_DIM]
    cos_s, sin_s = _rope_tables(past + jnp.arange(ts, dtype=jnp.int32), N_HEADS)
    cos_s, sin_s = jnp.tile(cos_s, (tm_s // ts, 1)), jnp.tile(sin_s, (tm_s // ts, 1))
    prefix = jnp.pad(state_pool, ((0, 0), (0, 0), (POOL_PAD - state_pool.shape[2], 0), (0, 0)))
    cache_rt = jnp.swapaxes(cache_k_rope, 2, 3)
    wr_t = w_router.T
    rb = router_bias.reshape(N_EXPERTS, 1)

    x_p, x_s, x_s_off = x_prompt.reshape(n_p, d), x_sample.reshape(n_s, d), 0
    outs = [[] for _ in range(6)]
    for l in range(depth):
        wq = w_q_up[l]
        wq_nope = jnp.transpose(wq[:, :, :QK_NOPE_DIM], (1, 0, 2))
        wqr = (wq[:, :, QK_NOPE_DIM:] * Q_SCALE).reshape(Q_LORA_RANK, N_HEADS * QK_ROPE_DIM).astype(BF16)
        wk = jnp.transpose(w_k_up[l], (1, 0, 2))
        wv = jnp.transpose(w_v_up[l], (1, 0, 2))
        wba = w_branch_a[l].reshape(N_HEADS, -1, d)
        wlat, wlat_t, wva = _fold_weights(wq_nope, wk, wv, wba)
        w1 = w_in[l, :, :s1].astype(BF16)
        w2 = w_in[l, :, s1:].astype(BF16)
        gq = g_q_norm[l].reshape(1, -1)
        gkv = g_kv_norm[l].reshape(1, -1)
        mix_w = (w2, wva, w_pool_mix[l].astype(BF16), pool_scale[l].reshape(1, pw), w_branch_b[l].astype(BF16),
                 w_out[l].astype(BF16), ln1_g[l].reshape(1, d), ln1_b[l].reshape(1, d), wr_t, rb)
        wg, wu, wd = w_exp_gate[l].astype(BF16), w_exp_up[l].astype(BF16), w_exp_down[l].astype(BF16)
        g2, b2 = ln2_g[l].reshape(1, d), ln2_b[l].reshape(1, d)

        qt, kvcat, ckv_t, ckv, krope = _proj_prompt(x_p, n_p, w1, wlat_t, wqr.T, gq, gkv, cos_p, sin_p, cos_pt,
                                                    sin_pt, tm=tm_p)
        o = _attn_prompt(qt, kvcat, ckv_t, batch=bp, seq=tp, tq=tq, tk=tk)
        x1, meta, tail = _mix_prompt(x_p, 0, o, mix_w, n_all=n_all, tm=tm_mix_p, alpha=alpha, seq=tp)
        outs[0].append(ckv.reshape(bp, tp, -1))
        outs[1].append(krope.reshape(bp, tp, -1))
        outs[2].append(tail[:, 1:, :])

        q, kvcat, ckv, krope = _proj_sample(x_s, x_s_off, n_s, w1, wlat, wqr, gq, gkv, cos_s, sin_s, tm=tm_s)
        o = _attn_sample(page_table, q, kvcat, cache_kv_latent, cache_rt, layer=l, t_new=ts)
        x1, meta, tail = _mix_sample(x_s, x_s_off, o, prefix[l], x1, meta, mix_w, row0=n_p, tm=tm_s, alpha=alpha,
                                     t_new=ts, past=past)
        outs[3].append(ckv.reshape(bs, ts, -1))
        outs[4].append(krope.reshape(bs, ts, -1))
        outs[5].append(tail[:, 1:, :])

        slot, token_of_slot, tile_ea, tile_eb, n_used, wts = _routing_plan(meta, tm=tm_moe)
        xg = _gather_rows(x1, token_of_slot, tm=tm_gather, cw=d // 2)
        yg = _moe(xg, wts, tile_ea, tile_eb, n_used, wg, wu, wd, g2, b2, tm=tm_moe, alpha=alpha)
        x_all = _gather_rows(yg, slot, tm=tm_gather, cw=d // 4)
        x_p, x_s, x_s_off = x_all, x_all, n_p // tm_s

    return (x_all[:n_p].reshape(bp, tp, d), x_all[n_p:].reshape(bs, ts, d), jnp.stack(outs[0]),
            jnp.stack(outs[1]), jnp.stack(outs[2]), jnp.stack(outs[3]), jnp.stack(outs[4]), jnp.stack(outs[5]))
```

```python
import functools

import jax
import jax.numpy as jnp
from jax import lax
from jax.experimental import pallas as pl
from jax.experimental.pallas import tpu as pltpu
from jax.experimental.pallas import tpu_sc as plsc

F32 = jnp.float32
BF16 = jnp.bfloat16

N_HEADS = 8
QK_NOPE_DIM = 64
QK_ROPE_DIM = 32
Q_LORA_RANK = 256
KV_LORA_RANK = 128
QK_CAT = KV_LORA_RANK + QK_ROPE_DIM
ROPE_BASE = 10000.0
SM_SCALE = (QK_NOPE_DIM + QK_ROPE_DIM) ** -0.5
LOG2_E = 1.4426950408889634
Q_SCALE = SM_SCALE * LOG2_E
POOL_WINDOWS = (2, 4, 8, 16)
POOL_PAD = 16
N_EXPERTS = 16
N_EXPERT_GROUPS = 4
EXPERTS_PER_GROUP = 4
PAIRS_PER_GROUP = 6
PAIR_LO = (0, 0, 0, 1, 1, 2)
PAIR_HI = (1, 2, 3, 2, 3, 3)
N_BUCKETS = N_EXPERT_GROUPS * PAIRS_PER_GROUP
META_ROWS = 8
LN_EPS = 1e-5
RMS_EPS = 1e-6

VMEM_LIMIT_BYTES = 56 * 1024 * 1024


def _params(*semantics):
    return pltpu.CompilerParams(dimension_semantics=semantics, vmem_limit_bytes=VMEM_LIMIT_BYTES)


def _const_spec(shape):
    zeros = (0,) * len(shape)
    return pl.BlockSpec(shape, lambda *_: zeros, pipeline_mode=pl.Buffered(1))


def _split_bf16(a):
    hi = a.astype(BF16)
    lo = (a - hi.astype(F32)).astype(BF16)
    return hi, lo


def _dot3(a, b, dims):
    a_hi, a_lo = _split_bf16(a)
    b_hi, b_lo = _split_bf16(b)
    d = functools.partial(lax.dot_general, dimension_numbers=dims, preferred_element_type=F32)
    return d(a_hi, b_hi) + (d(a_hi, b_lo) + d(a_lo, b_hi))


_NN = (((1,), (0,)), ((), ()))
_NT = (((1,), (1,)), ((), ()))


def _layer_norm(y, g, b):
    mu = jnp.mean(y, axis=-1, keepdims=True)
    d = y - mu
    var = jnp.mean(d * d, axis=-1, keepdims=True)
    return d * lax.rsqrt(var + LN_EPS) * g + b


def _rms_norm(y, g):
    return y * lax.rsqrt(jnp.mean(y * y, axis=-1, keepdims=True) + RMS_EPS) * g


def _fold_kernel(wqn_ref, wqn_t_ref, wk_ref, wk_t_ref, wv_ref, wba_ref, wlat_ref, wlat_t_ref, wva_ref):
    wlat_ref[...] = (_dot3(wqn_ref[0], wk_t_ref[0], _NN) * Q_SCALE).astype(BF16)
    wlat_t_ref[...] = (_dot3(wk_ref[0], wqn_t_ref[0], _NN) * Q_SCALE).astype(BF16)
    wva_ref[...] = _dot3(wv_ref[0], wba_ref[0], _NN).astype(BF16)


def _fold_weights(wq_nope, wk, wv, wba):
    h, r, dn = wq_nope.shape
    d = wba.shape[-1]
    head = lambda *blk: pl.BlockSpec((1,) + blk, lambda i: (i, 0, 0))
    return pl.pallas_call(
        _fold_kernel,
        grid=(h,),
        in_specs=[head(r, dn), head(dn, r), head(KV_LORA_RANK, dn), head(dn, KV_LORA_RANK),
                  head(KV_LORA_RANK, wv.shape[-1]), head(wba.shape[1], d)],
        out_specs=[pl.BlockSpec((r, KV_LORA_RANK), lambda i: (0, i)),
                   pl.BlockSpec((KV_LORA_RANK, r), lambda i: (i, 0)),
                   pl.BlockSpec((KV_LORA_RANK, d), lambda i: (i, 0))],
        out_shape=[jax.ShapeDtypeStruct((r, h * KV_LORA_RANK), BF16),
                   jax.ShapeDtypeStruct((h * KV_LORA_RANK, r), BF16),
                   jax.ShapeDtypeStruct((h * KV_LORA_RANK, d), BF16)],
        compiler_params=_params("arbitrary"),
        name="fold",
    )(wq_nope, jnp.swapaxes(wq_nope, 1, 2), wk, jnp.swapaxes(wk, 1, 2), wv, wba)


def _proj_common(x_ref, w1_ref, gq_ref, gkv_ref, cos_ref, sin_ref, ckv_ref, krope_ref):
    x = x_ref[...].astype(BF16)
    h = jnp.dot(x, w1_ref[...], preferred_element_type=F32)
    qa = h[:, :Q_LORA_RANK]
    kva = h[:, Q_LORA_RANK:Q_LORA_RANK + KV_LORA_RANK]
    kr = h[:, Q_LORA_RANK + KV_LORA_RANK:]
    qn = _rms_norm(qa, gq_ref[...])
    ckv = _rms_norm(kva, gkv_ref[...])
    half = QK_ROPE_DIM // 2
    kr_swapped = jnp.concatenate([kr[:, half:], kr[:, :half]], axis=1)
    krope = kr * cos_ref[:, :QK_ROPE_DIM] + kr_swapped * sin_ref[:, :QK_ROPE_DIM]
    ckv_ref[...] = ckv
    krope_ref[...] = krope
    return qn, ckv, krope


def _proj_sample_kernel(x_ref, w1_ref, wlat_ref, wqr_ref, gq_ref, gkv_ref, cos_ref, sin_ref,
                        q_ref, kvcat_ref, ckv_ref, krope_ref):
    qn, ckv, krope = _proj_common(x_ref, w1_ref, gq_ref, gkv_ref, cos_ref, sin_ref, ckv_ref, krope_ref)
    qn = qn.astype(BF16)
    ql = jnp.dot(qn, wlat_ref[...], preferred_element_type=F32)
    qr = jnp.dot(qn, wqr_ref[...], preferred_element_type=F32)
    half = QK_ROPE_DIM // 2
    width = qr.shape[1]
    lane = lax.broadcasted_iota(jnp.int32, qr.shape, 1)
    swapped = jnp.where((lane % QK_ROPE_DIM) < half,
                        pltpu.roll(qr, width - half, 1), pltpu.roll(qr, half, 1))
    qrr = qr * cos_ref[...] + swapped * sin_ref[...]
    for hh in range(N_HEADS):
        q_ref[hh, :, :KV_LORA_RANK] = ql[:, hh * KV_LORA_RANK:(hh + 1) * KV_LORA_RANK]
        q_ref[hh, :, KV_LORA_RANK:] = qrr[:, hh * QK_ROPE_DIM:(hh + 1) * QK_ROPE_DIM]
    kvcat_ref[:, :KV_LORA_RANK] = ckv
    kvcat_ref[:, KV_LORA_RANK:] = krope


def _proj_prompt_kernel(x_ref, w1_ref, wlat_t_ref, wqr_t_ref, gq_ref, gkv_ref, cos_ref, sin_ref,
                        cos_t_ref, sin_t_ref, qt_ref, kvcat_ref, ckv_t_ref, ckv_ref, krope_ref):
    qn, ckv, krope = _proj_common(x_ref, w1_ref, gq_ref, gkv_ref, cos_ref, sin_ref, ckv_ref, krope_ref)
    qn_t = qn.T.astype(BF16)
    ql_t = jnp.dot(wlat_t_ref[...], qn_t, preferred_element_type=F32)
    qr_t = jnp.dot(wqr_t_ref[...], qn_t, preferred_element_type=F32)
    half = QK_ROPE_DIM // 2
    pieces = []
    for hh in range(N_HEADS):
        base = hh * QK_ROPE_DIM
        pieces += [qr_t[base + half:base + QK_ROPE_DIM], qr_t[base:base + half]]
    qrr_t = qr_t * cos_t_ref[...] + jnp.concatenate(pieces, axis=0) * sin_t_ref[...]
    for hh in range(N_HEADS):
        qt_ref[hh, :KV_LORA_RANK, :] = ql_t[hh * KV_LORA_RANK:(hh + 1) * KV_LORA_RANK].astype(BF16)
        qt_ref[hh, KV_LORA_RANK:, :] = qrr_t[hh * QK_ROPE_DIM:(hh + 1) * QK_ROPE_DIM].astype(BF16)
    kvcat_ref[:, :KV_LORA_RANK] = ckv.astype(BF16)
    kvcat_ref[:, KV_LORA_RANK:] = krope.astype(BF16)
    ckv_t_ref[...] = ckv.T.astype(BF16)


def _proj_sample(x, x_off, n, w1, wlat, wqr, gq, gkv, cos_t, sin_t, *, tm):
    d = x.shape[1]
    row = lambda width: pl.BlockSpec((tm, width), lambda i: (i, 0))
    return pl.pallas_call(
        _proj_sample_kernel,
        grid=(n // tm,),
        in_specs=[pl.BlockSpec((tm, d), lambda i: (i + x_off, 0)), _const_spec(w1.shape), _const_spec(wlat.shape), _const_spec(wqr.shape),
                  _const_spec(gq.shape), _const_spec(gkv.shape),
                  _const_spec(cos_t.shape), _const_spec(sin_t.shape)],
        out_specs=[pl.BlockSpec((N_HEADS, tm, QK_CAT), lambda i: (0, i, 0)),
                   row(QK_CAT), row(KV_LORA_RANK), row(QK_ROPE_DIM)],
        out_shape=[jax.ShapeDtypeStruct((N_HEADS, n, QK_CAT), F32),
                   jax.ShapeDtypeStruct((n, QK_CAT), F32),
                   jax.ShapeDtypeStruct((n, KV_LORA_RANK), F32),
                   jax.ShapeDtypeStruct((n, QK_ROPE_DIM), F32)],
        compiler_params=_params("arbitrary"),
        name="proj_sample",
    )(x, w1, wlat, wqr, gq, gkv, cos_t, sin_t)


def _proj_prompt(x, n, w1, wlat_t, wqr_t, gq, gkv, cos, sin, cos_t, sin_t, *, tm):
    d = x.shape[1]
    tiles = cos.shape[0] // tm
    row = lambda width: pl.BlockSpec((tm, width), lambda i: (i, 0))
    return pl.pallas_call(
        _proj_prompt_kernel,
        grid=(n // tm,),
        in_specs=[row(d), _const_spec(w1.shape), _const_spec(wlat_t.shape), _const_spec(wqr_t.shape),
                  _const_spec(gq.shape), _const_spec(gkv.shape),
                  pl.BlockSpec((tm, cos.shape[1]), lambda i: (i % tiles, 0)),
                  pl.BlockSpec((tm, sin.shape[1]), lambda i: (i % tiles, 0)),
                  pl.BlockSpec((cos_t.shape[0], tm), lambda i: (0, i % tiles)),
                  pl.BlockSpec((sin_t.shape[0], tm), lambda i: (0, i % tiles))],
        out_specs=[pl.BlockSpec((N_HEADS, QK_CAT, tm), lambda i: (0, 0, i)),
                   row(QK_CAT),
                   pl.BlockSpec((KV_LORA_RANK, tm), lambda i: (0, i)),
                   row(KV_LORA_RANK), row(QK_ROPE_DIM)],
        out_shape=[jax.ShapeDtypeStruct((N_HEADS, QK_CAT, n), BF16),
                   jax.ShapeDtypeStruct((n, QK_CAT), BF16),
                   jax.ShapeDtypeStruct((KV_LORA_RANK, n), BF16),
                   jax.ShapeDtypeStruct((n, KV_LORA_RANK), F32),
                   jax.ShapeDtypeStruct((n, QK_ROPE_DIM), F32)],
        compiler_params=_params("arbitrary"),
        name="proj_prompt",
    )(x, w1, wlat_t, wqr_t, gq, gkv, cos, sin, cos_t, sin_t)


SCORE_AHEAD = 2
SCORE_SLOTS = 4


def _col_reduce(x, op, reduce_rows):
    rows = x.shape[0]
    while rows >= 32:
        x = x.reshape(4, rows // 4, x.shape[1])
        x = op(op(x[0], x[1]), op(x[2], x[3]))
        rows //= 4
    return reduce_rows(x, axis=0, keepdims=True)


def _attn_prompt_kernel(qt_ref, k_ref, vt_ref, o_ref, m_ref, l_ref, acc_ref, s_ref, *, tq, tk):
    i = pl.program_id(1)
    n_full = (i * tq) // tk
    m_ref[...] = jnp.full(m_ref.shape, -jnp.inf, F32)
    l_ref[...] = jnp.zeros(l_ref.shape, F32)
    acc_ref[...] = jnp.zeros(acc_ref.shape, F32)

    def scores(j, hh):
        k = k_ref[pl.ds(pl.multiple_of(j * tk, tk), tk), :]
        s_ref[hh % SCORE_SLOTS] = jnp.dot(k, qt_ref[hh], preferred_element_type=F32)

    def step(j, masked):
        vt = vt_ref[:, pl.ds(pl.multiple_of(j * tk, tk), tk)]
        if masked:
            key = j * tk + lax.broadcasted_iota(jnp.int32, (tk, tq), 0)
            tok = i * tq + lax.broadcasted_iota(jnp.int32, (tk, tq), 1)
            visible = key <= tok
        for hh in range(N_HEADS):
            ahead = hh + SCORE_AHEAD
            if ahead < N_HEADS:
                scores(j, ahead)
            elif not masked:
                scores(j + 1, ahead - N_HEADS)
            s = s_ref[hh % SCORE_SLOTS]
            if masked:
                s = jnp.where(visible, s, -jnp.inf)
            m_prev = m_ref[hh]
            m_new = jnp.maximum(m_prev, _col_reduce(s, jnp.maximum, jnp.max))
            alpha = jnp.exp2(m_prev - m_new)
            p = jnp.exp2(s - m_new)
            l_ref[hh] = alpha * l_ref[hh] + _col_reduce(p, jnp.add, jnp.sum)
            pv = jnp.dot(vt, p.astype(BF16), preferred_element_type=F32)
            acc_ref[hh] = alpha * acc_ref[hh] + pv
            m_ref[hh] = m_new

    def body(j, carry):
        step(j, False)
        return carry

    for hh in range(SCORE_AHEAD):
        scores(0, hh)
    lax.fori_loop(0, n_full, body, 0)
    step(n_full, True)
    for hh in range(N_HEADS):
        o_t = acc_ref[hh] / l_ref[hh]
        o_ref[:, hh * KV_LORA_RANK:(hh + 1) * KV_LORA_RANK] = o_t.T.astype(o_ref.dtype)


def _attn_prompt(qt, kvcat, ckv_t, *, batch, seq, tq, tk):
    nq = seq // tq
    n = batch * seq
    return pl.pallas_call(
        functools.partial(_attn_prompt_kernel, tq=tq, tk=tk),
        grid=(batch, nq),
        in_specs=[pl.BlockSpec((N_HEADS, QK_CAT, tq), lambda b, i: (0, 0, b * nq + i)),
                  pl.BlockSpec((seq, QK_CAT), lambda b, i: (b, 0)),
                  pl.BlockSpec((KV_LORA_RANK, seq), lambda b, i: (0, b))],
        out_specs=pl.BlockSpec((tq, N_HEADS * KV_LORA_RANK), lambda b, i: (b * nq + i, 0)),
        out_shape=jax.ShapeDtypeStruct((n, N_HEADS * KV_LORA_RANK), BF16),
        scratch_shapes=[pltpu.VMEM((N_HEADS, 1, tq), F32), pltpu.VMEM((N_HEADS, 1, tq), F32),
                        pltpu.VMEM((N_HEADS, KV_LORA_RANK, tq), F32), pltpu.VMEM((SCORE_SLOTS, tk, tq), F32)],
        compiler_params=_params("arbitrary", "arbitrary"),
        name="attn_prompt",
    )(qt, kvcat, ckv_t)


def _attn_sample_kernel(pt_ref, q_ref, knew_ref, cache_c_ref, cache_rt_ref, o_ref, cbuf, rbuf, sem,
                        *, layer, n_pages, page, t_new):
    b = pl.program_id(0)
    nb = pl.num_programs(0)

    def page_copies(seq_idx, slot):
        copies = []
        for p in range(n_pages):
            pid = pt_ref[seq_idx, p]
            copies.append(pltpu.make_async_copy(
                cache_c_ref.at[layer, pid], cbuf.at[slot, pl.ds(p * page, page), :], sem.at[slot, 0]))
            copies.append(pltpu.make_async_copy(
                cache_rt_ref.at[layer, pid], rbuf.at[slot, :, pl.ds(p * page, page)], sem.at[slot, 1]))
        return copies

    slot = b % 2

    @pl.when(b == 0)
    def _():
        for c in page_copies(0, 0):
            c.start()

    @pl.when(b + 1 < nb)
    def _():
        for c in page_copies(b + 1, 1 - slot):
            c.start()

    for c in page_copies(b, slot):
        c.wait()

    rows = N_HEADS * t_new
    q = q_ref[...].reshape(rows, QK_CAT).astype(BF16)
    kc = cbuf[slot].astype(BF16)
    krt = rbuf[slot].astype(BF16)
    kn = knew_ref[...].astype(BF16)
    s = (lax.dot_general(q[:, :KV_LORA_RANK], kc, _NT, preferred_element_type=F32)
         + jnp.dot(q[:, KV_LORA_RANK:], krt, preferred_element_type=F32))
    sn = lax.dot_general(q, kn, _NT, preferred_element_type=F32)
    tok = lax.broadcasted_iota(jnp.int32, (N_HEADS, t_new, t_new), 1).reshape(rows, t_new)
    key = lax.broadcasted_iota(jnp.int32, (rows, t_new), 1)
    sn = jnp.where(key <= tok, sn, -jnp.inf)
    m = jnp.maximum(jnp.max(s, axis=-1, keepdims=True), jnp.max(sn, axis=-1, keepdims=True))
    p = jnp.exp2(s - m)
    pn = jnp.exp2(sn - m)
    l = jnp.sum(p, axis=-1, keepdims=True) + jnp.sum(pn, axis=-1, keepdims=True)
    o = (jnp.dot(p.astype(BF16), kc, preferred_element_type=F32)
         + jnp.dot(pn.astype(BF16), kn[:, :KV_LORA_RANK], preferred_element_type=F32)) / l
    for hh in range(N_HEADS):
        o_ref[:, hh * KV_LORA_RANK:(hh + 1) * KV_LORA_RANK] = o[hh * t_new:(hh + 1) * t_new]


def _attn_sample(page_table, q, knew, cache_c, cache_rt, *, layer, t_new):
    n_seq, n_pages = page_table.shape
    page = cache_c.shape[2]
    past = n_pages * page
    grid_spec = pltpu.PrefetchScalarGridSpec(
        num_scalar_prefetch=1,
        grid=(n_seq,),
        in_specs=[pl.BlockSpec((N_HEADS, t_new, QK_CAT), lambda b, pt: (0, b, 0)),
                  pl.BlockSpec((t_new, QK_CAT), lambda b, pt: (b, 0)),
                  pl.BlockSpec(memory_space=pl.ANY),
                  pl.BlockSpec(memory_space=pl.ANY)],
        out_specs=pl.BlockSpec((t_new, N_HEADS * KV_LORA_RANK), lambda b, pt: (b, 0)),
        scratch_shapes=[pltpu.VMEM((2, past, KV_LORA_RANK), F32), pltpu.VMEM((2, QK_ROPE_DIM, past), F32),
                        pltpu.SemaphoreType.DMA((2, 2))],
    )
    return pl.pallas_call(
        functools.partial(_attn_sample_kernel, layer=layer, n_pages=n_pages, page=page, t_new=t_new),
        grid_spec=grid_spec,
        out_shape=jax.ShapeDtypeStruct((n_seq * t_new, N_HEADS * KV_LORA_RANK), F32),
        compiler_params=_params("arbitrary"),
        name="attn_sample",
    )(page_table, q, knew, cache_c, cache_rt)


def _router_meta(x1, wr, rb):
    logits = _dot3(wr, x1, _NT)
    score = jax.nn.sigmoid(logits)
    biased = score + rb
    row = lambda a, e: a[e:e + 1, :]
    group_score = []
    for g in range(N_EXPERT_GROUPS):
        v = [row(biased, g * EXPERTS_PER_GROUP + k) for k in range(EXPERTS_PER_GROUP)]
        best = None
        for a in range(EXPERTS_PER_GROUP):
            for c in range(a + 1, EXPERTS_PER_GROUP):
                pair = v[a] + v[c]
                best = pair if best is None else jnp.maximum(best, pair)
        group_score.append(best)
    top = group_score[0]
    grp = jnp.zeros(top.shape, jnp.int32)
    for g in range(1, N_EXPERT_GROUPS):
        better = group_score[g] > top
        grp = jnp.where(better, g, grp)
        top = jnp.maximum(top, group_score[g])

    def pick(a, k):
        out = row(a, (N_EXPERT_GROUPS - 1) * EXPERTS_PER_GROUP + k)
        for g in range(N_EXPERT_GROUPS - 2, -1, -1):
            out = jnp.where(grp == g, row(a, g * EXPERTS_PER_GROUP + k), out)
        return out

    cand = [pick(biased, k) for k in range(EXPERTS_PER_GROUP)]
    aff = [pick(score, k) for k in range(EXPERTS_PER_GROUP)]
    first = jnp.zeros(top.shape, jnp.int32)
    best = cand[0]
    for k in range(1, EXPERTS_PER_GROUP):
        better = cand[k] > best
        first = jnp.where(better, k, first)
        best = jnp.maximum(best, cand[k])
    second = jnp.full(top.shape, -1, jnp.int32)
    best2 = jnp.full(top.shape, -jnp.inf, F32)
    for k in range(EXPERTS_PER_GROUP):
        better = (first != k) & ((second < 0) | (cand[k] > best2))
        second = jnp.where(better, k, second)
        best2 = jnp.where(better, cand[k], best2)
    lo = jnp.minimum(first, second)
    hi = jnp.maximum(first, second)

    def take(vals, k_idx):
        out = vals[EXPERTS_PER_GROUP - 1]
        for k in range(EXPERTS_PER_GROUP - 2, -1, -1):
            out = jnp.where(k_idx == k, vals[k], out)
        return out

    aff_lo, aff_hi = take(aff, lo), take(aff, hi)
    denom = aff_lo + aff_hi
    pair = jnp.where(lo == 0, 0, jnp.where(lo == 1, 3, 5)) + (hi - lo - 1)
    bucket = (grp * PAIRS_PER_GROUP + pair).astype(F32)
    pad = jnp.zeros((META_ROWS - 3,) + top.shape[1:], F32)
    return jnp.concatenate([bucket, aff_lo / denom, aff_hi / denom, pad], axis=0)


def _mix_tail(x, o, pooled, gate_a, gate_b, wva_ref, wmix_ref, ps_ref, wbb_ref, wout_ref, g_ref, b_ref,
              wr_ref, rb_ref, x1_ref, gate_ref, *, alpha):
    gd = wmix_ref.shape[1]
    yb = jnp.concatenate(
        [jnp.dot(pooled[:, g * gd:(g + 1) * gd].astype(BF16), wmix_ref[g], preferred_element_type=F32)
         for g in range(len(POOL_WINDOWS))], axis=1) * ps_ref[...]
    ya = jnp.dot(o.astype(BF16), wva_ref[...], preferred_element_type=F32)
    yb = jnp.dot(yb.astype(BF16), wbb_ref[...], preferred_element_type=F32)
    merged = jax.nn.sigmoid(gate_a) * ya + jax.nn.sigmoid(gate_b) * yb
    mix = jnp.dot(merged.astype(BF16), wout_ref[...], preferred_element_type=F32)
    x1 = _layer_norm(alpha * x + mix, g_ref[...], b_ref[...])
    x1_ref[...] = x1
    gate_ref[...] = _router_meta(x1, wr_ref[...], rb_ref[...])


def _mix_prompt_kernel(*refs, alpha, tiles_per_seq, n_tiles):
    x1_ref, gate_ref = refs[-4], refs[-3]

    @pl.when(pl.program_id(0) < n_tiles)
    def _():
        _mix_prompt_tile(*refs, alpha=alpha, tiles_per_seq=tiles_per_seq)

    @pl.when(pl.program_id(0) >= n_tiles)
    def _():
        x1_ref[...] = jnp.zeros(x1_ref.shape, F32)
        gate_ref[...] = jnp.zeros(gate_ref.shape, F32)


def _mix_prompt_tile(x_ref, o_ref, w2_ref, wva_ref, wmix_ref, ps_ref, wbb_ref, wout_ref, g_ref, b_ref,
                     wr_ref, rb_ref, x1_ref, gate_ref, tail_ref, ext_ref, *, alpha, tiles_per_seq):
    tm = x_ref.shape[0]
    pw = ps_ref.shape[1]
    gd = pw // len(POOL_WINDOWS)
    x = x_ref[...]
    h2 = jnp.dot(x.astype(BF16), w2_ref[...], preferred_element_type=F32)
    u = h2[:, :pw]
    t = pl.program_id(0) % tiles_per_seq

    @pl.when(t == 0)
    def _():
        ext_ref[0:POOL_PAD, :] = jnp.zeros((POOL_PAD, pw), F32)

    @pl.when(t != 0)
    def _():
        ext_ref[0:POOL_PAD, :] = ext_ref[tm:tm + POOL_PAD, :]

    ext_ref[POOL_PAD:POOL_PAD + tm, :] = u
    tail_ref[0] = u[tm - POOL_PAD:, :]
    pos = t * tm + lax.broadcasted_iota(jnp.int32, (tm, 1), 0)
    pooled = []
    for g, w in enumerate(POOL_WINDOWS):
        acc = ext_ref[POOL_PAD:POOL_PAD + tm, g * gd:(g + 1) * gd]
        for j in range(1, w):
            acc = acc + ext_ref[POOL_PAD - j:POOL_PAD - j + tm, g * gd:(g + 1) * gd]
        cnt = jnp.minimum(w, pos + 1).astype(F32)
        pooled.append(acc / cnt - u[:, g * gd:(g + 1) * gd])
    pooled = jnp.concatenate(pooled, axis=1)
    d = x.shape[1]
    _mix_tail(x, o_ref[...], pooled, h2[:, pw:pw + d], h2[:, pw + d:], wva_ref, wmix_ref, ps_ref, wbb_ref,
              wout_ref, g_ref, b_ref, wr_ref, rb_ref, x1_ref, gate_ref, alpha=alpha)


def _mix_sample_kernel(x_ref, o_ref, pre_ref, x1_all_ref, meta_all_ref, w2_ref, wva_ref, wmix_ref, ps_ref, wbb_ref,
                       wout_ref, g_ref, b_ref, wr_ref, rb_ref, x1_ref, gate_ref, tail_ref, ext_ref,
                       *, alpha, t_new, past):
    del x1_all_ref, meta_all_ref
    tm = x_ref.shape[0]
    sb = tm // t_new
    pw = ps_ref.shape[1]
    gd = pw // len(POOL_WINDOWS)
    x = x_ref[...]
    h2 = jnp.dot(x.astype(BF16), w2_ref[...], preferred_element_type=F32)
    u = h2[:, :pw]
    ext_ref[:, 0:POOL_PAD, :] = pre_ref[...]
    ext_ref[:, POOL_PAD:POOL_PAD + t_new, :] = u.reshape(sb, t_new, pw)
    tail_ref[...] = ext_ref[:, t_new:t_new + POOL_PAD, :]
    pos = past + lax.broadcasted_iota(jnp.int32, (sb, t_new, 1), 1)
    pooled = []
    for g, w in enumerate(POOL_WINDOWS):
        acc = ext_ref[:, POOL_PAD:POOL_PAD + t_new, g * gd:(g + 1) * gd]
        for j in range(1, w):
            acc = acc + ext_ref[:, POOL_PAD - j:POOL_PAD - j + t_new, g * gd:(g + 1) * gd]
        cnt = jnp.minimum(w, pos + 1).astype(F32)
        pooled.append((acc / cnt).reshape(tm, gd) - u[:, g * gd:(g + 1) * gd])
    pooled = jnp.concatenate(pooled, axis=1)
    d = x.shape[1]
    _mix_tail(x, o_ref[...], pooled, h2[:, pw:pw + d], h2[:, pw + d:], wva_ref, wmix_ref, ps_ref, wbb_ref,
              wout_ref, g_ref, b_ref, wr_ref, rb_ref, x1_ref, gate_ref, alpha=alpha)


def _mix_prompt(x, x_off, o, weights, *, n_all, tm, alpha, seq):
    n = o.shape[0]
    d = x.shape[1]
    pw = weights[3].shape[1]
    tiles_per_seq = seq // tm
    n_tiles = n // tm
    last = n_tiles - 1
    return pl.pallas_call(
        functools.partial(_mix_prompt_kernel, alpha=alpha, tiles_per_seq=tiles_per_seq, n_tiles=n_tiles),
        grid=(n_all // tm,),
        in_specs=[pl.BlockSpec((tm, d), lambda i: (jnp.minimum(i, last) + x_off, 0)),
                  pl.BlockSpec((tm, o.shape[1]), lambda i: (jnp.minimum(i, last), 0))]
        + [_const_spec(a.shape) for a in weights],
        out_specs=[pl.BlockSpec((tm, d), lambda i: (i, 0)),
                   pl.BlockSpec((META_ROWS, tm), lambda i: (0, i)),
                   pl.BlockSpec((1, POOL_PAD, pw), lambda i: (jnp.minimum(i, last) // tiles_per_seq, 0, 0))],
        out_shape=[jax.ShapeDtypeStruct((n_all, d), F32), jax.ShapeDtypeStruct((META_ROWS, n_all), F32),
                   jax.ShapeDtypeStruct((n // seq, POOL_PAD, pw), F32)],
        scratch_shapes=[pltpu.VMEM((tm + POOL_PAD, pw), F32)],
        compiler_params=_params("arbitrary"),
        name="mix_prompt",
    )(x, o, *weights)


def _mix_sample(x, x_off, o, prefix, x1_all, meta_all, weights, *, row0, tm, alpha, t_new, past):
    n, d = o.shape
    pw = weights[3].shape[1]
    sb = tm // t_new
    off = row0 // tm
    any_spec = pl.BlockSpec(memory_space=pl.ANY)
    return pl.pallas_call(
        functools.partial(_mix_sample_kernel, alpha=alpha, t_new=t_new, past=past),
        grid=(n // tm,),
        in_specs=[pl.BlockSpec((tm, d), lambda i: (i + x_off, 0)), pl.BlockSpec((tm, d), lambda i: (i, 0)),
                  pl.BlockSpec((sb, POOL_PAD, pw), lambda i: (i, 0, 0)), any_spec, any_spec]
        + [_const_spec(a.shape) for a in weights],
        out_specs=[pl.BlockSpec((tm, d), lambda i: (i + off, 0)),
                   pl.BlockSpec((META_ROWS, tm), lambda i: (0, i + off)),
                   pl.BlockSpec((sb, POOL_PAD, pw), lambda i: (i, 0, 0))],
        out_shape=[jax.ShapeDtypeStruct(x1_all.shape, F32), jax.ShapeDtypeStruct(meta_all.shape, F32),
                   jax.ShapeDtypeStruct((n // t_new, POOL_PAD, pw), F32)],
        input_output_aliases={3: 0, 4: 1},
        scratch_shapes=[pltpu.VMEM((sb, POOL_PAD + t_new, pw), F32)],
        compiler_params=_params("arbitrary"),
        name="mix_sample",
    )(x, o, prefix, x1_all, meta_all, *weights)


GATHER_UNROLL = 8


def _gather_rows_kernel(idx_ref, tbl_ref, out_ref):
    tm = out_ref.shape[0]
    base = pl.program_id(1) * tm

    def body(g, carry):
        for u in range(GATHER_UNROLL):
            r = g * GATHER_UNROLL + u
            out_ref[pl.ds(r, 1), :] = tbl_ref[pl.ds(idx_ref[base + r], 1), :]
        return carry

    lax.fori_loop(0, tm // GATHER_UNROLL, body, 0)


def _gather_rows(table, idx, *, tm, cw):
    r, d = table.shape
    m = idx.shape[0]
    grid_spec = pltpu.PrefetchScalarGridSpec(
        num_scalar_prefetch=1,
        grid=(d // cw, m // tm),
        in_specs=[pl.BlockSpec((r, cw), lambda c, i, idx_ref: (0, c), pipeline_mode=pl.Buffered(1))],
        out_specs=pl.BlockSpec((tm, cw), lambda c, i, idx_ref: (i, c)),
    )
    return pl.pallas_call(
        _gather_rows_kernel,
        grid_spec=grid_spec,
        out_shape=jax.ShapeDtypeStruct((m, d), table.dtype),
        compiler_params=_params("arbitrary", "arbitrary"),
        name="gather_rows",
    )(idx, table)


SC_WORKERS = 32
SC_CHUNK = 32


def _sc_gather_rows(table, idx):
    m = idx.shape[0]
    d = table.shape[1]
    per_worker = m // SC_WORKERS
    n_chunks = per_worker // SC_CHUNK
    assert per_worker * SC_WORKERS == m and n_chunks * SC_CHUNK == per_worker, (m, SC_WORKERS, SC_CHUNK)
    mesh = plsc.VectorSubcoreMesh(core_axis_name="c", subcore_axis_name="s")

    @functools.partial(
        pl.kernel, mesh=mesh, out_type=jax.ShapeDtypeStruct((m, d), table.dtype),
        scratch_types=[pltpu.VMEM((SC_CHUNK,), jnp.int32), pltpu.VMEM((SC_CHUNK, d), table.dtype),
                       pltpu.SemaphoreType.DMA],
        name="sc_gather_rows")
    def gather(table_hbm, idx_hbm, out_hbm, idx_v, rows_v, sem):
        worker = lax.axis_index("s") * mesh.num_cores + lax.axis_index("c")
        base = worker * per_worker

        @pl.loop(0, n_chunks)
        def _(c):
            off = pl.multiple_of(base + c * SC_CHUNK, SC_CHUNK)
            pltpu.sync_copy(idx_hbm.at[pl.ds(off, SC_CHUNK)], idx_v)
            pltpu.async_copy(table_hbm.at[idx_v], rows_v, sem).wait()
            pltpu.sync_copy(rows_v, out_hbm.at[pl.ds(off, SC_CHUNK)])

    return gather(table, idx)


def _moe_kernel(ea_ref, eb_ref, used_ref, x_ref, wt_ref, wga_ref, wua_ref, wda_ref, wgb_ref, wub_ref, wdb_ref,
                g_ref, b_ref, o_ref, *, alpha):
    del ea_ref, eb_ref

    @pl.when(pl.program_id(0) < used_ref[0])
    def _():
        x = x_ref[...]
        xb = x.astype(BF16)
        wt = wt_ref[...]

        def expert(wg_ref, wu_ref, wd_ref, col):
            hg = jnp.dot(xb, wg_ref[0], preferred_element_type=F32)
            hu = jnp.dot(xb, wu_ref[0], preferred_element_type=F32)
            hid = (jax.nn.silu(hg) * hu * col).astype(BF16)
            return jnp.dot(hid, wd_ref[0], preferred_element_type=F32)

        y = expert(wga_ref, wua_ref, wda_ref, wt[:, 0:1]) + expert(wgb_ref, wub_ref, wdb_ref, wt[:, 1:2])
        o_ref[...] = _layer_norm(alpha * x + y, g_ref[...], b_ref[...])

    @pl.when(pl.program_id(0) >= used_ref[0])
    def _():
        o_ref[...] = jnp.zeros(o_ref.shape, F32)


def _moe(xg, wts, tile_ea, tile_eb, n_used, wg, wu, wd, g2, b2, *, tm, alpha):
    p, d = xg.shape
    f = wg.shape[2]
    tile = lambda i, ea, eb, used: (jnp.minimum(i, used[0] - 1), 0)
    first = lambda i, ea, eb, used: (ea[i], 0, 0)
    second = lambda i, ea, eb, used: (eb[i], 0, 0)
    const = lambda shape: pl.BlockSpec(shape, lambda i, ea, eb, used: (0,) * len(shape),
                                       pipeline_mode=pl.Buffered(1))
    grid_spec = pltpu.PrefetchScalarGridSpec(
        num_scalar_prefetch=3,
        grid=(p // tm,),
        in_specs=[pl.BlockSpec((tm, d), tile), pl.BlockSpec((tm, wts.shape[1]), tile),
                  pl.BlockSpec((1, d, f), first), pl.BlockSpec((1, d, f), first), pl.BlockSpec((1, f, d), first),
                  pl.BlockSpec((1, d, f), second), pl.BlockSpec((1, d, f), second), pl.BlockSpec((1, f, d), second),
                  const(g2.shape), const(b2.shape)],
        out_specs=pl.BlockSpec((tm, d), lambda i, ea, eb, used: (i, 0)),
    )
    return pl.pallas_call(
        functools.partial(_moe_kernel, alpha=alpha),
        grid_spec=grid_spec,
        out_shape=jax.ShapeDtypeStruct((p, d), F32),
        compiler_params=_params("arbitrary"),
        name="moe",
    )(tile_ea, tile_eb, n_used, xg, wts, wg, wu, wd, wg, wu, wd, g2, b2)


def _routing_plan(meta, *, tm):
    n = meta.shape[1]
    n_tiles = (n + N_BUCKETS * (tm - 1) + tm - 1) // tm
    bucket = meta[0].astype(jnp.int32)
    onehot = (bucket[:, None] == jnp.arange(N_BUCKETS, dtype=jnp.int32)[None, :]).astype(jnp.int32)
    csum = jnp.cumsum(onehot, axis=0)
    rank = jnp.take_along_axis(csum, bucket[:, None], axis=1)[:, 0] - 1
    tiles_per_bucket = (csum[-1] + tm - 1) // tm
    tile_end = jnp.cumsum(tiles_per_bucket)
    n_used = tile_end[-1:]
    slot = (tile_end - tiles_per_bucket)[bucket] * tm + rank
    token_of_slot = jnp.zeros((n_tiles * tm,), jnp.int32).at[slot].set(jnp.arange(n, dtype=jnp.int32))
    tile_id = jnp.minimum(jnp.arange(n_tiles, dtype=jnp.int32), n_used - 1)
    tile_bucket = jnp.minimum(jnp.sum((tile_id[:, None] >= tile_end[None, :]).astype(jnp.int32), axis=1),
                              N_BUCKETS - 1)
    group, pair = tile_bucket // PAIRS_PER_GROUP, tile_bucket % PAIRS_PER_GROUP
    tile_ea = group * EXPERTS_PER_GROUP + jnp.asarray(PAIR_LO, jnp.int32)[pair]
    tile_eb = group * EXPERTS_PER_GROUP + jnp.asarray(PAIR_HI, jnp.int32)[pair]
    wts = jnp.zeros((n_tiles * tm, 128), F32).at[:, :2].set(meta[1:3].T[token_of_slot])
    return slot, token_of_slot, tile_ea, tile_eb, n_used.astype(jnp.int32), wts


def _rope_tables(pos, reps):
    half = QK_ROPE_DIM // 2
    inv = ROPE_BASE ** (-jnp.arange(half, dtype=F32) / half)
    ang = pos.astype(F32)[:, None] * inv
    cos, sin = jnp.cos(ang), jnp.sin(ang)
    return (jnp.tile(jnp.concatenate([cos, cos], axis=1), (1, reps)),
            jnp.tile(jnp.concatenate([-sin, sin], axis=1), (1, reps)))


def kernel(x_prompt, x_sample, cache_kv_latent, cache_k_rope, state_pool, page_table, w_in, g_q_norm, g_kv_norm,
           w_q_up, w_k_up, w_v_up, w_pool_mix, pool_scale, w_branch_a, w_branch_b, w_out, ln1_g, ln1_b,
           w_router, router_bias, w_exp_gate, w_exp_up, w_exp_down, ln2_g, ln2_b):
    bp, tp, d = x_prompt.shape
    bs, ts, _ = x_sample.shape
    depth = w_in.shape[0]
    n_pages = page_table.shape[1]
    past = n_pages * cache_kv_latent.shape[2]
    alpha = (2 * depth) ** 0.25
    pw = pool_scale.shape[1]
    n_p, n_s = bp * tp, bs * ts
    s1 = Q_LORA_RANK + KV_LORA_RANK + QK_ROPE_DIM

    tm_p = min(512, tp)
    tq = min(512, tp)
    tk = min(512, tp)
    tm_mix_p = min(512, tp)
    tm_s = min(256, n_s)
    tm_moe = 256
    tm_gather = min(1024, n_s)
    n_all = n_p + n_s

    cos_p, sin_p = _rope_tables(jnp.arange(tp, dtype=jnp.int32), N_HEADS)
    cos_pt, sin_pt = cos_p.T, sin_p.T
    cos_p, sin_p = cos_p[:, :QK_ROPE_DIM], sin_p[:, :QK_ROPE_DIM]
    cos_s, sin_s = _rope_tables(past + jnp.arange(ts, dtype=jnp.int32), N_HEADS)
    cos_s, sin_s = jnp.tile(cos_s, (tm_s // ts, 1)), jnp.tile(sin_s, (tm_s // ts, 1))
    prefix = jnp.pad(state_pool, ((0, 0), (0, 0), (POOL_PAD - state_pool.shape[2], 0), (0, 0)))
    cache_rt = jnp.swapaxes(cache_k_rope, 2, 3)
    wr_t = w_router.T
    rb = router_bias.reshape(N_EXPERTS, 1)

    x_p, x_s, x_s_off = x_prompt.reshape(n_p, d), x_sample.reshape(n_s, d), 0
    outs = [[] for _ in range(6)]
    for l in range(depth):
        wq = w_q_up[l]
        wq_nope = jnp.transpose(wq[:, :, :QK_NOPE_DIM], (1, 0, 2))
        wqr = (wq[:, :, QK_NOPE_DIM:] * Q_SCALE).reshape(Q_LORA_RANK, N_HEADS * QK_ROPE_DIM).astype(BF16)
        wk = jnp.transpose(w_k_up[l], (1, 0, 2))
        wv = jnp.transpose(w_v_up[l], (1, 0, 2))
        wba = w_branch_a[l].reshape(N_HEADS, -1, d)
        wlat, wlat_t, wva = _fold_weights(wq_nope, wk, wv, wba)
        w1 = w_in[l, :, :s1].astype(BF16)
        w2 = w_in[l, :, s1:].astype(BF16)
        gq = g_q_norm[l].reshape(1, -1)
        gkv = g_kv_norm[l].reshape(1, -1)
        mix_w = (w2, wva, w_pool_mix[l].astype(BF16), pool_scale[l].reshape(1, pw), w_branch_b[l].astype(BF16),
                 w_out[l].astype(BF16), ln1_g[l].reshape(1, d), ln1_b[l].reshape(1, d), wr_t, rb)
        wg, wu, wd = w_exp_gate[l].astype(BF16), w_exp_up[l].astype(BF16), w_exp_down[l].astype(BF16)
        g2, b2 = ln2_g[l].reshape(1, d), ln2_b[l].reshape(1, d)

        qt, kvcat, ckv_t, ckv, krope = _proj_prompt(x_p, n_p, w1, wlat_t, wqr.T, gq, gkv, cos_p, sin_p, cos_pt,
                                                    sin_pt, tm=tm_p)
        o = _attn_prompt(qt, kvcat, ckv_t, batch=bp, seq=tp, tq=tq, tk=tk)
        x1, meta, tail = _mix_prompt(x_p, 0, o, mix_w, n_all=n_all, tm=tm_mix_p, alpha=alpha, seq=tp)
        outs[0].append(ckv.reshape(bp, tp, -1))
        outs[1].append(krope.reshape(bp, tp, -1))
        outs[2].append(tail[:, 1:, :])

        q, kvcat, ckv, krope = _proj_sample(x_s, x_s_off, n_s, w1, wlat, wqr, gq, gkv, cos_s, sin_s, tm=tm_s)
        o = _attn_sample(page_table, q, kvcat, cache_kv_latent, cache_rt, layer=l, t_new=ts)
        x1, meta, tail = _mix_sample(x_s, x_s_off, o, prefix[l], x1, meta, mix_w, row0=n_p, tm=tm_s, alpha=alpha,
                                     t_new=ts, past=past)
        outs[3].append(ckv.reshape(bs, ts, -1))
        outs[4].append(krope.reshape(bs, ts, -1))
        outs[5].append(tail[:, 1:, :])

        slot, token_of_slot, tile_ea, tile_eb, n_used, wts = _routing_plan(meta, tm=tm_moe)
        xg = _sc_gather_rows(x1, token_of_slot)
        yg = _moe(xg, wts, tile_ea, tile_eb, n_used, wg, wu, wd, g2, b2, tm=tm_moe, alpha=alpha)
        x_all = _sc_gather_rows(yg, slot)
        x_p, x_s, x_s_off = x_all, x_all, n_p // tm_s

    return (x_all[:n_p].reshape(bp, tp, d), x_all[n_p:].reshape(bs, ts, d), jnp.stack(outs[0]),
            jnp.stack(outs[1]), jnp.stack(outs[2]), jnp.stack(outs[3]), jnp.stack(outs[4]), jnp.stack(outs[5]))
```

```python
import functools

import jax
import jax.numpy as jnp
from jax import lax
from jax.experimental import pallas as pl
from jax.experimental.pallas import tpu as pltpu
from jax.experimental.pallas import tpu_sc as plsc

F32 = jnp.float32
BF16 = jnp.bfloat16

N_HEADS = 8
QK_NOPE_DIM = 64
QK_ROPE_DIM = 32
Q_LORA_RANK = 256
KV_LORA_RANK = 128
QK_CAT = KV_LORA_RANK + QK_ROPE_DIM
ROPE_BASE = 10000.0
SM_SCALE = (QK_NOPE_DIM + QK_ROPE_DIM) ** -0.5
LOG2_E = 1.4426950408889634
Q_SCALE = SM_SCALE * LOG2_E
POOL_WINDOWS = (2, 4, 8, 16)
POOL_PAD = 16
N_EXPERTS = 16
N_EXPERT_GROUPS = 4
EXPERTS_PER_GROUP = 4
PAIRS_PER_GROUP = 6
PAIR_LO = (0, 0, 0, 1, 1, 2)
PAIR_HI = (1, 2, 3, 2, 3, 3)
N_BUCKETS = N_EXPERT_GROUPS * PAIRS_PER_GROUP
META_ROWS = 8
LN_EPS = 1e-5
RMS_EPS = 1e-6

LANES = 128
VMEM_LIMIT_BYTES = 56 * 1024 * 1024


def _params(*semantics):
    return pltpu.CompilerParams(dimension_semantics=semantics, vmem_limit_bytes=VMEM_LIMIT_BYTES)


def _const_spec(shape):
    zeros = (0,) * len(shape)
    return pl.BlockSpec(shape, lambda *_: zeros, pipeline_mode=pl.Buffered(1))


def _split_bf16(a):
    hi = a.astype(BF16)
    lo = (a - hi.astype(F32)).astype(BF16)
    return hi, lo


def _dot3(a, b, dims):
    a_hi, a_lo = _split_bf16(a)
    b_hi, b_lo = _split_bf16(b)
    d = functools.partial(lax.dot_general, dimension_numbers=dims, preferred_element_type=F32)
    return d(a_hi, b_hi) + (d(a_hi, b_lo) + d(a_lo, b_hi))


_NN = (((1,), (0,)), ((), ()))
_NT = (((1,), (1,)), ((), ()))


def _layer_norm(y, g, b):
    mu = jnp.mean(y, axis=-1, keepdims=True)
    d = y - mu
    var = jnp.mean(d * d, axis=-1, keepdims=True)
    return d * lax.rsqrt(var + LN_EPS) * g + b


def _rms_norm(y, g):
    return y * lax.rsqrt(jnp.mean(y * y, axis=-1, keepdims=True) + RMS_EPS) * g


def _fold_kernel(wqn_ref, wqn_t_ref, wk_ref, wk_t_ref, wv_ref, wba_ref, wlat_ref, wlat_t_ref, wva_ref):
    wlat_ref[...] = (_dot3(wqn_ref[0], wk_t_ref[0], _NN) * Q_SCALE).astype(BF16)
    wlat_t_ref[...] = (_dot3(wk_ref[0], wqn_t_ref[0], _NN) * Q_SCALE).astype(BF16)
    wva_ref[...] = _dot3(wv_ref[0], wba_ref[0], _NN).astype(BF16)


def _fold_weights(wq_nope, wk, wv, wba):
    h, r, dn = wq_nope.shape
    d = wba.shape[-1]
    head = lambda *blk: pl.BlockSpec((1,) + blk, lambda i: (i, 0, 0))
    return pl.pallas_call(
        _fold_kernel,
        grid=(h,),
        in_specs=[head(r, dn), head(dn, r), head(KV_LORA_RANK, dn), head(dn, KV_LORA_RANK),
                  head(KV_LORA_RANK, wv.shape[-1]), head(wba.shape[1], d)],
        out_specs=[pl.BlockSpec((r, KV_LORA_RANK), lambda i: (0, i)),
                   pl.BlockSpec((KV_LORA_RANK, r), lambda i: (i, 0)),
                   pl.BlockSpec((KV_LORA_RANK, d), lambda i: (i, 0))],
        out_shape=[jax.ShapeDtypeStruct((r, h * KV_LORA_RANK), BF16),
                   jax.ShapeDtypeStruct((h * KV_LORA_RANK, r), BF16),
                   jax.ShapeDtypeStruct((h * KV_LORA_RANK, d), BF16)],
        compiler_params=_params("arbitrary"),
        name="fold",
    )(wq_nope, jnp.swapaxes(wq_nope, 1, 2), wk, jnp.swapaxes(wk, 1, 2), wv, wba)


def _proj_common(x_ref, w1_ref, gq_ref, gkv_ref, cos_ref, sin_ref, ckv_ref, krope_ref):
    x = x_ref[...].astype(BF16)
    h = jnp.dot(x, w1_ref[...], preferred_element_type=F32)
    qa = h[:, :Q_LORA_RANK]
    kva = h[:, Q_LORA_RANK:Q_LORA_RANK + KV_LORA_RANK]
    kr = h[:, Q_LORA_RANK + KV_LORA_RANK:]
    qn = _rms_norm(qa, gq_ref[...])
    ckv = _rms_norm(kva, gkv_ref[...])
    half = QK_ROPE_DIM // 2
    kr_swapped = jnp.concatenate([kr[:, half:], kr[:, :half]], axis=1)
    krope = kr * cos_ref[:, :QK_ROPE_DIM] + kr_swapped * sin_ref[:, :QK_ROPE_DIM]
    ckv_ref[...] = ckv
    krope_ref[...] = krope
    return qn, ckv, krope


def _proj_sample_kernel(x_ref, w1_ref, wlat_ref, wqr_ref, gq_ref, gkv_ref, cos_ref, sin_ref,
                        q_ref, kvcat_ref, ckv_ref, krope_ref):
    qn, ckv, krope = _proj_common(x_ref, w1_ref, gq_ref, gkv_ref, cos_ref, sin_ref, ckv_ref, krope_ref)
    qn = qn.astype(BF16)
    ql = jnp.dot(qn, wlat_ref[...], preferred_element_type=F32)
    qr = jnp.dot(qn, wqr_ref[...], preferred_element_type=F32)
    half = QK_ROPE_DIM // 2
    width = qr.shape[1]
    lane = lax.broadcasted_iota(jnp.int32, qr.shape, 1)
    swapped = jnp.where((lane % QK_ROPE_DIM) < half,
                        pltpu.roll(qr, width - half, 1), pltpu.roll(qr, half, 1))
    qrr = qr * cos_ref[...] + swapped * sin_ref[...]
    for hh in range(N_HEADS):
        q_ref[hh, :, :KV_LORA_RANK] = ql[:, hh * KV_LORA_RANK:(hh + 1) * KV_LORA_RANK]
        q_ref[hh, :, KV_LORA_RANK:] = qrr[:, hh * QK_ROPE_DIM:(hh + 1) * QK_ROPE_DIM]
    kvcat_ref[:, :KV_LORA_RANK] = ckv
    kvcat_ref[:, KV_LORA_RANK:] = krope


def _proj_prompt_kernel(x_ref, w1_ref, wlat_t_ref, wqr_t_ref, gq_ref, gkv_ref, cos_ref, sin_ref,
                        cos_t_ref, sin_t_ref, qt_ref, kvcat_ref, ckv_t_ref, ckv_ref, krope_ref):
    qn, ckv, krope = _proj_common(x_ref, w1_ref, gq_ref, gkv_ref, cos_ref, sin_ref, ckv_ref, krope_ref)
    qn_t = qn.T.astype(BF16)
    ql_t = jnp.dot(wlat_t_ref[...], qn_t, preferred_element_type=F32)
    qr_t = jnp.dot(wqr_t_ref[...], qn_t, preferred_element_type=F32)
    half = QK_ROPE_DIM // 2
    pieces = []
    for hh in range(N_HEADS):
        base = hh * QK_ROPE_DIM
        pieces += [qr_t[base + half:base + QK_ROPE_DIM], qr_t[base:base + half]]
    qrr_t = qr_t * cos_t_ref[...] + jnp.concatenate(pieces, axis=0) * sin_t_ref[...]
    for hh in range(N_HEADS):
        qt_ref[hh, :KV_LORA_RANK, :] = ql_t[hh * KV_LORA_RANK:(hh + 1) * KV_LORA_RANK].astype(BF16)
        qt_ref[hh, KV_LORA_RANK:, :] = qrr_t[hh * QK_ROPE_DIM:(hh + 1) * QK_ROPE_DIM].astype(BF16)
    kvcat_ref[:, :KV_LORA_RANK] = ckv.astype(BF16)
    kvcat_ref[:, KV_LORA_RANK:] = krope.astype(BF16)
    ckv_t_ref[...] = ckv.T.astype(BF16)


def _proj_sample(x, x_off, n, w1, wlat, wqr, gq, gkv, cos_t, sin_t, *, tm):
    d = x.shape[1]
    row = lambda width: pl.BlockSpec((tm, width), lambda i: (i, 0))
    return pl.pallas_call(
        _proj_sample_kernel,
        grid=(n // tm,),
        in_specs=[pl.BlockSpec((tm, d), lambda i: (i + x_off, 0)), _const_spec(w1.shape), _const_spec(wlat.shape), _const_spec(wqr.shape),
                  _const_spec(gq.shape), _const_spec(gkv.shape),
                  _const_spec(cos_t.shape), _const_spec(sin_t.shape)],
        out_specs=[pl.BlockSpec((N_HEADS, tm, QK_CAT), lambda i: (0, i, 0)),
                   row(QK_CAT), row(KV_LORA_RANK), row(QK_ROPE_DIM)],
        out_shape=[jax.ShapeDtypeStruct((N_HEADS, n, QK_CAT), F32),
                   jax.ShapeDtypeStruct((n, QK_CAT), F32),
                   jax.ShapeDtypeStruct((n, KV_LORA_RANK), F32),
                   jax.ShapeDtypeStruct((n, QK_ROPE_DIM), F32)],
        compiler_params=_params("arbitrary"),
        name="proj_sample",
    )(x, w1, wlat, wqr, gq, gkv, cos_t, sin_t)


def _proj_prompt(x, n, w1, wlat_t, wqr_t, gq, gkv, cos, sin, cos_t, sin_t, *, tm):
    d = x.shape[1]
    tiles = cos.shape[0] // tm
    row = lambda width: pl.BlockSpec((tm, width), lambda i: (i, 0))
    return pl.pallas_call(
        _proj_prompt_kernel,
        grid=(n // tm,),
        in_specs=[row(d), _const_spec(w1.shape), _const_spec(wlat_t.shape), _const_spec(wqr_t.shape),
                  _const_spec(gq.shape), _const_spec(gkv.shape),
                  pl.BlockSpec((tm, cos.shape[1]), lambda i: (i % tiles, 0)),
                  pl.BlockSpec((tm, sin.shape[1]), lambda i: (i % tiles, 0)),
                  pl.BlockSpec((cos_t.shape[0], tm), lambda i: (0, i % tiles)),
                  pl.BlockSpec((sin_t.shape[0], tm), lambda i: (0, i % tiles))],
        out_specs=[pl.BlockSpec((N_HEADS, QK_CAT, tm), lambda i: (0, 0, i)),
                   row(QK_CAT),
                   pl.BlockSpec((KV_LORA_RANK, tm), lambda i: (0, i)),
                   row(KV_LORA_RANK), row(QK_ROPE_DIM)],
        out_shape=[jax.ShapeDtypeStruct((N_HEADS, QK_CAT, n), BF16),
                   jax.ShapeDtypeStruct((n, QK_CAT), BF16),
                   jax.ShapeDtypeStruct((KV_LORA_RANK, n), BF16),
                   jax.ShapeDtypeStruct((n, KV_LORA_RANK), F32),
                   jax.ShapeDtypeStruct((n, QK_ROPE_DIM), F32)],
        compiler_params=_params("arbitrary"),
        name="proj_prompt",
    )(x, w1, wlat_t, wqr_t, gq, gkv, cos, sin, cos_t, sin_t)


SCORE_AHEAD = 2
SCORE_SLOTS = 4


def _col_reduce(x, op, reduce_rows):
    rows = x.shape[0]
    while rows >= 32:
        x = x.reshape(4, rows // 4, x.shape[1])
        x = op(op(x[0], x[1]), op(x[2], x[3]))
        rows //= 4
    return reduce_rows(x, axis=0, keepdims=True)


def _attn_prompt_kernel(qt_ref, k_ref, vt_ref, o_ref, m_ref, l_ref, acc_ref, s_ref, *, tq, tk):
    i = pl.program_id(1)
    n_full = (i * tq) // tk
    m_ref[...] = jnp.full(m_ref.shape, -jnp.inf, F32)
    l_ref[...] = jnp.zeros(l_ref.shape, F32)
    acc_ref[...] = jnp.zeros(acc_ref.shape, F32)

    def scores(j, hh):
        k = k_ref[pl.ds(pl.multiple_of(j * tk, tk), tk), :]
        s_ref[hh % SCORE_SLOTS] = jnp.dot(k, qt_ref[hh], preferred_element_type=F32)

    def step(j, masked):
        vt = vt_ref[:, pl.ds(pl.multiple_of(j * tk, tk), tk)]
        if masked:
            key = j * tk + lax.broadcasted_iota(jnp.int32, (tk, tq), 0)
            tok = i * tq + lax.broadcasted_iota(jnp.int32, (tk, tq), 1)
            visible = key <= tok
        for hh in range(N_HEADS):
            ahead = hh + SCORE_AHEAD
            if ahead < N_HEADS:
                scores(j, ahead)
            elif not masked:
                scores(j + 1, ahead - N_HEADS)
            s = s_ref[hh % SCORE_SLOTS]
            if masked:
                s = jnp.where(visible, s, -jnp.inf)
            m_prev = m_ref[hh]
            m_new = jnp.maximum(m_prev, _col_reduce(s, jnp.maximum, jnp.max))
            alpha = jnp.exp2(m_prev - m_new)
            p = jnp.exp2(s - m_new)
            l_ref[hh] = alpha * l_ref[hh] + _col_reduce(p, jnp.add, jnp.sum)
            pv = jnp.dot(vt, p.astype(BF16), preferred_element_type=F32)
            acc_ref[hh] = alpha * acc_ref[hh] + pv
            m_ref[hh] = m_new

    def body(j, carry):
        step(j, False)
        return carry

    for hh in range(SCORE_AHEAD):
        scores(0, hh)
    lax.fori_loop(0, n_full, body, 0)
    step(n_full, True)
    for hh in range(N_HEADS):
        o_t = acc_ref[hh] / l_ref[hh]
        o_ref[:, hh * KV_LORA_RANK:(hh + 1) * KV_LORA_RANK] = o_t.T.astype(o_ref.dtype)


def _attn_prompt(qt, kvcat, ckv_t, *, batch, seq, tq, tk):
    nq = seq // tq
    n = batch * seq
    return pl.pallas_call(
        functools.partial(_attn_prompt_kernel, tq=tq, tk=tk),
        grid=(batch, nq),
        in_specs=[pl.BlockSpec((N_HEADS, QK_CAT, tq), lambda b, i: (0, 0, b * nq + i)),
                  pl.BlockSpec((seq, QK_CAT), lambda b, i: (b, 0)),
                  pl.BlockSpec((KV_LORA_RANK, seq), lambda b, i: (0, b))],
        out_specs=pl.BlockSpec((tq, N_HEADS * KV_LORA_RANK), lambda b, i: (b * nq + i, 0)),
        out_shape=jax.ShapeDtypeStruct((n, N_HEADS * KV_LORA_RANK), BF16),
        scratch_shapes=[pltpu.VMEM((N_HEADS, 1, tq), F32), pltpu.VMEM((N_HEADS, 1, tq), F32),
                        pltpu.VMEM((N_HEADS, KV_LORA_RANK, tq), F32), pltpu.VMEM((SCORE_SLOTS, tk, tq), F32)],
        compiler_params=_params("arbitrary", "arbitrary"),
        name="attn_prompt",
    )(qt, kvcat, ckv_t)


def _attn_sample_kernel(pt_ref, q_ref, knew_ref, cache_c_ref, cache_rt_ref, o_ref, cbuf, rbuf, sem,
                        *, layer, n_pages, page, t_new):
    b = pl.program_id(0)
    nb = pl.num_programs(0)

    def page_copies(seq_idx, slot):
        copies = []
        for p in range(n_pages):
            pid = pt_ref[seq_idx, p]
            copies.append(pltpu.make_async_copy(
                cache_c_ref.at[layer, pid], cbuf.at[slot, pl.ds(p * page, page), :], sem.at[slot, 0]))
            copies.append(pltpu.make_async_copy(
                cache_rt_ref.at[layer, pid], rbuf.at[slot, :, pl.ds(p * page, page)], sem.at[slot, 1]))
        return copies

    slot = b % 2

    @pl.when(b == 0)
    def _():
        for c in page_copies(0, 0):
            c.start()

    @pl.when(b + 1 < nb)
    def _():
        for c in page_copies(b + 1, 1 - slot):
            c.start()

    for c in page_copies(b, slot):
        c.wait()

    rows = N_HEADS * t_new
    q = q_ref[...].reshape(rows, QK_CAT).astype(BF16)
    kc = cbuf[slot].astype(BF16)
    krt = rbuf[slot].astype(BF16)
    kn = knew_ref[...].astype(BF16)
    s = (lax.dot_general(q[:, :KV_LORA_RANK], kc, _NT, preferred_element_type=F32)
         + jnp.dot(q[:, KV_LORA_RANK:], krt, preferred_element_type=F32))
    sn = lax.dot_general(q, kn, _NT, preferred_element_type=F32)
    tok = lax.broadcasted_iota(jnp.int32, (N_HEADS, t_new, t_new), 1).reshape(rows, t_new)
    key = lax.broadcasted_iota(jnp.int32, (rows, t_new), 1)
    sn = jnp.where(key <= tok, sn, -jnp.inf)
    m = jnp.maximum(jnp.max(s, axis=-1, keepdims=True), jnp.max(sn, axis=-1, keepdims=True))
    p = jnp.exp2(s - m)
    pn = jnp.exp2(sn - m)
    l = jnp.sum(p, axis=-1, keepdims=True) + jnp.sum(pn, axis=-1, keepdims=True)
    o = (jnp.dot(p.astype(BF16), kc, preferred_element_type=F32)
         + jnp.dot(pn.astype(BF16), kn[:, :KV_LORA_RANK], preferred_element_type=F32)) / l
    for hh in range(N_HEADS):
        o_ref[:, hh * KV_LORA_RANK:(hh + 1) * KV_LORA_RANK] = o[hh * t_new:(hh + 1) * t_new]


def _attn_sample(page_table, q, knew, cache_c, cache_rt, *, layer, t_new):
    n_seq, n_pages = page_table.shape
    page = cache_c.shape[2]
    past = n_pages * page
    grid_spec = pltpu.PrefetchScalarGridSpec(
        num_scalar_prefetch=1,
        grid=(n_seq,),
        in_specs=[pl.BlockSpec((N_HEADS, t_new, QK_CAT), lambda b, pt: (0, b, 0)),
                  pl.BlockSpec((t_new, QK_CAT), lambda b, pt: (b, 0)),
                  pl.BlockSpec(memory_space=pl.ANY),
                  pl.BlockSpec(memory_space=pl.ANY)],
        out_specs=pl.BlockSpec((t_new, N_HEADS * KV_LORA_RANK), lambda b, pt: (b, 0)),
        scratch_shapes=[pltpu.VMEM((2, past, KV_LORA_RANK), F32), pltpu.VMEM((2, QK_ROPE_DIM, past), F32),
                        pltpu.SemaphoreType.DMA((2, 2))],
    )
    return pl.pallas_call(
        functools.partial(_attn_sample_kernel, layer=layer, n_pages=n_pages, page=page, t_new=t_new),
        grid_spec=grid_spec,
        out_shape=jax.ShapeDtypeStruct((n_seq * t_new, N_HEADS * KV_LORA_RANK), F32),
        compiler_params=_params("arbitrary"),
        name="attn_sample",
    )(page_table, q, knew, cache_c, cache_rt)


def _router_meta(x1, wr, rb):
    logits = _dot3(wr, x1, _NT)
    score = jax.nn.sigmoid(logits)
    biased = score + rb
    row = lambda a, e: a[e:e + 1, :]
    group_score = []
    for g in range(N_EXPERT_GROUPS):
        v = [row(biased, g * EXPERTS_PER_GROUP + k) for k in range(EXPERTS_PER_GROUP)]
        best = None
        for a in range(EXPERTS_PER_GROUP):
            for c in range(a + 1, EXPERTS_PER_GROUP):
                pair = v[a] + v[c]
                best = pair if best is None else jnp.maximum(best, pair)
        group_score.append(best)
    top = group_score[0]
    grp = jnp.zeros(top.shape, jnp.int32)
    for g in range(1, N_EXPERT_GROUPS):
        better = group_score[g] > top
        grp = jnp.where(better, g, grp)
        top = jnp.maximum(top, group_score[g])

    def pick(a, k):
        out = row(a, (N_EXPERT_GROUPS - 1) * EXPERTS_PER_GROUP + k)
        for g in range(N_EXPERT_GROUPS - 2, -1, -1):
            out = jnp.where(grp == g, row(a, g * EXPERTS_PER_GROUP + k), out)
        return out

    cand = [pick(biased, k) for k in range(EXPERTS_PER_GROUP)]
    aff = [pick(score, k) for k in range(EXPERTS_PER_GROUP)]
    first = jnp.zeros(top.shape, jnp.int32)
    best = cand[0]
    for k in range(1, EXPERTS_PER_GROUP):
        better = cand[k] > best
        first = jnp.where(better, k, first)
        best = jnp.maximum(best, cand[k])
    second = jnp.full(top.shape, -1, jnp.int32)
    best2 = jnp.full(top.shape, -jnp.inf, F32)
    for k in range(EXPERTS_PER_GROUP):
        better = (first != k) & ((second < 0) | (cand[k] > best2))
        second = jnp.where(better, k, second)
        best2 = jnp.where(better, cand[k], best2)
    lo = jnp.minimum(first, second)
    hi = jnp.maximum(first, second)

    def take(vals, k_idx):
        out = vals[EXPERTS_PER_GROUP - 1]
        for k in range(EXPERTS_PER_GROUP - 2, -1, -1):
            out = jnp.where(k_idx == k, vals[k], out)
        return out

    aff_lo, aff_hi = take(aff, lo), take(aff, hi)
    denom = aff_lo + aff_hi
    pair = jnp.where(lo == 0, 0, jnp.where(lo == 1, 3, 5)) + (hi - lo - 1)
    bucket = (grp * PAIRS_PER_GROUP + pair).astype(F32)
    pad = jnp.zeros((META_ROWS - 3,) + top.shape[1:], F32)
    return jnp.concatenate([bucket, aff_lo / denom, aff_hi / denom, pad], axis=0)


def _mix_tail(x, o, pooled, gate_a, gate_b, wva_ref, wmix_ref, ps_ref, wbb_ref, wout_ref, g_ref, b_ref,
              wr_ref, rb_ref, x1_ref, gate_ref, *, alpha):
    gd = wmix_ref.shape[1]
    yb = jnp.concatenate(
        [jnp.dot(pooled[:, g * gd:(g + 1) * gd].astype(BF16), wmix_ref[g], preferred_element_type=F32)
         for g in range(len(POOL_WINDOWS))], axis=1) * ps_ref[...]
    ya = jnp.dot(o.astype(BF16), wva_ref[...], preferred_element_type=F32)
    yb = jnp.dot(yb.astype(BF16), wbb_ref[...], preferred_element_type=F32)
    merged = jax.nn.sigmoid(gate_a) * ya + jax.nn.sigmoid(gate_b) * yb
    mix = jnp.dot(merged.astype(BF16), wout_ref[...], preferred_element_type=F32)
    x1 = _layer_norm(alpha * x + mix, g_ref[...], b_ref[...])
    x1_ref[...] = x1
    gate_ref[...] = _router_meta(x1, wr_ref[...], rb_ref[...])


def _mix_prompt_kernel(*refs, alpha, tiles_per_seq, n_tiles):
    x1_ref, gate_ref = refs[-4], refs[-3]

    @pl.when(pl.program_id(0) < n_tiles)
    def _():
        _mix_prompt_tile(*refs, alpha=alpha, tiles_per_seq=tiles_per_seq)

    @pl.when(pl.program_id(0) >= n_tiles)
    def _():
        x1_ref[...] = jnp.zeros(x1_ref.shape, F32)
        gate_ref[...] = jnp.zeros(gate_ref.shape, F32)


def _mix_prompt_tile(x_ref, o_ref, w2_ref, wva_ref, wmix_ref, ps_ref, wbb_ref, wout_ref, g_ref, b_ref,
                     wr_ref, rb_ref, x1_ref, gate_ref, tail_ref, ext_ref, *, alpha, tiles_per_seq):
    tm = x_ref.shape[0]
    pw = ps_ref.shape[1]
    gd = pw // len(POOL_WINDOWS)
    x = x_ref[...]
    h2 = jnp.dot(x.astype(BF16), w2_ref[...], preferred_element_type=F32)
    u = h2[:, :pw]
    t = pl.program_id(0) % tiles_per_seq

    @pl.when(t == 0)
    def _():
        ext_ref[0:POOL_PAD, :] = jnp.zeros((POOL_PAD, pw), F32)

    @pl.when(t != 0)
    def _():
        ext_ref[0:POOL_PAD, :] = ext_ref[tm:tm + POOL_PAD, :]

    ext_ref[POOL_PAD:POOL_PAD + tm, :] = u
    tail_ref[0] = u[tm - POOL_PAD:, :]
    pos = t * tm + lax.broadcasted_iota(jnp.int32, (tm, 1), 0)
    pooled = []
    for g, w in enumerate(POOL_WINDOWS):
        acc = ext_ref[POOL_PAD:POOL_PAD + tm, g * gd:(g + 1) * gd]
        for j in range(1, w):
            acc = acc + ext_ref[POOL_PAD - j:POOL_PAD - j + tm, g * gd:(g + 1) * gd]
        cnt = jnp.minimum(w, pos + 1).astype(F32)
        pooled.append(acc / cnt - u[:, g * gd:(g + 1) * gd])
    pooled = jnp.concatenate(pooled, axis=1)
    d = x.shape[1]
    _mix_tail(x, o_ref[...], pooled, h2[:, pw:pw + d], h2[:, pw + d:], wva_ref, wmix_ref, ps_ref, wbb_ref,
              wout_ref, g_ref, b_ref, wr_ref, rb_ref, x1_ref, gate_ref, alpha=alpha)


def _mix_sample_kernel(x_ref, o_ref, pre_ref, x1_all_ref, meta_all_ref, w2_ref, wva_ref, wmix_ref, ps_ref, wbb_ref,
                       wout_ref, g_ref, b_ref, wr_ref, rb_ref, x1_ref, gate_ref, tail_ref, ext_ref,
                       *, alpha, t_new, past):
    del x1_all_ref, meta_all_ref
    tm = x_ref.shape[0]
    sb = tm // t_new
    pw = ps_ref.shape[1]
    gd = pw // len(POOL_WINDOWS)
    x = x_ref[...]
    h2 = jnp.dot(x.astype(BF16), w2_ref[...], preferred_element_type=F32)
    u = h2[:, :pw]
    ext_ref[:, 0:POOL_PAD, :] = pre_ref[...]
    ext_ref[:, POOL_PAD:POOL_PAD + t_new, :] = u.reshape(sb, t_new, pw)
    tail_ref[...] = ext_ref[:, t_new:t_new + POOL_PAD, :]
    pos = past + lax.broadcasted_iota(jnp.int32, (sb, t_new, 1), 1)
    pooled = []
    for g, w in enumerate(POOL_WINDOWS):
        acc = ext_ref[:, POOL_PAD:POOL_PAD + t_new, g * gd:(g + 1) * gd]
        for j in range(1, w):
            acc = acc + ext_ref[:, POOL_PAD - j:POOL_PAD - j + t_new, g * gd:(g + 1) * gd]
        cnt = jnp.minimum(w, pos + 1).astype(F32)
        pooled.append((acc / cnt).reshape(tm, gd) - u[:, g * gd:(g + 1) * gd])
    pooled = jnp.concatenate(pooled, axis=1)
    d = x.shape[1]
    _mix_tail(x, o_ref[...], pooled, h2[:, pw:pw + d], h2[:, pw + d:], wva_ref, wmix_ref, ps_ref, wbb_ref,
              wout_ref, g_ref, b_ref, wr_ref, rb_ref, x1_ref, gate_ref, alpha=alpha)


def _mix_prompt(x, x_off, o, weights, *, n_all, tm, alpha, seq):
    n = o.shape[0]
    d = x.shape[1]
    pw = weights[3].shape[1]
    tiles_per_seq = seq // tm
    n_tiles = n // tm
    last = n_tiles - 1
    return pl.pallas_call(
        functools.partial(_mix_prompt_kernel, alpha=alpha, tiles_per_seq=tiles_per_seq, n_tiles=n_tiles),
        grid=(n_all // tm,),
        in_specs=[pl.BlockSpec((tm, d), lambda i: (jnp.minimum(i, last) + x_off, 0)),
                  pl.BlockSpec((tm, o.shape[1]), lambda i: (jnp.minimum(i, last), 0))]
        + [_const_spec(a.shape) for a in weights],
        out_specs=[pl.BlockSpec((tm, d), lambda i: (i, 0)),
                   pl.BlockSpec((META_ROWS, tm), lambda i: (0, i)),
                   pl.BlockSpec((1, POOL_PAD, pw), lambda i: (jnp.minimum(i, last) // tiles_per_seq, 0, 0))],
        out_shape=[jax.ShapeDtypeStruct((n_all, d), F32), jax.ShapeDtypeStruct((META_ROWS, n_all), F32),
                   jax.ShapeDtypeStruct((n // seq, POOL_PAD, pw), F32)],
        scratch_shapes=[pltpu.VMEM((tm + POOL_PAD, pw), F32)],
        compiler_params=_params("arbitrary"),
        name="mix_prompt",
    )(x, o, *weights)


def _mix_sample(x, x_off, o, prefix, x1_all, meta_all, weights, *, row0, tm, alpha, t_new, past):
    n, d = o.shape
    pw = weights[3].shape[1]
    sb = tm // t_new
    off = row0 // tm
    any_spec = pl.BlockSpec(memory_space=pl.ANY)
    return pl.pallas_call(
        functools.partial(_mix_sample_kernel, alpha=alpha, t_new=t_new, past=past),
        grid=(n // tm,),
        in_specs=[pl.BlockSpec((tm, d), lambda i: (i + x_off, 0)), pl.BlockSpec((tm, d), lambda i: (i, 0)),
                  pl.BlockSpec((sb, POOL_PAD, pw), lambda i: (i, 0, 0)), any_spec, any_spec]
        + [_const_spec(a.shape) for a in weights],
        out_specs=[pl.BlockSpec((tm, d), lambda i: (i + off, 0)),
                   pl.BlockSpec((META_ROWS, tm), lambda i: (0, i + off)),
                   pl.BlockSpec((sb, POOL_PAD, pw), lambda i: (i, 0, 0))],
        out_shape=[jax.ShapeDtypeStruct(x1_all.shape, F32), jax.ShapeDtypeStruct(meta_all.shape, F32),
                   jax.ShapeDtypeStruct((n // t_new, POOL_PAD, pw), F32)],
        input_output_aliases={3: 0, 4: 1},
        scratch_shapes=[pltpu.VMEM((sb, POOL_PAD + t_new, pw), F32)],
        compiler_params=_params("arbitrary"),
        name="mix_sample",
    )(x, o, prefix, x1_all, meta_all, *weights)


GATHER_UNROLL = 8


def _gather_rows_kernel(idx_ref, tbl_ref, out_ref):
    tm = out_ref.shape[0]
    base = pl.program_id(1) * tm

    def body(g, carry):
        for u in range(GATHER_UNROLL):
            r = g * GATHER_UNROLL + u
            out_ref[pl.ds(r, 1), :] = tbl_ref[pl.ds(idx_ref[base + r], 1), :]
        return carry

    lax.fori_loop(0, tm // GATHER_UNROLL, body, 0)


def _gather_rows(table, idx, *, tm, cw):
    r, d = table.shape
    m = idx.shape[0]
    grid_spec = pltpu.PrefetchScalarGridSpec(
        num_scalar_prefetch=1,
        grid=(d // cw, m // tm),
        in_specs=[pl.BlockSpec((r, cw), lambda c, i, idx_ref: (0, c), pipeline_mode=pl.Buffered(1))],
        out_specs=pl.BlockSpec((tm, cw), lambda c, i, idx_ref: (i, c)),
    )
    return pl.pallas_call(
        _gather_rows_kernel,
        grid_spec=grid_spec,
        out_shape=jax.ShapeDtypeStruct((m, d), table.dtype),
        compiler_params=_params("arbitrary", "arbitrary"),
        name="gather_rows",
    )(idx, table)


SC_WORKERS = 32
SC_CHUNK = 16
SC_BUFFERS = 4


def _sc_gather_rows(table, idx):
    m = idx.shape[0]
    d = table.shape[1]
    per_worker = m // SC_WORKERS
    n_chunks = per_worker // SC_CHUNK
    assert per_worker * SC_WORKERS == m and n_chunks * SC_CHUNK == per_worker, (m, SC_WORKERS, SC_CHUNK)
    mesh = plsc.VectorSubcoreMesh(core_axis_name="c", subcore_axis_name="s")

    @functools.partial(
        pl.kernel, mesh=mesh, out_type=jax.ShapeDtypeStruct((m, d), table.dtype),
        scratch_types=[pltpu.VMEM((per_worker,), jnp.int32), pltpu.VMEM((SC_BUFFERS, SC_CHUNK, d), table.dtype),
                       pltpu.SemaphoreType.DMA((SC_BUFFERS,)), pltpu.SemaphoreType.DMA((SC_BUFFERS,))],
        name="sc_gather_rows")
    def gather(table_hbm, idx_hbm, out_hbm, idx_v, rows_v, gather_sem, write_sem):
        worker = lax.axis_index("s") * mesh.num_cores + lax.axis_index("c")
        base = worker * per_worker
        pltpu.sync_copy(idx_hbm.at[pl.ds(pl.multiple_of(base, SC_CHUNK), per_worker)], idx_v)

        def fetch(c, b):
            rows = idx_v.at[pl.ds(pl.multiple_of(c * SC_CHUNK, SC_CHUNK), SC_CHUNK)]
            return pltpu.make_async_copy(table_hbm.at[rows], rows_v.at[b], gather_sem.at[b])

        def write(c, b):
            dst = out_hbm.at[pl.ds(pl.multiple_of(base + c * SC_CHUNK, SC_CHUNK), SC_CHUNK)]
            return pltpu.make_async_copy(rows_v.at[b], dst, write_sem.at[b])

        for b in range(min(SC_BUFFERS, n_chunks)):
            fetch(b, b).start()

        @pl.loop(0, pl.cdiv(n_chunks, SC_BUFFERS))
        def _(p):
            for b in range(SC_BUFFERS):
                c = p * SC_BUFFERS + b

                @pl.when(c < n_chunks)
                def _():
                    fetch(c, b).wait()
                    write(c, b).start()

                    @pl.when(c + SC_BUFFERS < n_chunks)
                    def _():
                        write(c, b).wait()
                        fetch(c + SC_BUFFERS, b).start()

        for b in range(min(SC_BUFFERS, n_chunks)):
            last = ((n_chunks - 1 - b) // SC_BUFFERS) * SC_BUFFERS + b
            write(last, b).wait()

    return gather(table, idx)


def _moe_kernel(ea_ref, eb_ref, used_ref, x_ref, wt_ref, wga_ref, wua_ref, wda_ref, wgb_ref, wub_ref, wdb_ref,
                g_ref, b_ref, o_ref, *, alpha):
    del ea_ref, eb_ref

    @pl.when(pl.program_id(0) < used_ref[0])
    def _():
        x = x_ref[...]
        xb = x.astype(BF16)
        wt = wt_ref[...]

        def expert(wg_ref, wu_ref, wd_ref, col):
            hg = jnp.dot(xb, wg_ref[0], preferred_element_type=F32)
            hu = jnp.dot(xb, wu_ref[0], preferred_element_type=F32)
            hid = (jax.nn.silu(hg) * hu * col).astype(BF16)
            return jnp.dot(hid, wd_ref[0], preferred_element_type=F32)

        y = expert(wga_ref, wua_ref, wda_ref, wt[:, 0:1]) + expert(wgb_ref, wub_ref, wdb_ref, wt[:, 1:2])
        o_ref[...] = _layer_norm(alpha * x + y, g_ref[...], b_ref[...])

    @pl.when(pl.program_id(0) >= used_ref[0])
    def _():
        o_ref[...] = jnp.zeros(o_ref.shape, F32)


def _moe(xg, wts, tile_ea, tile_eb, n_used, wg, wu, wd, g2, b2, *, tm, alpha):
    p, d = xg.shape
    f = wg.shape[2]
    tile = lambda i, ea, eb, used: (jnp.minimum(i, used[0] - 1), 0)
    first = lambda i, ea, eb, used: (ea[i], 0, 0)
    second = lambda i, ea, eb, used: (eb[i], 0, 0)
    const = lambda shape: pl.BlockSpec(shape, lambda i, ea, eb, used: (0,) * len(shape),
                                       pipeline_mode=pl.Buffered(1))
    grid_spec = pltpu.PrefetchScalarGridSpec(
        num_scalar_prefetch=3,
        grid=(p // tm,),
        in_specs=[pl.BlockSpec((tm, d), tile), pl.BlockSpec((tm, wts.shape[1]), tile),
                  pl.BlockSpec((1, d, f), first), pl.BlockSpec((1, d, f), first), pl.BlockSpec((1, f, d), first),
                  pl.BlockSpec((1, d, f), second), pl.BlockSpec((1, d, f), second), pl.BlockSpec((1, f, d), second),
                  const(g2.shape), const(b2.shape)],
        out_specs=pl.BlockSpec((tm, d), lambda i, ea, eb, used: (i, 0)),
    )
    return pl.pallas_call(
        functools.partial(_moe_kernel, alpha=alpha),
        grid_spec=grid_spec,
        out_shape=jax.ShapeDtypeStruct((p, d), F32),
        compiler_params=_params("arbitrary"),
        name="moe",
    )(tile_ea, tile_eb, n_used, xg, wts, wg, wu, wd, wg, wu, wd, g2, b2)


def _routing_plan(meta, *, tm):
    n = meta.shape[1]
    n_tiles = (n + N_BUCKETS * (tm - 1) + tm - 1) // tm
    bucket = meta[0].astype(jnp.int32)
    onehot = (bucket[:, None] == jnp.arange(N_BUCKETS, dtype=jnp.int32)[None, :]).astype(jnp.int32)
    csum = jnp.cumsum(onehot, axis=0)
    rank = jnp.sum(csum * onehot, axis=1) - 1
    tiles_per_bucket = (csum[-1] + tm - 1) // tm
    tile_end = jnp.cumsum(tiles_per_bucket)
    n_used = tile_end[-1:]
    row_start = (tile_end - tiles_per_bucket) * tm
    slot = jnp.sum(onehot * row_start[None, :], axis=1) + rank
    token_of_slot = jnp.zeros((n_tiles * tm,), jnp.int32).at[slot].set(
        jnp.arange(n, dtype=jnp.int32), unique_indices=True)
    tile_id = jnp.minimum(jnp.arange(n_tiles, dtype=jnp.int32), n_used - 1)
    tile_bucket = jnp.minimum(jnp.sum((tile_id[:, None] >= tile_end[None, :]).astype(jnp.int32), axis=1),
                              N_BUCKETS - 1)
    group, pair = tile_bucket // PAIRS_PER_GROUP, tile_bucket % PAIRS_PER_GROUP
    lo = sum((pair == k).astype(jnp.int32) * PAIR_LO[k] for k in range(PAIRS_PER_GROUP))
    hi = sum((pair == k).astype(jnp.int32) * PAIR_HI[k] for k in range(PAIRS_PER_GROUP))
    tile_ea = group * EXPERTS_PER_GROUP + lo
    tile_eb = group * EXPERTS_PER_GROUP + hi
    weights = jnp.pad(meta[1:3].T, ((0, 0), (0, LANES - 2)))
    return slot, token_of_slot, tile_ea, tile_eb, n_used.astype(jnp.int32), weights


def _rope_tables(pos, reps):
    half = QK_ROPE_DIM // 2
    inv = ROPE_BASE ** (-jnp.arange(half, dtype=F32) / half)
    ang = pos.astype(F32)[:, None] * inv
    cos, sin = jnp.cos(ang), jnp.sin(ang)
    return (jnp.tile(jnp.concatenate([cos, cos], axis=1), (1, reps)),
            jnp.tile(jnp.concatenate([-sin, sin], axis=1), (1, reps)))


def kernel(x_prompt, x_sample, cache_kv_latent, cache_k_rope, state_pool, page_table, w_in, g_q_norm, g_kv_norm,
           w_q_up, w_k_up, w_v_up, w_pool_mix, pool_scale, w_branch_a, w_branch_b, w_out, ln1_g, ln1_b,
           w_router, router_bias, w_exp_gate, w_exp_up, w_exp_down, ln2_g, ln2_b):
    bp, tp, d = x_prompt.shape
    bs, ts, _ = x_sample.shape
    depth = w_in.shape[0]
    n_pages = page_table.shape[1]
    past = n_pages * cache_kv_latent.shape[2]
    alpha = (2 * depth) ** 0.25
    pw = pool_scale.shape[1]
    n_p, n_s = bp * tp, bs * ts
    s1 = Q_LORA_RANK + KV_LORA_RANK + QK_ROPE_DIM

    tm_p = min(512, tp)
    tq = min(512, tp)
    tk = min(512, tp)
    tm_mix_p = min(512, tp)
    tm_s = min(256, n_s)
    tm_moe = 256
    tm_gather = min(1024, n_s)
    n_all = n_p + n_s

    cos_p, sin_p = _rope_tables(jnp.arange(tp, dtype=jnp.int32), N_HEADS)
    cos_pt, sin_pt = cos_p.T, sin_p.T
    cos_p, sin_p = cos_p[:, :QK_ROPE_DIM], sin_p[:, :QK_ROPE_DIM]
    cos_s, sin_s = _rope_tables(past + jnp.arange(ts, dtype=jnp.int32), N_HEADS)
    cos_s, sin_s = jnp.tile(cos_s, (tm_s // ts, 1)), jnp.tile(sin_s, (tm_s // ts, 1))
    prefix = jnp.pad(state_pool, ((0, 0), (0, 0), (POOL_PAD - state_pool.shape[2], 0), (0, 0)))
    cache_rt = jnp.swapaxes(cache_k_rope, 2, 3)
    wr_t = w_router.T
    rb = router_bias.reshape(N_EXPERTS, 1)

    x_p, x_s, x_s_off = x_prompt.reshape(n_p, d), x_sample.reshape(n_s, d), 0
    outs = [[] for _ in range(6)]
    for l in range(depth):
        wq = w_q_up[l]
        wq_nope = jnp.transpose(wq[:, :, :QK_NOPE_DIM], (1, 0, 2))
        wqr = (wq[:, :, QK_NOPE_DIM:] * Q_SCALE).reshape(Q_LORA_RANK, N_HEADS * QK_ROPE_DIM).astype(BF16)
        wk = jnp.transpose(w_k_up[l], (1, 0, 2))
        wv = jnp.transpose(w_v_up[l], (1, 0, 2))
        wba = w_branch_a[l].reshape(N_HEADS, -1, d)
        wlat, wlat_t, wva = _fold_weights(wq_nope, wk, wv, wba)
        w1 = w_in[l, :, :s1].astype(BF16)
        w2 = w_in[l, :, s1:].astype(BF16)
        gq = g_q_norm[l].reshape(1, -1)
        gkv = g_kv_norm[l].reshape(1, -1)
        mix_w = (w2, wva, w_pool_mix[l].astype(BF16), pool_scale[l].reshape(1, pw), w_branch_b[l].astype(BF16),
                 w_out[l].astype(BF16), ln1_g[l].reshape(1, d), ln1_b[l].reshape(1, d), wr_t, rb)
        wg, wu, wd = w_exp_gate[l].astype(BF16), w_exp_up[l].astype(BF16), w_exp_down[l].astype(BF16)
        g2, b2 = ln2_g[l].reshape(1, d), ln2_b[l].reshape(1, d)

        qt, kvcat, ckv_t, ckv, krope = _proj_prompt(x_p, n_p, w1, wlat_t, wqr.T, gq, gkv, cos_p, sin_p, cos_pt,
                                                    sin_pt, tm=tm_p)
        o = _attn_prompt(qt, kvcat, ckv_t, batch=bp, seq=tp, tq=tq, tk=tk)
        x1, meta, tail = _mix_prompt(x_p, 0, o, mix_w, n_all=n_all, tm=tm_mix_p, alpha=alpha, seq=tp)
        outs[0].append(ckv.reshape(bp, tp, -1))
        outs[1].append(krope.reshape(bp, tp, -1))
        outs[2].append(tail[:, 1:, :])

        q, kvcat, ckv, krope = _proj_sample(x_s, x_s_off, n_s, w1, wlat, wqr, gq, gkv, cos_s, sin_s, tm=tm_s)
        o = _attn_sample(page_table, q, kvcat, cache_kv_latent, cache_rt, layer=l, t_new=ts)
        x1, meta, tail = _mix_sample(x_s, x_s_off, o, prefix[l], x1, meta, mix_w, row0=n_p, tm=tm_s, alpha=alpha,
                                     t_new=ts, past=past)
        outs[3].append(ckv.reshape(bs, ts, -1))
        outs[4].append(krope.reshape(bs, ts, -1))
        outs[5].append(tail[:, 1:, :])

        slot, token_of_slot, tile_ea, tile_eb, n_used, wts_tok = _routing_plan(meta, tm=tm_moe)
        xg = _sc_gather_rows(x1, token_of_slot)
        wts = _sc_gather_rows(wts_tok, token_of_slot)
        yg = _moe(xg, wts, tile_ea, tile_eb, n_used, wg, wu, wd, g2, b2, tm=tm_moe, alpha=alpha)
        x_all = _sc_gather_rows(yg, slot)
        x_p, x_s, x_s_off = x_all, x_all, n_p // tm_s

    return (x_all[:n_p].reshape(bp, tp, d), x_all[n_p:].reshape(bs, ts, d), jnp.stack(outs[0]),
            jnp.stack(outs[1]), jnp.stack(outs[2]), jnp.stack(outs[3]), jnp.stack(outs[4]), jnp.stack(outs[5]))
```

```python
import functools

import jax
import jax.numpy as jnp
from jax import lax
from jax.experimental import pallas as pl
from jax.experimental.pallas import tpu as pltpu
from jax.experimental.pallas import tpu_sc as plsc

F32 = jnp.float32
BF16 = jnp.bfloat16

N_HEADS = 8
QK_NOPE_DIM = 64
QK_ROPE_DIM = 32
Q_LORA_RANK = 256
KV_LORA_RANK = 128
QK_CAT = KV_LORA_RANK + QK_ROPE_DIM
ROPE_BASE = 10000.0
SM_SCALE = (QK_NOPE_DIM + QK_ROPE_DIM) ** -0.5
LOG2_E = 1.4426950408889634
Q_SCALE = SM_SCALE * LOG2_E
POOL_WINDOWS = (2, 4, 8, 16)
POOL_PAD = 16
N_EXPERTS = 16
N_EXPERT_GROUPS = 4
EXPERTS_PER_GROUP = 4
PAIRS_PER_GROUP = 6
PAIR_LO = (0, 0, 0, 1, 1, 2)
PAIR_HI = (1, 2, 3, 2, 3, 3)
N_BUCKETS = N_EXPERT_GROUPS * PAIRS_PER_GROUP
META_ROWS = 8
LN_EPS = 1e-5
RMS_EPS = 1e-6

LANES = 128
VMEM_LIMIT_BYTES = 56 * 1024 * 1024


def _params(*semantics):
    return pltpu.CompilerParams(dimension_semantics=semantics, vmem_limit_bytes=VMEM_LIMIT_BYTES)


def _const_spec(shape):
    zeros = (0,) * len(shape)
    return pl.BlockSpec(shape, lambda *_: zeros, pipeline_mode=pl.Buffered(1))


def _split_bf16(a):
    hi = a.astype(BF16)
    lo = (a - hi.astype(F32)).astype(BF16)
    return hi, lo


def _dot3(a, b, dims):
    a_hi, a_lo = _split_bf16(a)
    b_hi, b_lo = _split_bf16(b)
    d = functools.partial(lax.dot_general, dimension_numbers=dims, preferred_element_type=F32)
    return d(a_hi, b_hi) + (d(a_hi, b_lo) + d(a_lo, b_hi))


_NN = (((1,), (0,)), ((), ()))
_NT = (((1,), (1,)), ((), ()))


def _layer_norm(y, g, b):
    mu = jnp.mean(y, axis=-1, keepdims=True)
    d = y - mu
    var = jnp.mean(d * d, axis=-1, keepdims=True)
    return d * lax.rsqrt(var + LN_EPS) * g + b


def _rms_norm(y, g):
    return y * lax.rsqrt(jnp.mean(y * y, axis=-1, keepdims=True) + RMS_EPS) * g


def _fold_kernel(wqn_ref, wqn_t_ref, wk_ref, wk_t_ref, wv_ref, wba_ref, wlat_ref, wlat_t_ref, wva_ref):
    wlat_ref[...] = (_dot3(wqn_ref[0], wk_t_ref[0], _NN) * Q_SCALE).astype(BF16)
    wlat_t_ref[...] = (_dot3(wk_ref[0], wqn_t_ref[0], _NN) * Q_SCALE).astype(BF16)
    wva_ref[...] = _dot3(wv_ref[0], wba_ref[0], _NN).astype(BF16)


def _fold_weights(wq_nope, wk, wv, wba):
    h, r, dn = wq_nope.shape
    d = wba.shape[-1]
    head = lambda *blk: pl.BlockSpec((1,) + blk, lambda i: (i, 0, 0))
    return pl.pallas_call(
        _fold_kernel,
        grid=(h,),
        in_specs=[head(r, dn), head(dn, r), head(KV_LORA_RANK, dn), head(dn, KV_LORA_RANK),
                  head(KV_LORA_RANK, wv.shape[-1]), head(wba.shape[1], d)],
        out_specs=[pl.BlockSpec((r, KV_LORA_RANK), lambda i: (0, i)),
                   pl.BlockSpec((KV_LORA_RANK, r), lambda i: (i, 0)),
                   pl.BlockSpec((KV_LORA_RANK, d), lambda i: (i, 0))],
        out_shape=[jax.ShapeDtypeStruct((r, h * KV_LORA_RANK), BF16),
                   jax.ShapeDtypeStruct((h * KV_LORA_RANK, r), BF16),
                   jax.ShapeDtypeStruct((h * KV_LORA_RANK, d), BF16)],
        compiler_params=_params("arbitrary"),
        name="fold",
    )(wq_nope, jnp.swapaxes(wq_nope, 1, 2), wk, jnp.swapaxes(wk, 1, 2), wv, wba)


def _proj_common(x_ref, w1_ref, gq_ref, gkv_ref, cos_ref, sin_ref, ckv_ref, krope_ref):
    x = x_ref[...].astype(BF16)
    h = jnp.dot(x, w1_ref[...], preferred_element_type=F32)
    qa = h[:, :Q_LORA_RANK]
    kva = h[:, Q_LORA_RANK:Q_LORA_RANK + KV_LORA_RANK]
    kr = h[:, Q_LORA_RANK + KV_LORA_RANK:]
    qn = _rms_norm(qa, gq_ref[...])
    ckv = _rms_norm(kva, gkv_ref[...])
    half = QK_ROPE_DIM // 2
    kr_swapped = jnp.concatenate([kr[:, half:], kr[:, :half]], axis=1)
    krope = kr * cos_ref[:, :QK_ROPE_DIM] + kr_swapped * sin_ref[:, :QK_ROPE_DIM]
    ckv_ref[...] = ckv
    krope_ref[...] = krope
    return qn, ckv, krope


def _proj_sample_kernel(x_ref, w1_ref, wlat_ref, wqr_ref, gq_ref, gkv_ref, cos_ref, sin_ref,
                        q_ref, kvcat_ref, ckv_ref, krope_ref):
    qn, ckv, krope = _proj_common(x_ref, w1_ref, gq_ref, gkv_ref, cos_ref, sin_ref, ckv_ref, krope_ref)
    qn = qn.astype(BF16)
    ql = jnp.dot(qn, wlat_ref[...], preferred_element_type=F32)
    qr = jnp.dot(qn, wqr_ref[...], preferred_element_type=F32)
    half = QK_ROPE_DIM // 2
    width = qr.shape[1]
    lane = lax.broadcasted_iota(jnp.int32, qr.shape, 1)
    swapped = jnp.where((lane % QK_ROPE_DIM) < half,
                        pltpu.roll(qr, width - half, 1), pltpu.roll(qr, half, 1))
    qrr = qr * cos_ref[...] + swapped * sin_ref[...]
    for hh in range(N_HEADS):
        q_ref[hh, :, :KV_LORA_RANK] = ql[:, hh * KV_LORA_RANK:(hh + 1) * KV_LORA_RANK]
        q_ref[hh, :, KV_LORA_RANK:] = qrr[:, hh * QK_ROPE_DIM:(hh + 1) * QK_ROPE_DIM]
    kvcat_ref[:, :KV_LORA_RANK] = ckv
    kvcat_ref[:, KV_LORA_RANK:] = krope


def _proj_prompt_kernel(x_ref, w1_ref, wlat_t_ref, wqr_t_ref, gq_ref, gkv_ref, cos_ref, sin_ref,
                        cos_t_ref, sin_t_ref, qt_ref, kvcat_ref, ckv_t_ref, ckv_ref, krope_ref):
    qn, ckv, krope = _proj_common(x_ref, w1_ref, gq_ref, gkv_ref, cos_ref, sin_ref, ckv_ref, krope_ref)
    qn_t = qn.T.astype(BF16)
    ql_t = jnp.dot(wlat_t_ref[...], qn_t, preferred_element_type=F32)
    qr_t = jnp.dot(wqr_t_ref[...], qn_t, preferred_element_type=F32)
    half = QK_ROPE_DIM // 2
    pieces = []
    for hh in range(N_HEADS):
        base = hh * QK_ROPE_DIM
        pieces += [qr_t[base + half:base + QK_ROPE_DIM], qr_t[base:base + half]]
    qrr_t = qr_t * cos_t_ref[...] + jnp.concatenate(pieces, axis=0) * sin_t_ref[...]
    for hh in range(N_HEADS):
        qt_ref[hh, :KV_LORA_RANK, :] = ql_t[hh * KV_LORA_RANK:(hh + 1) * KV_LORA_RANK].astype(BF16)
        qt_ref[hh, KV_LORA_RANK:, :] = qrr_t[hh * QK_ROPE_DIM:(hh + 1) * QK_ROPE_DIM].astype(BF16)
    kvcat_ref[:, :KV_LORA_RANK] = ckv.astype(BF16)
    kvcat_ref[:, KV_LORA_RANK:] = krope.astype(BF16)
    ckv_t_ref[...] = ckv.T.astype(BF16)


def _proj_sample(x, x_off, n, w1, wlat, wqr, gq, gkv, cos_t, sin_t, *, tm):
    d = x.shape[1]
    row = lambda width: pl.BlockSpec((tm, width), lambda i: (i, 0))
    return pl.pallas_call(
        _proj_sample_kernel,
        grid=(n // tm,),
        in_specs=[pl.BlockSpec((tm, d), lambda i: (i + x_off, 0)), _const_spec(w1.shape), _const_spec(wlat.shape), _const_spec(wqr.shape),
                  _const_spec(gq.shape), _const_spec(gkv.shape),
                  _const_spec(cos_t.shape), _const_spec(sin_t.shape)],
        out_specs=[pl.BlockSpec((N_HEADS, tm, QK_CAT), lambda i: (0, i, 0)),
                   row(QK_CAT), row(KV_LORA_RANK), row(QK_ROPE_DIM)],
        out_shape=[jax.ShapeDtypeStruct((N_HEADS, n, QK_CAT), F32),
                   jax.ShapeDtypeStruct((n, QK_CAT), F32),
                   jax.ShapeDtypeStruct((n, KV_LORA_RANK), F32),
                   jax.ShapeDtypeStruct((n, QK_ROPE_DIM), F32)],
        compiler_params=_params("arbitrary"),
        name="proj_sample",
    )(x, w1, wlat, wqr, gq, gkv, cos_t, sin_t)


def _proj_prompt(x, n, w1, wlat_t, wqr_t, gq, gkv, cos, sin, cos_t, sin_t, *, tm):
    d = x.shape[1]
    tiles = cos.shape[0] // tm
    row = lambda width: pl.BlockSpec((tm, width), lambda i: (i, 0))
    return pl.pallas_call(
        _proj_prompt_kernel,
        grid=(n // tm,),
        in_specs=[row(d), _const_spec(w1.shape), _const_spec(wlat_t.shape), _const_spec(wqr_t.shape),
                  _const_spec(gq.shape), _const_spec(gkv.shape),
                  pl.BlockSpec((tm, cos.shape[1]), lambda i: (i % tiles, 0)),
                  pl.BlockSpec((tm, sin.shape[1]), lambda i: (i % tiles, 0)),
                  pl.BlockSpec((cos_t.shape[0], tm), lambda i: (0, i % tiles)),
                  pl.BlockSpec((sin_t.shape[0], tm), lambda i: (0, i % tiles))],
        out_specs=[pl.BlockSpec((N_HEADS, QK_CAT, tm), lambda i: (0, 0, i)),
                   row(QK_CAT),
                   pl.BlockSpec((KV_LORA_RANK, tm), lambda i: (0, i)),
                   row(KV_LORA_RANK), row(QK_ROPE_DIM)],
        out_shape=[jax.ShapeDtypeStruct((N_HEADS, QK_CAT, n), BF16),
                   jax.ShapeDtypeStruct((n, QK_CAT), BF16),
                   jax.ShapeDtypeStruct((KV_LORA_RANK, n), BF16),
                   jax.ShapeDtypeStruct((n, KV_LORA_RANK), F32),
                   jax.ShapeDtypeStruct((n, QK_ROPE_DIM), F32)],
        compiler_params=_params("arbitrary"),
        name="proj_prompt",
    )(x, w1, wlat_t, wqr_t, gq, gkv, cos, sin, cos_t, sin_t)


SCORE_AHEAD = 2
SCORE_SLOTS = 4


def _col_reduce(x, op, reduce_rows):
    rows = x.shape[0]
    while rows >= 32:
        x = x.reshape(4, rows // 4, x.shape[1])
        x = op(op(x[0], x[1]), op(x[2], x[3]))
        rows //= 4
    return reduce_rows(x, axis=0, keepdims=True)


def _attn_prompt_kernel(qt_ref, k_ref, vt_ref, o_ref, m_ref, l_ref, acc_ref, s_ref, *, tq, tk):
    i = pl.program_id(1)
    n_full = (i * tq) // tk
    m_ref[...] = jnp.full(m_ref.shape, -jnp.inf, F32)
    l_ref[...] = jnp.zeros(l_ref.shape, F32)
    acc_ref[...] = jnp.zeros(acc_ref.shape, F32)

    def scores(j, hh):
        k = k_ref[pl.ds(pl.multiple_of(j * tk, tk), tk), :]
        s_ref[hh % SCORE_SLOTS] = jnp.dot(k, qt_ref[hh], preferred_element_type=F32)

    def step(j, masked):
        vt = vt_ref[:, pl.ds(pl.multiple_of(j * tk, tk), tk)]
        if masked:
            key = j * tk + lax.broadcasted_iota(jnp.int32, (tk, tq), 0)
            tok = i * tq + lax.broadcasted_iota(jnp.int32, (tk, tq), 1)
            visible = key <= tok
        for hh in range(N_HEADS):
            ahead = hh + SCORE_AHEAD
            if ahead < N_HEADS:
                scores(j, ahead)
            elif not masked:
                scores(j + 1, ahead - N_HEADS)
            s = s_ref[hh % SCORE_SLOTS]
            if masked:
                s = jnp.where(visible, s, -jnp.inf)
            m_prev = m_ref[hh]
            m_new = jnp.maximum(m_prev, _col_reduce(s, jnp.maximum, jnp.max))
            alpha = jnp.exp2(m_prev - m_new)
            p = jnp.exp2(s - m_new)
            l_ref[hh] = alpha * l_ref[hh] + _col_reduce(p, jnp.add, jnp.sum)
            pv = jnp.dot(vt, p.astype(BF16), preferred_element_type=F32)
            acc_ref[hh] = alpha * acc_ref[hh] + pv
            m_ref[hh] = m_new

    def body(j, carry):
        step(j, False)
        return carry

    for hh in range(SCORE_AHEAD):
        scores(0, hh)
    lax.fori_loop(0, n_full, body, 0)
    step(n_full, True)
    for hh in range(N_HEADS):
        o_t = acc_ref[hh] / l_ref[hh]
        o_ref[:, hh * KV_LORA_RANK:(hh + 1) * KV_LORA_RANK] = o_t.T.astype(o_ref.dtype)


def _attn_prompt(qt, kvcat, ckv_t, *, batch, seq, tq, tk):
    nq = seq // tq
    n = batch * seq
    return pl.pallas_call(
        functools.partial(_attn_prompt_kernel, tq=tq, tk=tk),
        grid=(batch, nq),
        in_specs=[pl.BlockSpec((N_HEADS, QK_CAT, tq), lambda b, i: (0, 0, b * nq + i)),
                  pl.BlockSpec((seq, QK_CAT), lambda b, i: (b, 0)),
                  pl.BlockSpec((KV_LORA_RANK, seq), lambda b, i: (0, b))],
        out_specs=pl.BlockSpec((tq, N_HEADS * KV_LORA_RANK), lambda b, i: (b * nq + i, 0)),
        out_shape=jax.ShapeDtypeStruct((n, N_HEADS * KV_LORA_RANK), BF16),
        scratch_shapes=[pltpu.VMEM((N_HEADS, 1, tq), F32), pltpu.VMEM((N_HEADS, 1, tq), F32),
                        pltpu.VMEM((N_HEADS, KV_LORA_RANK, tq), F32), pltpu.VMEM((SCORE_SLOTS, tk, tq), F32)],
        compiler_params=_params("arbitrary", "arbitrary"),
        name="attn_prompt",
    )(qt, kvcat, ckv_t)


def _attn_sample_kernel(pt_ref, q_ref, knew_ref, cache_c_ref, cache_rt_ref, o_ref, cbuf, rbuf, sem,
                        *, layer, n_pages, page, t_new):
    b = pl.program_id(0)
    nb = pl.num_programs(0)

    def page_copies(seq_idx, slot):
        copies = []
        for p in range(n_pages):
            pid = pt_ref[seq_idx, p]
            copies.append(pltpu.make_async_copy(
                cache_c_ref.at[layer, pid], cbuf.at[slot, pl.ds(p * page, page), :], sem.at[slot, 0]))
            copies.append(pltpu.make_async_copy(
                cache_rt_ref.at[layer, pid], rbuf.at[slot, :, pl.ds(p * page, page)], sem.at[slot, 1]))
        return copies

    slot = b % 2

    @pl.when(b == 0)
    def _():
        for c in page_copies(0, 0):
            c.start()

    @pl.when(b + 1 < nb)
    def _():
        for c in page_copies(b + 1, 1 - slot):
            c.start()

    for c in page_copies(b, slot):
        c.wait()

    rows = N_HEADS * t_new
    q = q_ref[...].reshape(rows, QK_CAT).astype(BF16)
    kc = cbuf[slot].astype(BF16)
    krt = rbuf[slot].astype(BF16)
    kn = knew_ref[...].astype(BF16)
    s = (lax.dot_general(q[:, :KV_LORA_RANK], kc, _NT, preferred_element_type=F32)
         + jnp.dot(q[:, KV_LORA_RANK:], krt, preferred_element_type=F32))
    sn = lax.dot_general(q, kn, _NT, preferred_element_type=F32)
    tok = lax.broadcasted_iota(jnp.int32, (N_HEADS, t_new, t_new), 1).reshape(rows, t_new)
    key = lax.broadcasted_iota(jnp.int32, (rows, t_new), 1)
    sn = jnp.where(key <= tok, sn, -jnp.inf)
    m = jnp.maximum(jnp.max(s, axis=-1, keepdims=True), jnp.max(sn, axis=-1, keepdims=True))
    p = jnp.exp2(s - m)
    pn = jnp.exp2(sn - m)
    l = jnp.sum(p, axis=-1, keepdims=True) + jnp.sum(pn, axis=-1, keepdims=True)
    o = (jnp.dot(p.astype(BF16), kc, preferred_element_type=F32)
         + jnp.dot(pn.astype(BF16), kn[:, :KV_LORA_RANK], preferred_element_type=F32)) / l
    for hh in range(N_HEADS):
        o_ref[:, hh * KV_LORA_RANK:(hh + 1) * KV_LORA_RANK] = o[hh * t_new:(hh + 1) * t_new]


def _attn_sample(page_table, q, knew, cache_c, cache_rt, *, layer, t_new):
    n_seq, n_pages = page_table.shape
    page = cache_c.shape[2]
    past = n_pages * page
    grid_spec = pltpu.PrefetchScalarGridSpec(
        num_scalar_prefetch=1,
        grid=(n_seq,),
        in_specs=[pl.BlockSpec((N_HEADS, t_new, QK_CAT), lambda b, pt: (0, b, 0)),
                  pl.BlockSpec((t_new, QK_CAT), lambda b, pt: (b, 0)),
                  pl.BlockSpec(memory_space=pl.ANY),
                  pl.BlockSpec(memory_space=pl.ANY)],
        out_specs=pl.BlockSpec((t_new, N_HEADS * KV_LORA_RANK), lambda b, pt: (b, 0)),
        scratch_shapes=[pltpu.VMEM((2, past, KV_LORA_RANK), F32), pltpu.VMEM((2, QK_ROPE_DIM, past), F32),
                        pltpu.SemaphoreType.DMA((2, 2))],
    )
    return pl.pallas_call(
        functools.partial(_attn_sample_kernel, layer=layer, n_pages=n_pages, page=page, t_new=t_new),
        grid_spec=grid_spec,
        out_shape=jax.ShapeDtypeStruct((n_seq * t_new, N_HEADS * KV_LORA_RANK), F32),
        compiler_params=_params("arbitrary"),
        name="attn_sample",
    )(page_table, q, knew, cache_c, cache_rt)


def _router_meta(x1, wr, rb):
    logits = _dot3(wr, x1, _NT)
    score = jax.nn.sigmoid(logits)
    biased = score + rb
    row = lambda a, e: a[e:e + 1, :]
    group_score = []
    for g in range(N_EXPERT_GROUPS):
        v = [row(biased, g * EXPERTS_PER_GROUP + k) for k in range(EXPERTS_PER_GROUP)]
        best = None
        for a in range(EXPERTS_PER_GROUP):
            for c in range(a + 1, EXPERTS_PER_GROUP):
                pair = v[a] + v[c]
                best = pair if best is None else jnp.maximum(best, pair)
        group_score.append(best)
    top = group_score[0]
    grp = jnp.zeros(top.shape, jnp.int32)
    for g in range(1, N_EXPERT_GROUPS):
        better = group_score[g] > top
        grp = jnp.where(better, g, grp)
        top = jnp.maximum(top, group_score[g])

    def pick(a, k):
        out = row(a, (N_EXPERT_GROUPS - 1) * EXPERTS_PER_GROUP + k)
        for g in range(N_EXPERT_GROUPS - 2, -1, -1):
            out = jnp.where(grp == g, row(a, g * EXPERTS_PER_GROUP + k), out)
        return out

    cand = [pick(biased, k) for k in range(EXPERTS_PER_GROUP)]
    aff = [pick(score, k) for k in range(EXPERTS_PER_GROUP)]
    first = jnp.zeros(top.shape, jnp.int32)
    best = cand[0]
    for k in range(1, EXPERTS_PER_GROUP):
        better = cand[k] > best
        first = jnp.where(better, k, first)
        best = jnp.maximum(best, cand[k])
    second = jnp.full(top.shape, -1, jnp.int32)
    best2 = jnp.full(top.shape, -jnp.inf, F32)
    for k in range(EXPERTS_PER_GROUP):
        better = (first != k) & ((second < 0) | (cand[k] > best2))
        second = jnp.where(better, k, second)
        best2 = jnp.where(better, cand[k], best2)
    lo = jnp.minimum(first, second)
    hi = jnp.maximum(first, second)

    def take(vals, k_idx):
        out = vals[EXPERTS_PER_GROUP - 1]
        for k in range(EXPERTS_PER_GROUP - 2, -1, -1):
            out = jnp.where(k_idx == k, vals[k], out)
        return out

    aff_lo, aff_hi = take(aff, lo), take(aff, hi)
    denom = aff_lo + aff_hi
    pair = jnp.where(lo == 0, 0, jnp.where(lo == 1, 3, 5)) + (hi - lo - 1)
    bucket = (grp * PAIRS_PER_GROUP + pair).astype(F32)
    pad = jnp.zeros((META_ROWS - 3,) + top.shape[1:], F32)
    return jnp.concatenate([bucket, aff_lo / denom, aff_hi / denom, pad], axis=0)


def _mix_tail(x, o, pooled, gate_a, gate_b, wva_ref, wmix_ref, ps_ref, wbb_ref, wout_ref, g_ref, b_ref,
              wr_ref, rb_ref, x1_ref, gate_ref, *, alpha):
    gd = wmix_ref.shape[1]
    yb = jnp.concatenate(
        [jnp.dot(pooled[:, g * gd:(g + 1) * gd].astype(BF16), wmix_ref[g], preferred_element_type=F32)
         for g in range(len(POOL_WINDOWS))], axis=1) * ps_ref[...]
    ya = jnp.dot(o.astype(BF16), wva_ref[...], preferred_element_type=F32)
    yb = jnp.dot(yb.astype(BF16), wbb_ref[...], preferred_element_type=F32)
    merged = jax.nn.sigmoid(gate_a) * ya + jax.nn.sigmoid(gate_b) * yb
    mix = jnp.dot(merged.astype(BF16), wout_ref[...], preferred_element_type=F32)
    x1 = _layer_norm(alpha * x + mix, g_ref[...], b_ref[...])
    x1_ref[...] = x1
    gate_ref[...] = _router_meta(x1, wr_ref[...], rb_ref[...])


def _mix_prompt_kernel(*refs, alpha, tiles_per_seq, n_tiles):
    x1_ref, gate_ref = refs[-4], refs[-3]

    @pl.when(pl.program_id(0) < n_tiles)
    def _():
        _mix_prompt_tile(*refs, alpha=alpha, tiles_per_seq=tiles_per_seq)

    @pl.when(pl.program_id(0) >= n_tiles)
    def _():
        x1_ref[...] = jnp.zeros(x1_ref.shape, F32)
        gate_ref[...] = jnp.zeros(gate_ref.shape, F32)


def _mix_prompt_tile(x_ref, o_ref, w2_ref, wva_ref, wmix_ref, ps_ref, wbb_ref, wout_ref, g_ref, b_ref,
                     wr_ref, rb_ref, x1_ref, gate_ref, tail_ref, ext_ref, *, alpha, tiles_per_seq):
    tm = x_ref.shape[0]
    pw = ps_ref.shape[1]
    gd = pw // len(POOL_WINDOWS)
    x = x_ref[...]
    h2 = jnp.dot(x.astype(BF16), w2_ref[...], preferred_element_type=F32)
    u = h2[:, :pw]
    t = pl.program_id(0) % tiles_per_seq

    @pl.when(t == 0)
    def _():
        ext_ref[0:POOL_PAD, :] = jnp.zeros((POOL_PAD, pw), F32)

    @pl.when(t != 0)
    def _():
        ext_ref[0:POOL_PAD, :] = ext_ref[tm:tm + POOL_PAD, :]

    ext_ref[POOL_PAD:POOL_PAD + tm, :] = u
    tail_ref[0] = u[tm - POOL_PAD:, :]
    pos = t * tm + lax.broadcasted_iota(jnp.int32, (tm, 1), 0)
    pooled = []
    for g, w in enumerate(POOL_WINDOWS):
        acc = ext_ref[POOL_PAD:POOL_PAD + tm, g * gd:(g + 1) * gd]
        for j in range(1, w):
            acc = acc + ext_ref[POOL_PAD - j:POOL_PAD - j + tm, g * gd:(g + 1) * gd]
        cnt = jnp.minimum(w, pos + 1).astype(F32)
        pooled.append(acc / cnt - u[:, g * gd:(g + 1) * gd])
    pooled = jnp.concatenate(pooled, axis=1)
    d = x.shape[1]
    _mix_tail(x, o_ref[...], pooled, h2[:, pw:pw + d], h2[:, pw + d:], wva_ref, wmix_ref, ps_ref, wbb_ref,
              wout_ref, g_ref, b_ref, wr_ref, rb_ref, x1_ref, gate_ref, alpha=alpha)


def _mix_sample_kernel(x_ref, o_ref, pre_ref, x1_all_ref, meta_all_ref, w2_ref, wva_ref, wmix_ref, ps_ref, wbb_ref,
                       wout_ref, g_ref, b_ref, wr_ref, rb_ref, x1_ref, gate_ref, tail_ref, ext_ref,
                       *, alpha, t_new, past):
    del x1_all_ref, meta_all_ref
    tm = x_ref.shape[0]
    sb = tm // t_new
    pw = ps_ref.shape[1]
    gd = pw // len(POOL_WINDOWS)
    x = x_ref[...]
    h2 = jnp.dot(x.astype(BF16), w2_ref[...], preferred_element_type=F32)
    u = h2[:, :pw]
    ext_ref[:, 0:POOL_PAD, :] = pre_ref[...]
    ext_ref[:, POOL_PAD:POOL_PAD + t_new, :] = u.reshape(sb, t_new, pw)
    tail_ref[...] = ext_ref[:, t_new:t_new + POOL_PAD, :]
    pos = past + lax.broadcasted_iota(jnp.int32, (sb, t_new, 1), 1)
    pooled = []
    for g, w in enumerate(POOL_WINDOWS):
        acc = ext_ref[:, POOL_PAD:POOL_PAD + t_new, g * gd:(g + 1) * gd]
        for j in range(1, w):
            acc = acc + ext_ref[:, POOL_PAD - j:POOL_PAD - j + t_new, g * gd:(g + 1) * gd]
        cnt = jnp.minimum(w, pos + 1).astype(F32)
        pooled.append((acc / cnt).reshape(tm, gd) - u[:, g * gd:(g + 1) * gd])
    pooled = jnp.concatenate(pooled, axis=1)
    d = x.shape[1]
    _mix_tail(x, o_ref[...], pooled, h2[:, pw:pw + d], h2[:, pw + d:], wva_ref, wmix_ref, ps_ref, wbb_ref,
              wout_ref, g_ref, b_ref, wr_ref, rb_ref, x1_ref, gate_ref, alpha=alpha)


def _mix_prompt(x, x_off, o, weights, *, n_all, tm, alpha, seq):
    n = o.shape[0]
    d = x.shape[1]
    pw = weights[3].shape[1]
    tiles_per_seq = seq // tm
    n_tiles = n // tm
    last = n_tiles - 1
    return pl.pallas_call(
        functools.partial(_mix_prompt_kernel, alpha=alpha, tiles_per_seq=tiles_per_seq, n_tiles=n_tiles),
        grid=(n_all // tm,),
        in_specs=[pl.BlockSpec((tm, d), lambda i: (jnp.minimum(i, last) + x_off, 0)),
                  pl.BlockSpec((tm, o.shape[1]), lambda i: (jnp.minimum(i, last), 0))]
        + [_const_spec(a.shape) for a in weights],
        out_specs=[pl.BlockSpec((tm, d), lambda i: (i, 0)),
                   pl.BlockSpec((META_ROWS, tm), lambda i: (0, i)),
                   pl.BlockSpec((1, POOL_PAD, pw), lambda i: (jnp.minimum(i, last) // tiles_per_seq, 0, 0))],
        out_shape=[jax.ShapeDtypeStruct((n_all, d), F32), jax.ShapeDtypeStruct((META_ROWS, n_all), F32),
                   jax.ShapeDtypeStruct((n // seq, POOL_PAD, pw), F32)],
        scratch_shapes=[pltpu.VMEM((tm + POOL_PAD, pw), F32)],
        compiler_params=_params("arbitrary"),
        name="mix_prompt",
    )(x, o, *weights)


def _mix_sample(x, x_off, o, prefix, x1_all, meta_all, weights, *, row0, tm, alpha, t_new, past):
    n, d = o.shape
    pw = weights[3].shape[1]
    sb = tm // t_new
    off = row0 // tm
    any_spec = pl.BlockSpec(memory_space=pl.ANY)
    return pl.pallas_call(
        functools.partial(_mix_sample_kernel, alpha=alpha, t_new=t_new, past=past),
        grid=(n // tm,),
        in_specs=[pl.BlockSpec((tm, d), lambda i: (i + x_off, 0)), pl.BlockSpec((tm, d), lambda i: (i, 0)),
                  pl.BlockSpec((sb, POOL_PAD, pw), lambda i: (i, 0, 0)), any_spec, any_spec]
        + [_const_spec(a.shape) for a in weights],
        out_specs=[pl.BlockSpec((tm, d), lambda i: (i + off, 0)),
                   pl.BlockSpec((META_ROWS, tm), lambda i: (0, i + off)),
                   pl.BlockSpec((sb, POOL_PAD, pw), lambda i: (i, 0, 0))],
        out_shape=[jax.ShapeDtypeStruct(x1_all.shape, F32), jax.ShapeDtypeStruct(meta_all.shape, F32),
                   jax.ShapeDtypeStruct((n // t_new, POOL_PAD, pw), F32)],
        input_output_aliases={3: 0, 4: 1},
        scratch_shapes=[pltpu.VMEM((sb, POOL_PAD + t_new, pw), F32)],
        compiler_params=_params("arbitrary"),
        name="mix_sample",
    )(x, o, prefix, x1_all, meta_all, *weights)


GATHER_UNROLL = 8


def _gather_rows_kernel(idx_ref, tbl_ref, out_ref):
    tm = out_ref.shape[0]
    base = pl.program_id(1) * tm

    def body(g, carry):
        for u in range(GATHER_UNROLL):
            r = g * GATHER_UNROLL + u
            out_ref[pl.ds(r, 1), :] = tbl_ref[pl.ds(idx_ref[base + r], 1), :]
        return carry

    lax.fori_loop(0, tm // GATHER_UNROLL, body, 0)


def _gather_rows(table, idx, *, tm, cw):
    r, d = table.shape
    m = idx.shape[0]
    grid_spec = pltpu.PrefetchScalarGridSpec(
        num_scalar_prefetch=1,
        grid=(d // cw, m // tm),
        in_specs=[pl.BlockSpec((r, cw), lambda c, i, idx_ref: (0, c), pipeline_mode=pl.Buffered(1))],
        out_specs=pl.BlockSpec((tm, cw), lambda c, i, idx_ref: (i, c)),
    )
    return pl.pallas_call(
        _gather_rows_kernel,
        grid_spec=grid_spec,
        out_shape=jax.ShapeDtypeStruct((m, d), table.dtype),
        compiler_params=_params("arbitrary", "arbitrary"),
        name="gather_rows",
    )(idx, table)


SC_WORKERS = 32
SC_CHUNK = 16
SC_BUFFERS = 4


def _sc_gather_rows(table, idx):
    m = idx.shape[0]
    d = table.shape[1]
    per_worker = m // SC_WORKERS
    n_chunks = per_worker // SC_CHUNK
    assert per_worker * SC_WORKERS == m and n_chunks * SC_CHUNK == per_worker, (m, SC_WORKERS, SC_CHUNK)
    mesh = plsc.VectorSubcoreMesh(core_axis_name="c", subcore_axis_name="s")

    @functools.partial(
        pl.kernel, mesh=mesh, out_type=jax.ShapeDtypeStruct((m, d), table.dtype),
        scratch_types=[pltpu.VMEM((per_worker,), jnp.int32), pltpu.VMEM((SC_BUFFERS, SC_CHUNK, d), table.dtype),
                       pltpu.SemaphoreType.DMA((SC_BUFFERS,)), pltpu.SemaphoreType.DMA((SC_BUFFERS,))],
        name="sc_gather_rows")
    def gather(table_hbm, idx_hbm, out_hbm, idx_v, rows_v, gather_sem, write_sem):
        worker = lax.axis_index("s") * mesh.num_cores + lax.axis_index("c")
        base = worker * per_worker
        pltpu.sync_copy(idx_hbm.at[pl.ds(pl.multiple_of(base, SC_CHUNK), per_worker)], idx_v)

        def fetch(c, b):
            rows = idx_v.at[pl.ds(pl.multiple_of(c * SC_CHUNK, SC_CHUNK), SC_CHUNK)]
            return pltpu.make_async_copy(table_hbm.at[rows], rows_v.at[b], gather_sem.at[b])

        def write(c, b):
            dst = out_hbm.at[pl.ds(pl.multiple_of(base + c * SC_CHUNK, SC_CHUNK), SC_CHUNK)]
            return pltpu.make_async_copy(rows_v.at[b], dst, write_sem.at[b])

        for b in range(min(SC_BUFFERS, n_chunks)):
            fetch(b, b).start()

        @pl.loop(0, pl.cdiv(n_chunks, SC_BUFFERS))
        def _(p):
            for b in range(SC_BUFFERS):
                c = p * SC_BUFFERS + b

                @pl.when(c < n_chunks)
                def _():
                    fetch(c, b).wait()
                    write(c, b).start()

                    @pl.when(c + SC_BUFFERS < n_chunks)
                    def _():
                        write(c, b).wait()
                        fetch(c + SC_BUFFERS, b).start()

        for b in range(min(SC_BUFFERS, n_chunks)):
            last = ((n_chunks - 1 - b) // SC_BUFFERS) * SC_BUFFERS + b
            write(last, b).wait()

    return gather(table, idx)


def _moe_kernel(src_ref, dst_ref, ea_ref, eb_ref, used_ref, x1_hbm, wt_ref, wga_ref, wua_ref, wda_ref, wgb_ref,
                wub_ref, wdb_ref, g_ref, b_ref, out_hbm, xbuf, ybuf, gather_sem, scatter_sem, *, alpha, tm):
    del ea_ref, eb_ref
    i = pl.program_id(0)
    used = used_ref[0]
    buf = i % 2

    def gather_row(tile, b, r):
        src = x1_hbm.at[pl.ds(src_ref[tile * tm + r], 1), :]
        return pltpu.make_async_copy(src, xbuf.at[b, pl.ds(r, 1), :], gather_sem.at[b])

    def scatter_row(tile, b, r):
        dst = out_hbm.at[pl.ds(dst_ref[tile * tm + r], 1), :]
        return pltpu.make_async_copy(ybuf.at[b, pl.ds(r, 1), :], dst, scatter_sem.at[b])

    def start_gather(tile, b):
        for r in range(tm):
            gather_row(tile, b, r).start()

    def wait_gather(b):
        for r in range(tm):
            gather_row(0, b, 0).wait()

    def wait_scatter(b):
        for r in range(tm):
            scatter_row(0, b, 0).wait()

    @pl.when(i == 0)
    def _():
        start_gather(0, 0)
        n_rows, n_out = x1_hbm.shape[0], out_hbm.shape[0]
        ybuf[1] = jnp.zeros(ybuf.shape[1:], F32)
        fills = [pltpu.make_async_copy(ybuf.at[1], out_hbm.at[pl.ds(row, tm), :], scatter_sem.at[1])
                 for row in range(n_rows, n_out, tm)]
        for c in fills:
            c.start()
        for c in fills:
            c.wait()

    @pl.when(i < used)
    def _():
        @pl.when(i + 1 < used)
        def _():
            start_gather(i + 1, 1 - buf)

        wait_gather(buf)

        @pl.when(i >= 2)
        def _():
            wait_scatter(buf)

        x = xbuf[buf]
        xb = x.astype(BF16)
        wt = wt_ref[...]

        def expert(wg_ref, wu_ref, wd_ref, col):
            hg = jnp.dot(xb, wg_ref[0], preferred_element_type=F32)
            hu = jnp.dot(xb, wu_ref[0], preferred_element_type=F32)
            hid = (jax.nn.silu(hg) * hu * col).astype(BF16)
            return jnp.dot(hid, wd_ref[0], preferred_element_type=F32)

        y = expert(wga_ref, wua_ref, wda_ref, wt[:, 0:1]) + expert(wgb_ref, wub_ref, wdb_ref, wt[:, 1:2])
        ybuf[buf] = _layer_norm(alpha * x + y, g_ref[...], b_ref[...])
        for r in range(tm):
            scatter_row(i, buf, r).start()

    @pl.when(i == pl.num_programs(0) - 1)
    def _():
        wait_scatter((used - 1) % 2)

        @pl.when(used >= 2)
        def _():
            wait_scatter(used % 2)


def _moe(x1, wts, row_src, row_dst, tile_ea, tile_eb, n_used, wg, wu, wd, g2, b2, *, tm, alpha):
    p = row_src.shape[0]
    d = x1.shape[1]
    f = wg.shape[2]
    tile = lambda i, *pre: (jnp.minimum(i, pre[-1][0] - 1), 0)
    first = lambda i, src, dst, ea, eb, used: (ea[i], 0, 0)
    second = lambda i, src, dst, ea, eb, used: (eb[i], 0, 0)
    const = lambda shape: pl.BlockSpec(shape, lambda i, *pre: (0,) * len(shape), pipeline_mode=pl.Buffered(1))
    grid_spec = pltpu.PrefetchScalarGridSpec(
        num_scalar_prefetch=5,
        grid=(p // tm,),
        in_specs=[pl.BlockSpec(memory_space=pl.ANY), pl.BlockSpec((tm, wts.shape[1]), tile),
                  pl.BlockSpec((1, d, f), first), pl.BlockSpec((1, d, f), first), pl.BlockSpec((1, f, d), first),
                  pl.BlockSpec((1, d, f), second), pl.BlockSpec((1, d, f), second), pl.BlockSpec((1, f, d), second),
                  const(g2.shape), const(b2.shape)],
        out_specs=pl.BlockSpec(memory_space=pl.ANY),
        scratch_shapes=[pltpu.VMEM((2, tm, d), F32), pltpu.VMEM((2, tm, d), F32),
                        pltpu.SemaphoreType.DMA((2,)), pltpu.SemaphoreType.DMA((2,))],
    )
    return pl.pallas_call(
        functools.partial(_moe_kernel, alpha=alpha, tm=tm),
        grid_spec=grid_spec,
        out_shape=jax.ShapeDtypeStruct((p, d), F32),
        compiler_params=_params("arbitrary"),
        name="moe",
    )(row_src, row_dst, tile_ea, tile_eb, n_used, x1, wts, wg, wu, wd, wg, wu, wd, g2, b2)


def _routing_plan(meta, *, tm):
    n = meta.shape[1]
    n_tiles = (n + N_BUCKETS * (tm - 1) + tm - 1) // tm
    p = n_tiles * tm
    bucket = meta[0].astype(jnp.int32)
    onehot = (bucket[:, None] == jnp.arange(N_BUCKETS, dtype=jnp.int32)[None, :]).astype(jnp.int32)
    csum = jnp.cumsum(onehot, axis=0)
    rank = jnp.sum(csum * onehot, axis=1) - 1
    tiles_per_bucket = (csum[-1] + tm - 1) // tm
    tile_end = jnp.cumsum(tiles_per_bucket)
    n_used = tile_end[-1:]
    row_start = (tile_end - tiles_per_bucket) * tm
    slot = jnp.sum(onehot * row_start[None, :], axis=1) + rank
    token = jnp.full((p,), -1, jnp.int32).at[slot].set(jnp.arange(n, dtype=jnp.int32), unique_indices=True)
    is_pad = token < 0
    row_src = jnp.where(is_pad, 0, token)
    row_dst = jnp.where(is_pad, n - 1 + jnp.cumsum(is_pad.astype(jnp.int32)), token)
    tile_id = jnp.minimum(jnp.arange(n_tiles, dtype=jnp.int32), n_used - 1)
    tile_bucket = jnp.minimum(jnp.sum((tile_id[:, None] >= tile_end[None, :]).astype(jnp.int32), axis=1),
                              N_BUCKETS - 1)
    group, pair = tile_bucket // PAIRS_PER_GROUP, tile_bucket % PAIRS_PER_GROUP
    lo = sum((pair == k).astype(jnp.int32) * PAIR_LO[k] for k in range(PAIRS_PER_GROUP))
    hi = sum((pair == k).astype(jnp.int32) * PAIR_HI[k] for k in range(PAIRS_PER_GROUP))
    tile_ea = group * EXPERTS_PER_GROUP + lo
    tile_eb = group * EXPERTS_PER_GROUP + hi
    weights = jnp.pad(meta[1:3].T[row_src], ((0, 0), (0, LANES - 2)))
    return row_src, row_dst, tile_ea, tile_eb, n_used.astype(jnp.int32), weights


def _rope_tables(pos, reps):
    half = QK_ROPE_DIM // 2
    inv = ROPE_BASE ** (-jnp.arange(half, dtype=F32) / half)
    ang = pos.astype(F32)[:, None] * inv
    cos, sin = jnp.cos(ang), jnp.sin(ang)
    return (jnp.tile(jnp.concatenate([cos, cos], axis=1), (1, reps)),
            jnp.tile(jnp.concatenate([-sin, sin], axis=1), (1, reps)))


def kernel(x_prompt, x_sample, cache_kv_latent, cache_k_rope, state_pool, page_table, w_in, g_q_norm, g_kv_norm,
           w_q_up, w_k_up, w_v_up, w_pool_mix, pool_scale, w_branch_a, w_branch_b, w_out, ln1_g, ln1_b,
           w_router, router_bias, w_exp_gate, w_exp_up, w_exp_down, ln2_g, ln2_b):
    bp, tp, d = x_prompt.shape
    bs, ts, _ = x_sample.shape
    depth = w_in.shape[0]
    n_pages = page_table.shape[1]
    past = n_pages * cache_kv_latent.shape[2]
    alpha = (2 * depth) ** 0.25
    pw = pool_scale.shape[1]
    n_p, n_s = bp * tp, bs * ts
    s1 = Q_LORA_RANK + KV_LORA_RANK + QK_ROPE_DIM

    tm_p = min(512, tp)
    tq = min(512, tp)
    tk = min(512, tp)
    tm_mix_p = min(512, tp)
    tm_s = min(256, n_s)
    tm_moe = 256
    tm_gather = min(1024, n_s)
    n_all = n_p + n_s

    cos_p, sin_p = _rope_tables(jnp.arange(tp, dtype=jnp.int32), N_HEADS)
    cos_pt, sin_pt = cos_p.T, sin_p.T
    cos_p, sin_p = cos_p[:, :QK_ROPE_DIM], sin_p[:, :QK_ROPE_DIM]
    cos_s, sin_s = _rope_tables(past + jnp.arange(ts, dtype=jnp.int32), N_HEADS)
    cos_s, sin_s = jnp.tile(cos_s, (tm_s // ts, 1)), jnp.tile(sin_s, (tm_s // ts, 1))
    prefix = jnp.pad(state_pool, ((0, 0), (0, 0), (POOL_PAD - state_pool.shape[2], 0), (0, 0)))
    cache_rt = jnp.swapaxes(cache_k_rope, 2, 3)
    wr_t = w_router.T
    rb = router_bias.reshape(N_EXPERTS, 1)

    x_p, x_s, x_s_off = x_prompt.reshape(n_p, d), x_sample.reshape(n_s, d), 0
    outs = [[] for _ in range(6)]
    for l in range(depth):
        wq = w_q_up[l]
        wq_nope = jnp.transpose(wq[:, :, :QK_NOPE_DIM], (1, 0, 2))
        wqr = (wq[:, :, QK_NOPE_DIM:] * Q_SCALE).reshape(Q_LORA_RANK, N_HEADS * QK_ROPE_DIM).astype(BF16)
        wk = jnp.transpose(w_k_up[l], (1, 0, 2))
        wv = jnp.transpose(w_v_up[l], (1, 0, 2))
        wba = w_branch_a[l].reshape(N_HEADS, -1, d)
        wlat, wlat_t, wva = _fold_weights(wq_nope, wk, wv, wba)
        w1 = w_in[l, :, :s1].astype(BF16)
        w2 = w_in[l, :, s1:].astype(BF16)
        gq = g_q_norm[l].reshape(1, -1)
        gkv = g_kv_norm[l].reshape(1, -1)
        mix_w = (w2, wva, w_pool_mix[l].astype(BF16), pool_scale[l].reshape(1, pw), w_branch_b[l].astype(BF16),
                 w_out[l].astype(BF16), ln1_g[l].reshape(1, d), ln1_b[l].reshape(1, d), wr_t, rb)
        wg, wu, wd = w_exp_gate[l].astype(BF16), w_exp_up[l].astype(BF16), w_exp_down[l].astype(BF16)
        g2, b2 = ln2_g[l].reshape(1, d), ln2_b[l].reshape(1, d)

        qt, kvcat, ckv_t, ckv, krope = _proj_prompt(x_p, n_p, w1, wlat_t, wqr.T, gq, gkv, cos_p, sin_p, cos_pt,
                                                    sin_pt, tm=tm_p)
        o = _attn_prompt(qt, kvcat, ckv_t, batch=bp, seq=tp, tq=tq, tk=tk)
        x1, meta, tail = _mix_prompt(x_p, 0, o, mix_w, n_all=n_all, tm=tm_mix_p, alpha=alpha, seq=tp)
        outs[0].append(ckv.reshape(bp, tp, -1))
        outs[1].append(krope.reshape(bp, tp, -1))
        outs[2].append(tail[:, 1:, :])

        q, kvcat, ckv, krope = _proj_sample(x_s, x_s_off, n_s, w1, wlat, wqr, gq, gkv, cos_s, sin_s, tm=tm_s)
        o = _attn_sample(page_table, q, kvcat, cache_kv_latent, cache_rt, layer=l, t_new=ts)
        x1, meta, tail = _mix_sample(x_s, x_s_off, o, prefix[l], x1, meta, mix_w, row0=n_p, tm=tm_s, alpha=alpha,
                                     t_new=ts, past=past)
        outs[3].append(ckv.reshape(bs, ts, -1))
        outs[4].append(krope.reshape(bs, ts, -1))
        outs[5].append(tail[:, 1:, :])

        row_src, row_dst, tile_ea, tile_eb, n_used, wts = _routing_plan(meta, tm=tm_moe)
        x_all = _moe(x1, wts, row_src, row_dst, tile_ea, tile_eb, n_used, wg, wu, wd, g2, b2, tm=tm_moe, alpha=alpha)
        x_p, x_s, x_s_off = x_all, x_all, n_p // tm_s

    return (x_all[:n_p].reshape(bp, tp, d), x_all[n_p:n_all].reshape(bs, ts, d), jnp.stack(outs[0]),
            jnp.stack(outs[1]), jnp.stack(outs[2]), jnp.stack(outs[3]), jnp.stack(outs[4]), jnp.stack(outs[5]))
```

```python
import functools

import jax
import jax.numpy as jnp
from jax import lax
from jax.experimental import pallas as pl
from jax.experimental.pallas import tpu as pltpu
from jax.experimental.pallas import tpu_sc as plsc

F32 = jnp.float32
BF16 = jnp.bfloat16

N_HEADS = 8
QK_NOPE_DIM = 64
QK_ROPE_DIM = 32
Q_LORA_RANK = 256
KV_LORA_RANK = 128
QK_CAT = KV_LORA_RANK + QK_ROPE_DIM
ROPE_BASE = 10000.0
SM_SCALE = (QK_NOPE_DIM + QK_ROPE_DIM) ** -0.5
LOG2_E = 1.4426950408889634
Q_SCALE = SM_SCALE * LOG2_E
POOL_WINDOWS = (2, 4, 8, 16)
POOL_PAD = 16
N_EXPERTS = 16
N_EXPERT_GROUPS = 4
EXPERTS_PER_GROUP = 4
PAIRS_PER_GROUP = 6
PAIR_LO = (0, 0, 0, 1, 1, 2)
PAIR_HI = (1, 2, 3, 2, 3, 3)
N_BUCKETS = N_EXPERT_GROUPS * PAIRS_PER_GROUP
META_ROWS = 8
LN_EPS = 1e-5
RMS_EPS = 1e-6

LANES = 128
VMEM_LIMIT_BYTES = 56 * 1024 * 1024


def _params(*semantics):
    return pltpu.CompilerParams(dimension_semantics=semantics, vmem_limit_bytes=VMEM_LIMIT_BYTES)


def _const_spec(shape):
    zeros = (0,) * len(shape)
    return pl.BlockSpec(shape, lambda *_: zeros, pipeline_mode=pl.Buffered(1))


def _split_bf16(a):
    hi = a.astype(BF16)
    lo = (a - hi.astype(F32)).astype(BF16)
    return hi, lo


def _dot3(a, b, dims):
    a_hi, a_lo = _split_bf16(a)
    b_hi, b_lo = _split_bf16(b)
    d = functools.partial(lax.dot_general, dimension_numbers=dims, preferred_element_type=F32)
    return d(a_hi, b_hi) + (d(a_hi, b_lo) + d(a_lo, b_hi))


_NN = (((1,), (0,)), ((), ()))
_NT = (((1,), (1,)), ((), ()))


def _layer_norm(y, g, b):
    mu = jnp.mean(y, axis=-1, keepdims=True)
    d = y - mu
    var = jnp.mean(d * d, axis=-1, keepdims=True)
    return d * lax.rsqrt(var + LN_EPS) * g + b


def _rms_norm(y, g):
    return y * lax.rsqrt(jnp.mean(y * y, axis=-1, keepdims=True) + RMS_EPS) * g


def _fold_kernel(wqn_ref, wqn_t_ref, wk_ref, wk_t_ref, wv_ref, wba_ref, wlat_ref, wlat_t_ref, wva_ref):
    wlat_ref[...] = (_dot3(wqn_ref[0], wk_t_ref[0], _NN) * Q_SCALE).astype(BF16)
    wlat_t_ref[...] = (_dot3(wk_ref[0], wqn_t_ref[0], _NN) * Q_SCALE).astype(BF16)
    wva_ref[...] = _dot3(wv_ref[0], wba_ref[0], _NN).astype(BF16)


def _fold_weights(wq_nope, wk, wv, wba):
    h, r, dn = wq_nope.shape
    d = wba.shape[-1]
    head = lambda *blk: pl.BlockSpec((1,) + blk, lambda i: (i, 0, 0))
    return pl.pallas_call(
        _fold_kernel,
        grid=(h,),
        in_specs=[head(r, dn), head(dn, r), head(KV_LORA_RANK, dn), head(dn, KV_LORA_RANK),
                  head(KV_LORA_RANK, wv.shape[-1]), head(wba.shape[1], d)],
        out_specs=[pl.BlockSpec((r, KV_LORA_RANK), lambda i: (0, i)),
                   pl.BlockSpec((KV_LORA_RANK, r), lambda i: (i, 0)),
                   pl.BlockSpec((KV_LORA_RANK, d), lambda i: (i, 0))],
        out_shape=[jax.ShapeDtypeStruct((r, h * KV_LORA_RANK), BF16),
                   jax.ShapeDtypeStruct((h * KV_LORA_RANK, r), BF16),
                   jax.ShapeDtypeStruct((h * KV_LORA_RANK, d), BF16)],
        compiler_params=_params("arbitrary"),
        name="fold",
    )(wq_nope, jnp.swapaxes(wq_nope, 1, 2), wk, jnp.swapaxes(wk, 1, 2), wv, wba)


def _proj_common(x_ref, w1_ref, gq_ref, gkv_ref, cos_ref, sin_ref, ckv_ref, krope_ref):
    x = x_ref[...].astype(BF16)
    h = jnp.dot(x, w1_ref[...], preferred_element_type=F32)
    qa = h[:, :Q_LORA_RANK]
    kva = h[:, Q_LORA_RANK:Q_LORA_RANK + KV_LORA_RANK]
    kr = h[:, Q_LORA_RANK + KV_LORA_RANK:]
    qn = _rms_norm(qa, gq_ref[...])
    ckv = _rms_norm(kva, gkv_ref[...])
    half = QK_ROPE_DIM // 2
    kr_swapped = jnp.concatenate([kr[:, half:], kr[:, :half]], axis=1)
    krope = kr * cos_ref[:, :QK_ROPE_DIM] + kr_swapped * sin_ref[:, :QK_ROPE_DIM]
    ckv_ref[...] = ckv
    krope_ref[...] = krope
    return qn, ckv, krope


def _proj_sample_kernel(x_ref, w1_ref, wlat_ref, wqr_ref, gq_ref, gkv_ref, cos_ref, sin_ref,
                        q_ref, kvcat_ref, ckv_ref, krope_ref):
    qn, ckv, krope = _proj_common(x_ref, w1_ref, gq_ref, gkv_ref, cos_ref, sin_ref, ckv_ref, krope_ref)
    qn = qn.astype(BF16)
    ql = jnp.dot(qn, wlat_ref[...], preferred_element_type=F32)
    qr = jnp.dot(qn, wqr_ref[...], preferred_element_type=F32)
    half = QK_ROPE_DIM // 2
    width = qr.shape[1]
    lane = lax.broadcasted_iota(jnp.int32, qr.shape, 1)
    swapped = jnp.where((lane % QK_ROPE_DIM) < half,
                        pltpu.roll(qr, width - half, 1), pltpu.roll(qr, half, 1))
    qrr = qr * cos_ref[...] + swapped * sin_ref[...]
    for hh in range(N_HEADS):
        q_ref[hh, :, :KV_LORA_RANK] = ql[:, hh * KV_LORA_RANK:(hh + 1) * KV_LORA_RANK]
        q_ref[hh, :, KV_LORA_RANK:] = qrr[:, hh * QK_ROPE_DIM:(hh + 1) * QK_ROPE_DIM]
    kvcat_ref[:, :KV_LORA_RANK] = ckv
    kvcat_ref[:, KV_LORA_RANK:] = krope


def _proj_prompt_kernel(x_ref, w1_ref, wlat_t_ref, wqr_t_ref, gq_ref, gkv_ref, cos_ref, sin_ref,
                        cos_t_ref, sin_t_ref, qt_ref, kvcat_ref, ckv_t_ref, ckv_ref, krope_ref):
    qn, ckv, krope = _proj_common(x_ref, w1_ref, gq_ref, gkv_ref, cos_ref, sin_ref, ckv_ref, krope_ref)
    qn_t = qn.T.astype(BF16)
    ql_t = jnp.dot(wlat_t_ref[...], qn_t, preferred_element_type=F32)
    qr_t = jnp.dot(wqr_t_ref[...], qn_t, preferred_element_type=F32)
    half = QK_ROPE_DIM // 2
    pieces = []
    for hh in range(N_HEADS):
        base = hh * QK_ROPE_DIM
        pieces += [qr_t[base + half:base + QK_ROPE_DIM], qr_t[base:base + half]]
    qrr_t = qr_t * cos_t_ref[...] + jnp.concatenate(pieces, axis=0) * sin_t_ref[...]
    for hh in range(N_HEADS):
        qt_ref[hh, :KV_LORA_RANK, :] = ql_t[hh * KV_LORA_RANK:(hh + 1) * KV_LORA_RANK].astype(BF16)
        qt_ref[hh, KV_LORA_RANK:, :] = qrr_t[hh * QK_ROPE_DIM:(hh + 1) * QK_ROPE_DIM].astype(BF16)
    kvcat_ref[:, :KV_LORA_RANK] = ckv.astype(BF16)
    kvcat_ref[:, KV_LORA_RANK:] = krope.astype(BF16)
    ckv_t_ref[...] = ckv.T.astype(BF16)


def _proj_sample(x, x_off, n, w1, wlat, wqr, gq, gkv, cos_t, sin_t, *, tm):
    d = x.shape[1]
    row = lambda width: pl.BlockSpec((tm, width), lambda i: (i, 0))
    return pl.pallas_call(
        _proj_sample_kernel,
        grid=(n // tm,),
        in_specs=[pl.BlockSpec((tm, d), lambda i: (i + x_off, 0)), _const_spec(w1.shape), _const_spec(wlat.shape), _const_spec(wqr.shape),
                  _const_spec(gq.shape), _const_spec(gkv.shape),
                  _const_spec(cos_t.shape), _const_spec(sin_t.shape)],
        out_specs=[pl.BlockSpec((N_HEADS, tm, QK_CAT), lambda i: (0, i, 0)),
                   row(QK_CAT), row(KV_LORA_RANK), row(QK_ROPE_DIM)],
        out_shape=[jax.ShapeDtypeStruct((N_HEADS, n, QK_CAT), F32),
                   jax.ShapeDtypeStruct((n, QK_CAT), F32),
                   jax.ShapeDtypeStruct((n, KV_LORA_RANK), F32),
                   jax.ShapeDtypeStruct((n, QK_ROPE_DIM), F32)],
        compiler_params=_params("arbitrary"),
        name="proj_sample",
    )(x, w1, wlat, wqr, gq, gkv, cos_t, sin_t)


def _proj_prompt(x, n, w1, wlat_t, wqr_t, gq, gkv, cos, sin, cos_t, sin_t, *, tm):
    d = x.shape[1]
    tiles = cos.shape[0] // tm
    row = lambda width: pl.BlockSpec((tm, width), lambda i: (i, 0))
    return pl.pallas_call(
        _proj_prompt_kernel,
        grid=(n // tm,),
        in_specs=[row(d), _const_spec(w1.shape), _const_spec(wlat_t.shape), _const_spec(wqr_t.shape),
                  _const_spec(gq.shape), _const_spec(gkv.shape),
                  pl.BlockSpec((tm, cos.shape[1]), lambda i: (i % tiles, 0)),
                  pl.BlockSpec((tm, sin.shape[1]), lambda i: (i % tiles, 0)),
                  pl.BlockSpec((cos_t.shape[0], tm), lambda i: (0, i % tiles)),
                  pl.BlockSpec((sin_t.shape[0], tm), lambda i: (0, i % tiles))],
        out_specs=[pl.BlockSpec((N_HEADS, QK_CAT, tm), lambda i: (0, 0, i)),
                   row(QK_CAT),
                   pl.BlockSpec((KV_LORA_RANK, tm), lambda i: (0, i)),
                   row(KV_LORA_RANK), row(QK_ROPE_DIM)],
        out_shape=[jax.ShapeDtypeStruct((N_HEADS, QK_CAT, n), BF16),
                   jax.ShapeDtypeStruct((n, QK_CAT), BF16),
                   jax.ShapeDtypeStruct((KV_LORA_RANK, n), BF16),
                   jax.ShapeDtypeStruct((n, KV_LORA_RANK), F32),
                   jax.ShapeDtypeStruct((n, QK_ROPE_DIM), F32)],
        compiler_params=_params("arbitrary"),
        name="proj_prompt",
    )(x, w1, wlat_t, wqr_t, gq, gkv, cos, sin, cos_t, sin_t)


SCORE_AHEAD = 2
SCORE_SLOTS = 4


def _col_reduce(x, op, reduce_rows):
    rows = x.shape[0]
    while rows >= 32:
        x = x.reshape(4, rows // 4, x.shape[1])
        x = op(op(x[0], x[1]), op(x[2], x[3]))
        rows //= 4
    return reduce_rows(x, axis=0, keepdims=True)


def _attn_prompt_kernel(qt_ref, k_ref, vt_ref, o_ref, m_ref, l_ref, acc_ref, s_ref, *, tq, tk):
    i = pl.program_id(1)
    n_full = (i * tq) // tk
    m_ref[...] = jnp.full(m_ref.shape, -jnp.inf, F32)
    l_ref[...] = jnp.zeros(l_ref.shape, F32)
    acc_ref[...] = jnp.zeros(acc_ref.shape, F32)

    def scores(j, hh):
        k = k_ref[pl.ds(pl.multiple_of(j * tk, tk), tk), :]
        s_ref[hh % SCORE_SLOTS] = jnp.dot(k, qt_ref[hh], preferred_element_type=F32)

    def step(j, masked):
        vt = vt_ref[:, pl.ds(pl.multiple_of(j * tk, tk), tk)]
        if masked:
            key = j * tk + lax.broadcasted_iota(jnp.int32, (tk, tq), 0)
            tok = i * tq + lax.broadcasted_iota(jnp.int32, (tk, tq), 1)
            visible = key <= tok
        for hh in range(N_HEADS):
            ahead = hh + SCORE_AHEAD
            if ahead < N_HEADS:
                scores(j, ahead)
            elif not masked:
                scores(j + 1, ahead - N_HEADS)
            s = s_ref[hh % SCORE_SLOTS]
            if masked:
                s = jnp.where(visible, s, -jnp.inf)
            m_prev = m_ref[hh]
            m_new = jnp.maximum(m_prev, _col_reduce(s, jnp.maximum, jnp.max))
            alpha = jnp.exp2(m_prev - m_new)
            p = jnp.exp2(s - m_new)
            l_ref[hh] = alpha * l_ref[hh] + _col_reduce(p, jnp.add, jnp.sum)
            pv = jnp.dot(vt, p.astype(BF16), preferred_element_type=F32)
            acc_ref[hh] = alpha * acc_ref[hh] + pv
            m_ref[hh] = m_new

    def body(j, carry):
        step(j, False)
        return carry

    for hh in range(SCORE_AHEAD):
        scores(0, hh)
    lax.fori_loop(0, n_full, body, 0)
    step(n_full, True)
    for hh in range(N_HEADS):
        o_t = acc_ref[hh] / l_ref[hh]
        o_ref[:, hh * KV_LORA_RANK:(hh + 1) * KV_LORA_RANK] = o_t.T.astype(o_ref.dtype)


def _attn_prompt(qt, kvcat, ckv_t, *, batch, seq, tq, tk):
    nq = seq // tq
    n = batch * seq
    return pl.pallas_call(
        functools.partial(_attn_prompt_kernel, tq=tq, tk=tk),
        grid=(batch, nq),
        in_specs=[pl.BlockSpec((N_HEADS, QK_CAT, tq), lambda b, i: (0, 0, b * nq + i)),
                  pl.BlockSpec((seq, QK_CAT), lambda b, i: (b, 0)),
                  pl.BlockSpec((KV_LORA_RANK, seq), lambda b, i: (0, b))],
        out_specs=pl.BlockSpec((tq, N_HEADS * KV_LORA_RANK), lambda b, i: (b * nq + i, 0)),
        out_shape=jax.ShapeDtypeStruct((n, N_HEADS * KV_LORA_RANK), BF16),
        scratch_shapes=[pltpu.VMEM((N_HEADS, 1, tq), F32), pltpu.VMEM((N_HEADS, 1, tq), F32),
                        pltpu.VMEM((N_HEADS, KV_LORA_RANK, tq), F32), pltpu.VMEM((SCORE_SLOTS, tk, tq), F32)],
        compiler_params=_params("arbitrary", "arbitrary"),
        name="attn_prompt",
    )(qt, kvcat, ckv_t)


def _attn_sample_kernel(pt_ref, q_ref, knew_ref, cache_c_ref, cache_rt_ref, o_ref, cbuf, rbuf, sem,
                        *, layer, n_pages, page, t_new):
    b = pl.program_id(0)
    nb = pl.num_programs(0)

    def page_copies(seq_idx, slot):
        copies = []
        for p in range(n_pages):
            pid = pt_ref[seq_idx, p]
            copies.append(pltpu.make_async_copy(
                cache_c_ref.at[layer, pid], cbuf.at[slot, pl.ds(p * page, page), :], sem.at[slot, 0]))
            copies.append(pltpu.make_async_copy(
                cache_rt_ref.at[layer, pid], rbuf.at[slot, :, pl.ds(p * page, page)], sem.at[slot, 1]))
        return copies

    slot = b % 2

    @pl.when(b == 0)
    def _():
        for c in page_copies(0, 0):
            c.start()

    @pl.when(b + 1 < nb)
    def _():
        for c in page_copies(b + 1, 1 - slot):
            c.start()

    for c in page_copies(b, slot):
        c.wait()

    rows = N_HEADS * t_new
    q = q_ref[...].reshape(rows, QK_CAT).astype(BF16)
    kc = cbuf[slot].astype(BF16)
    krt = rbuf[slot].astype(BF16)
    kn = knew_ref[...].astype(BF16)
    s = (lax.dot_general(q[:, :KV_LORA_RANK], kc, _NT, preferred_element_type=F32)
         + jnp.dot(q[:, KV_LORA_RANK:], krt, preferred_element_type=F32))
    sn = lax.dot_general(q, kn, _NT, preferred_element_type=F32)
    tok = lax.broadcasted_iota(jnp.int32, (N_HEADS, t_new, t_new), 1).reshape(rows, t_new)
    key = lax.broadcasted_iota(jnp.int32, (rows, t_new), 1)
    sn = jnp.where(key <= tok, sn, -jnp.inf)
    m = jnp.maximum(jnp.max(s, axis=-1, keepdims=True), jnp.max(sn, axis=-1, keepdims=True))
    p = jnp.exp2(s - m)
    pn = jnp.exp2(sn - m)
    l = jnp.sum(p, axis=-1, keepdims=True) + jnp.sum(pn, axis=-1, keepdims=True)
    o = (jnp.dot(p.astype(BF16), kc, preferred_element_type=F32)
         + jnp.dot(pn.astype(BF16), kn[:, :KV_LORA_RANK], preferred_element_type=F32)) / l
    for hh in range(N_HEADS):
        o_ref[:, hh * KV_LORA_RANK:(hh + 1) * KV_LORA_RANK] = o[hh * t_new:(hh + 1) * t_new]


def _attn_sample(page_table, q, knew, cache_c, cache_rt, *, layer, t_new):
    n_seq, n_pages = page_table.shape
    page = cache_c.shape[2]
    past = n_pages * page
    grid_spec = pltpu.PrefetchScalarGridSpec(
        num_scalar_prefetch=1,
        grid=(n_seq,),
        in_specs=[pl.BlockSpec((N_HEADS, t_new, QK_CAT), lambda b, pt: (0, b, 0)),
                  pl.BlockSpec((t_new, QK_CAT), lambda b, pt: (b, 0)),
                  pl.BlockSpec(memory_space=pl.ANY),
                  pl.BlockSpec(memory_space=pl.ANY)],
        out_specs=pl.BlockSpec((t_new, N_HEADS * KV_LORA_RANK), lambda b, pt: (b, 0)),
        scratch_shapes=[pltpu.VMEM((2, past, KV_LORA_RANK), F32), pltpu.VMEM((2, QK_ROPE_DIM, past), F32),
                        pltpu.SemaphoreType.DMA((2, 2))],
    )
    return pl.pallas_call(
        functools.partial(_attn_sample_kernel, layer=layer, n_pages=n_pages, page=page, t_new=t_new),
        grid_spec=grid_spec,
        out_shape=jax.ShapeDtypeStruct((n_seq * t_new, N_HEADS * KV_LORA_RANK), F32),
        compiler_params=_params("arbitrary"),
        name="attn_sample",
    )(page_table, q, knew, cache_c, cache_rt)


def _router_meta(x1, wr, rb):
    logits = _dot3(wr, x1, _NT)
    score = jax.nn.sigmoid(logits)
    biased = score + rb
    row = lambda a, e: a[e:e + 1, :]
    group_score = []
    for g in range(N_EXPERT_GROUPS):
        v = [row(biased, g * EXPERTS_PER_GROUP + k) for k in range(EXPERTS_PER_GROUP)]
        best = None
        for a in range(EXPERTS_PER_GROUP):
            for c in range(a + 1, EXPERTS_PER_GROUP):
                pair = v[a] + v[c]
                best = pair if best is None else jnp.maximum(best, pair)
        group_score.append(best)
    top = group_score[0]
    grp = jnp.zeros(top.shape, jnp.int32)
    for g in range(1, N_EXPERT_GROUPS):
        better = group_score[g] > top
        grp = jnp.where(better, g, grp)
        top = jnp.maximum(top, group_score[g])

    def pick(a, k):
        out = row(a, (N_EXPERT_GROUPS - 1) * EXPERTS_PER_GROUP + k)
        for g in range(N_EXPERT_GROUPS - 2, -1, -1):
            out = jnp.where(grp == g, row(a, g * EXPERTS_PER_GROUP + k), out)
        return out

    cand = [pick(biased, k) for k in range(EXPERTS_PER_GROUP)]
    aff = [pick(score, k) for k in range(EXPERTS_PER_GROUP)]
    first = jnp.zeros(top.shape, jnp.int32)
    best = cand[0]
    for k in range(1, EXPERTS_PER_GROUP):
        better = cand[k] > best
        first = jnp.where(better, k, first)
        best = jnp.maximum(best, cand[k])
    second = jnp.full(top.shape, -1, jnp.int32)
    best2 = jnp.full(top.shape, -jnp.inf, F32)
    for k in range(EXPERTS_PER_GROUP):
        better = (first != k) & ((second < 0) | (cand[k] > best2))
        second = jnp.where(better, k, second)
        best2 = jnp.where(better, cand[k], best2)
    lo = jnp.minimum(first, second)
    hi = jnp.maximum(first, second)

    def take(vals, k_idx):
        out = vals[EXPERTS_PER_GROUP - 1]
        for k in range(EXPERTS_PER_GROUP - 2, -1, -1):
            out = jnp.where(k_idx == k, vals[k], out)
        return out

    aff_lo, aff_hi = take(aff, lo), take(aff, hi)
    denom = aff_lo + aff_hi
    pair = jnp.where(lo == 0, 0, jnp.where(lo == 1, 3, 5)) + (hi - lo - 1)
    bucket = (grp * PAIRS_PER_GROUP + pair).astype(F32)
    pad = jnp.zeros((META_ROWS - 3,) + top.shape[1:], F32)
    return jnp.concatenate([bucket, aff_lo / denom, aff_hi / denom, pad], axis=0)


def _mix_tail(x, o, pooled, gate_a, gate_b, wva_ref, wmix_ref, ps_ref, wbb_ref, wout_ref, g_ref, b_ref,
              wr_ref, rb_ref, x1_ref, gate_ref, *, alpha):
    gd = wmix_ref.shape[1]
    yb = jnp.concatenate(
        [jnp.dot(pooled[:, g * gd:(g + 1) * gd].astype(BF16), wmix_ref[g], preferred_element_type=F32)
         for g in range(len(POOL_WINDOWS))], axis=1) * ps_ref[...]
    ya = jnp.dot(o.astype(BF16), wva_ref[...], preferred_element_type=F32)
    yb = jnp.dot(yb.astype(BF16), wbb_ref[...], preferred_element_type=F32)
    merged = jax.nn.sigmoid(gate_a) * ya + jax.nn.sigmoid(gate_b) * yb
    mix = jnp.dot(merged.astype(BF16), wout_ref[...], preferred_element_type=F32)
    x1 = _layer_norm(alpha * x + mix, g_ref[...], b_ref[...])
    meta = _router_meta(x1, wr_ref[...], rb_ref[...])
    gate_ref[...] = meta
    d = x1.shape[1]
    x1_ref[:, :d] = x1
    wide = jnp.concatenate([meta[1:3], jnp.zeros((LANES - 2, meta.shape[1]), F32)], axis=0)
    x1_ref[:, d:] = wide.T


def _mix_prompt_kernel(*refs, alpha, tiles_per_seq, n_tiles):
    x1_ref, gate_ref = refs[-4], refs[-3]

    @pl.when(pl.program_id(0) < n_tiles)
    def _():
        _mix_prompt_tile(*refs, alpha=alpha, tiles_per_seq=tiles_per_seq)

    @pl.when(pl.program_id(0) >= n_tiles)
    def _():
        x1_ref[...] = jnp.zeros(x1_ref.shape, F32)
        gate_ref[...] = jnp.zeros(gate_ref.shape, F32)


def _mix_prompt_tile(x_ref, o_ref, w2_ref, wva_ref, wmix_ref, ps_ref, wbb_ref, wout_ref, g_ref, b_ref,
                     wr_ref, rb_ref, x1_ref, gate_ref, tail_ref, ext_ref, *, alpha, tiles_per_seq):
    tm = x_ref.shape[0]
    pw = ps_ref.shape[1]
    gd = pw // len(POOL_WINDOWS)
    x = x_ref[...]
    h2 = jnp.dot(x.astype(BF16), w2_ref[...], preferred_element_type=F32)
    u = h2[:, :pw]
    t = pl.program_id(0) % tiles_per_seq

    @pl.when(t == 0)
    def _():
        ext_ref[0:POOL_PAD, :] = jnp.zeros((POOL_PAD, pw), F32)

    @pl.when(t != 0)
    def _():
        ext_ref[0:POOL_PAD, :] = ext_ref[tm:tm + POOL_PAD, :]

    ext_ref[POOL_PAD:POOL_PAD + tm, :] = u
    tail_ref[0] = u[tm - POOL_PAD:, :]
    pos = t * tm + lax.broadcasted_iota(jnp.int32, (tm, 1), 0)
    pooled = []
    for g, w in enumerate(POOL_WINDOWS):
        acc = ext_ref[POOL_PAD:POOL_PAD + tm, g * gd:(g + 1) * gd]
        for j in range(1, w):
            acc = acc + ext_ref[POOL_PAD - j:POOL_PAD - j + tm, g * gd:(g + 1) * gd]
        cnt = jnp.minimum(w, pos + 1).astype(F32)
        pooled.append(acc / cnt - u[:, g * gd:(g + 1) * gd])
    pooled = jnp.concatenate(pooled, axis=1)
    d = x.shape[1]
    _mix_tail(x, o_ref[...], pooled, h2[:, pw:pw + d], h2[:, pw + d:], wva_ref, wmix_ref, ps_ref, wbb_ref,
              wout_ref, g_ref, b_ref, wr_ref, rb_ref, x1_ref, gate_ref, alpha=alpha)


def _mix_sample_kernel(x_ref, o_ref, pre_ref, x1_all_ref, meta_all_ref, w2_ref, wva_ref, wmix_ref, ps_ref, wbb_ref,
                       wout_ref, g_ref, b_ref, wr_ref, rb_ref, x1_ref, gate_ref, tail_ref, ext_ref,
                       *, alpha, t_new, past):
    del x1_all_ref, meta_all_ref
    tm = x_ref.shape[0]
    sb = tm // t_new
    pw = ps_ref.shape[1]
    gd = pw // len(POOL_WINDOWS)
    x = x_ref[...]
    h2 = jnp.dot(x.astype(BF16), w2_ref[...], preferred_element_type=F32)
    u = h2[:, :pw]
    ext_ref[:, 0:POOL_PAD, :] = pre_ref[...]
    ext_ref[:, POOL_PAD:POOL_PAD + t_new, :] = u.reshape(sb, t_new, pw)
    tail_ref[...] = ext_ref[:, t_new:t_new + POOL_PAD, :]
    pos = past + lax.broadcasted_iota(jnp.int32, (sb, t_new, 1), 1)
    pooled = []
    for g, w in enumerate(POOL_WINDOWS):
        acc = ext_ref[:, POOL_PAD:POOL_PAD + t_new, g * gd:(g + 1) * gd]
        for j in range(1, w):
            acc = acc + ext_ref[:, POOL_PAD - j:POOL_PAD - j + t_new, g * gd:(g + 1) * gd]
        cnt = jnp.minimum(w, pos + 1).astype(F32)
        pooled.append((acc / cnt).reshape(tm, gd) - u[:, g * gd:(g + 1) * gd])
    pooled = jnp.concatenate(pooled, axis=1)
    d = x.shape[1]
    _mix_tail(x, o_ref[...], pooled, h2[:, pw:pw + d], h2[:, pw + d:], wva_ref, wmix_ref, ps_ref, wbb_ref,
              wout_ref, g_ref, b_ref, wr_ref, rb_ref, x1_ref, gate_ref, alpha=alpha)


def _mix_prompt(x, x_off, o, weights, *, n_all, tm, alpha, seq):
    n = o.shape[0]
    d = x.shape[1]
    pw = weights[3].shape[1]
    tiles_per_seq = seq // tm
    n_tiles = n // tm
    last = n_tiles - 1
    return pl.pallas_call(
        functools.partial(_mix_prompt_kernel, alpha=alpha, tiles_per_seq=tiles_per_seq, n_tiles=n_tiles),
        grid=(n_all // tm,),
        in_specs=[pl.BlockSpec((tm, d), lambda i: (jnp.minimum(i, last) + x_off, 0)),
                  pl.BlockSpec((tm, o.shape[1]), lambda i: (jnp.minimum(i, last), 0))]
        + [_const_spec(a.shape) for a in weights],
        out_specs=[pl.BlockSpec((tm, d + LANES), lambda i: (i, 0)),
                   pl.BlockSpec((META_ROWS, tm), lambda i: (0, i)),
                   pl.BlockSpec((1, POOL_PAD, pw), lambda i: (jnp.minimum(i, last) // tiles_per_seq, 0, 0))],
        out_shape=[jax.ShapeDtypeStruct((n_all, d + LANES), F32), jax.ShapeDtypeStruct((META_ROWS, n_all), F32),
                   jax.ShapeDtypeStruct((n // seq, POOL_PAD, pw), F32)],
        scratch_shapes=[pltpu.VMEM((tm + POOL_PAD, pw), F32)],
        compiler_params=_params("arbitrary"),
        name="mix_prompt",
    )(x, o, *weights)


def _mix_sample(x, x_off, o, prefix, x1_all, meta_all, weights, *, row0, tm, alpha, t_new, past):
    n = o.shape[0]
    d = x.shape[1]
    pw = weights[3].shape[1]
    sb = tm // t_new
    off = row0 // tm
    any_spec = pl.BlockSpec(memory_space=pl.ANY)
    return pl.pallas_call(
        functools.partial(_mix_sample_kernel, alpha=alpha, t_new=t_new, past=past),
        grid=(n // tm,),
        in_specs=[pl.BlockSpec((tm, d), lambda i: (i + x_off, 0)), pl.BlockSpec((tm, o.shape[1]), lambda i: (i, 0)),
                  pl.BlockSpec((sb, POOL_PAD, pw), lambda i: (i, 0, 0)), any_spec, any_spec]
        + [_const_spec(a.shape) for a in weights],
        out_specs=[pl.BlockSpec((tm, d + LANES), lambda i: (i + off, 0)),
                   pl.BlockSpec((META_ROWS, tm), lambda i: (0, i + off)),
                   pl.BlockSpec((sb, POOL_PAD, pw), lambda i: (i, 0, 0))],
        out_shape=[jax.ShapeDtypeStruct(x1_all.shape, F32), jax.ShapeDtypeStruct(meta_all.shape, F32),
                   jax.ShapeDtypeStruct((n // t_new, POOL_PAD, pw), F32)],
        input_output_aliases={3: 0, 4: 1},
        scratch_shapes=[pltpu.VMEM((sb, POOL_PAD + t_new, pw), F32)],
        compiler_params=_params("arbitrary"),
        name="mix_sample",
    )(x, o, prefix, x1_all, meta_all, *weights)


GATHER_UNROLL = 8


def _gather_rows_kernel(idx_ref, tbl_ref, out_ref):
    tm = out_ref.shape[0]
    base = pl.program_id(1) * tm

    def body(g, carry):
        for u in range(GATHER_UNROLL):
            r = g * GATHER_UNROLL + u
            out_ref[pl.ds(r, 1), :] = tbl_ref[pl.ds(idx_ref[base + r], 1), :]
        return carry

    lax.fori_loop(0, tm // GATHER_UNROLL, body, 0)


def _gather_rows(table, idx, *, tm, cw):
    r, d = table.shape
    m = idx.shape[0]
    grid_spec = pltpu.PrefetchScalarGridSpec(
        num_scalar_prefetch=1,
        grid=(d // cw, m // tm),
        in_specs=[pl.BlockSpec((r, cw), lambda c, i, idx_ref: (0, c), pipeline_mode=pl.Buffered(1))],
        out_specs=pl.BlockSpec((tm, cw), lambda c, i, idx_ref: (i, c)),
    )
    return pl.pallas_call(
        _gather_rows_kernel,
        grid_spec=grid_spec,
        out_shape=jax.ShapeDtypeStruct((m, d), table.dtype),
        compiler_params=_params("arbitrary", "arbitrary"),
        name="gather_rows",
    )(idx, table)


SC_WORKERS = 32
SC_CHUNK = 16
SC_BUFFERS = 4


def _sc_gather_rows(table, idx):
    m = idx.shape[0]
    d = table.shape[1]
    per_worker = m // SC_WORKERS
    n_chunks = per_worker // SC_CHUNK
    assert per_worker * SC_WORKERS == m and n_chunks * SC_CHUNK == per_worker, (m, SC_WORKERS, SC_CHUNK)
    mesh = plsc.VectorSubcoreMesh(core_axis_name="c", subcore_axis_name="s")

    @functools.partial(
        pl.kernel, mesh=mesh, out_type=jax.ShapeDtypeStruct((m, d), table.dtype),
        scratch_types=[pltpu.VMEM((per_worker,), jnp.int32), pltpu.VMEM((SC_BUFFERS, SC_CHUNK, d), table.dtype),
                       pltpu.SemaphoreType.DMA((SC_BUFFERS,)), pltpu.SemaphoreType.DMA((SC_BUFFERS,))],
        name="sc_gather_rows")
    def gather(table_hbm, idx_hbm, out_hbm, idx_v, rows_v, gather_sem, write_sem):
        worker = lax.axis_index("s") * mesh.num_cores + lax.axis_index("c")
        base = worker * per_worker
        pltpu.sync_copy(idx_hbm.at[pl.ds(pl.multiple_of(base, SC_CHUNK), per_worker)], idx_v)

        def fetch(c, b):
            rows = idx_v.at[pl.ds(pl.multiple_of(c * SC_CHUNK, SC_CHUNK), SC_CHUNK)]
            return pltpu.make_async_copy(table_hbm.at[rows], rows_v.at[b], gather_sem.at[b])

        def write(c, b):
            dst = out_hbm.at[pl.ds(pl.multiple_of(base + c * SC_CHUNK, SC_CHUNK), SC_CHUNK)]
            return pltpu.make_async_copy(rows_v.at[b], dst, write_sem.at[b])

        for b in range(min(SC_BUFFERS, n_chunks)):
            fetch(b, b).start()

        @pl.loop(0, pl.cdiv(n_chunks, SC_BUFFERS))
        def _(p):
            for b in range(SC_BUFFERS):
                c = p * SC_BUFFERS + b

                @pl.when(c < n_chunks)
                def _():
                    fetch(c, b).wait()
                    write(c, b).start()

                    @pl.when(c + SC_BUFFERS < n_chunks)
                    def _():
                        write(c, b).wait()
                        fetch(c + SC_BUFFERS, b).start()

        for b in range(min(SC_BUFFERS, n_chunks)):
            last = ((n_chunks - 1 - b) // SC_BUFFERS) * SC_BUFFERS + b
            write(last, b).wait()

    return gather(table, idx)


def _moe_kernel(src_ref, dst_ref, ea_ref, eb_ref, used_ref, x1_hbm, wga_ref, wua_ref, wda_ref, wgb_ref,
                wub_ref, wdb_ref, g_ref, b_ref, out_hbm, xbuf, ybuf, gather_sem, scatter_sem, *, alpha, tm):
    del ea_ref, eb_ref
    i = pl.program_id(0)
    used = used_ref[0]
    buf = i % 2
    d = out_hbm.shape[1]

    def gather_row(tile, b, r):
        src = x1_hbm.at[pl.ds(src_ref[tile * tm + r], 1), :]
        return pltpu.make_async_copy(src, xbuf.at[b, pl.ds(r, 1), :], gather_sem.at[b])

    def scatter_row(tile, b, r):
        dst = out_hbm.at[pl.ds(dst_ref[tile * tm + r], 1), :]
        return pltpu.make_async_copy(ybuf.at[b, pl.ds(r, 1), :], dst, scatter_sem.at[b])

    def start_gather(tile, b):
        for r in range(tm):
            gather_row(tile, b, r).start()

    def wait_gather(b):
        for r in range(tm):
            gather_row(0, b, 0).wait()

    def wait_scatter(b):
        for r in range(tm):
            scatter_row(0, b, 0).wait()

    @pl.when(i == 0)
    def _():
        start_gather(0, 0)
        n_rows, n_out = x1_hbm.shape[0], out_hbm.shape[0]
        ybuf[1] = jnp.zeros(ybuf.shape[1:], F32)
        fills = [pltpu.make_async_copy(ybuf.at[1], out_hbm.at[pl.ds(row, tm), :], scatter_sem.at[1])
                 for row in range(n_rows, n_out, tm)]
        for c in fills:
            c.start()
        for c in fills:
            c.wait()

    for b in range(2):
        @pl.when((i < used) & (buf == b))
        def _():
            @pl.when(i + 1 < used)
            def _():
                start_gather(i + 1, 1 - b)

            wait_gather(b)

            @pl.when(i >= 2)
            def _():
                wait_scatter(b)

    @pl.when(i < used)
    def _():
        row = xbuf[buf]
        x = row[:, :d]
        wt = row[:, d:]
        xb = x.astype(BF16)

        def expert(wg_ref, wu_ref, wd_ref, col):
            hg = jnp.dot(xb, wg_ref[0], preferred_element_type=F32)
            hu = jnp.dot(xb, wu_ref[0], preferred_element_type=F32)
            hid = (jax.nn.silu(hg) * hu * col).astype(BF16)
            return jnp.dot(hid, wd_ref[0], preferred_element_type=F32)

        y = expert(wga_ref, wua_ref, wda_ref, wt[:, 0:1]) + expert(wgb_ref, wub_ref, wdb_ref, wt[:, 1:2])
        ybuf[buf] = _layer_norm(alpha * x + y, g_ref[...], b_ref[...])

    for b in range(2):
        @pl.when((i < used) & (buf == b))
        def _():
            for r in range(tm):
                scatter_row(i, b, r).start()

    @pl.when(i == pl.num_programs(0) - 1)
    def _():
        wait_scatter((used - 1) % 2)

        @pl.when(used >= 2)
        def _():
            wait_scatter(used % 2)


def _moe(x1, row_src, row_dst, tile_ea, tile_eb, n_used, wg, wu, wd, g2, b2, *, tm, alpha):
    p = row_src.shape[0]
    d = g2.shape[1]
    f = wg.shape[2]
    first = lambda i, src, dst, ea, eb, used: (ea[i], 0, 0)
    second = lambda i, src, dst, ea, eb, used: (eb[i], 0, 0)
    const = lambda shape: pl.BlockSpec(shape, lambda i, *pre: (0,) * len(shape), pipeline_mode=pl.Buffered(1))
    grid_spec = pltpu.PrefetchScalarGridSpec(
        num_scalar_prefetch=5,
        grid=(p // tm,),
        in_specs=[pl.BlockSpec(memory_space=pl.ANY),
                  pl.BlockSpec((1, d, f), first), pl.BlockSpec((1, d, f), first), pl.BlockSpec((1, f, d), first),
                  pl.BlockSpec((1, d, f), second), pl.BlockSpec((1, d, f), second), pl.BlockSpec((1, f, d), second),
                  const(g2.shape), const(b2.shape)],
        out_specs=pl.BlockSpec(memory_space=pl.ANY),
        scratch_shapes=[pltpu.VMEM((2, tm, x1.shape[1]), F32), pltpu.VMEM((2, tm, d), F32),
                        pltpu.SemaphoreType.DMA((2,)), pltpu.SemaphoreType.DMA((2,))],
    )
    return pl.pallas_call(
        functools.partial(_moe_kernel, alpha=alpha, tm=tm),
        grid_spec=grid_spec,
        out_shape=jax.ShapeDtypeStruct((p, d), F32),
        compiler_params=_params("arbitrary"),
        name="moe",
    )(row_src, row_dst, tile_ea, tile_eb, n_used, x1, wg, wu, wd, wg, wu, wd, g2, b2)


def _routing_plan(meta, *, tm):
    n = meta.shape[1]
    n_tiles = (n + N_BUCKETS * (tm - 1) + tm - 1) // tm
    p = n_tiles * tm
    bucket = meta[0].astype(jnp.int32)
    onehot = (bucket[:, None] == jnp.arange(N_BUCKETS, dtype=jnp.int32)[None, :]).astype(jnp.int32)
    csum = jnp.cumsum(onehot, axis=0)
    rank = jnp.sum(csum * onehot, axis=1) - 1
    tiles_per_bucket = (csum[-1] + tm - 1) // tm
    tile_end = jnp.cumsum(tiles_per_bucket)
    n_used = tile_end[-1:]
    row_start = (tile_end - tiles_per_bucket) * tm
    slot = jnp.sum(onehot * row_start[None, :], axis=1) + rank
    token = jnp.full((p,), -1, jnp.int32).at[slot].set(jnp.arange(n, dtype=jnp.int32), unique_indices=True)
    is_pad = token < 0
    row_src = jnp.where(is_pad, 0, token)
    row_dst = jnp.where(is_pad, n - 1 + jnp.cumsum(is_pad.astype(jnp.int32)), token)
    tile_id = jnp.minimum(jnp.arange(n_tiles, dtype=jnp.int32), n_used - 1)
    tile_bucket = jnp.minimum(jnp.sum((tile_id[:, None] >= tile_end[None, :]).astype(jnp.int32), axis=1),
                              N_BUCKETS - 1)
    group, pair = tile_bucket // PAIRS_PER_GROUP, tile_bucket % PAIRS_PER_GROUP
    lo = sum((pair == k).astype(jnp.int32) * PAIR_LO[k] for k in range(PAIRS_PER_GROUP))
    hi = sum((pair == k).astype(jnp.int32) * PAIR_HI[k] for k in range(PAIRS_PER_GROUP))
    tile_ea = group * EXPERTS_PER_GROUP + lo
    tile_eb = group * EXPERTS_PER_GROUP + hi
    return row_src, row_dst, tile_ea, tile_eb, n_used.astype(jnp.int32)


def _rope_tables(pos, reps):
    half = QK_ROPE_DIM // 2
    inv = ROPE_BASE ** (-jnp.arange(half, dtype=F32) / half)
    ang = pos.astype(F32)[:, None] * inv
    cos, sin = jnp.cos(ang), jnp.sin(ang)
    return (jnp.tile(jnp.concatenate([cos, cos], axis=1), (1, reps)),
            jnp.tile(jnp.concatenate([-sin, sin], axis=1), (1, reps)))


def kernel(x_prompt, x_sample, cache_kv_latent, cache_k_rope, state_pool, page_table, w_in, g_q_norm, g_kv_norm,
           w_q_up, w_k_up, w_v_up, w_pool_mix, pool_scale, w_branch_a, w_branch_b, w_out, ln1_g, ln1_b,
           w_router, router_bias, w_exp_gate, w_exp_up, w_exp_down, ln2_g, ln2_b):
    bp, tp, d = x_prompt.shape
    bs, ts, _ = x_sample.shape
    depth = w_in.shape[0]
    n_pages = page_table.shape[1]
    past = n_pages * cache_kv_latent.shape[2]
    alpha = (2 * depth) ** 0.25
    pw = pool_scale.shape[1]
    n_p, n_s = bp * tp, bs * ts
    s1 = Q_LORA_RANK + KV_LORA_RANK + QK_ROPE_DIM

    tm_p = min(512, tp)
    tq = min(512, tp)
    tk = min(512, tp)
    tm_mix_p = min(512, tp)
    tm_s = min(256, n_s)
    tm_moe = 256
    tm_gather = min(1024, n_s)
    n_all = n_p + n_s

    cos_p, sin_p = _rope_tables(jnp.arange(tp, dtype=jnp.int32), N_HEADS)
    cos_pt, sin_pt = cos_p.T, sin_p.T
    cos_p, sin_p = cos_p[:, :QK_ROPE_DIM], sin_p[:, :QK_ROPE_DIM]
    cos_s, sin_s = _rope_tables(past + jnp.arange(ts, dtype=jnp.int32), N_HEADS)
    cos_s, sin_s = jnp.tile(cos_s, (tm_s // ts, 1)), jnp.tile(sin_s, (tm_s // ts, 1))
    prefix = jnp.pad(state_pool, ((0, 0), (0, 0), (POOL_PAD - state_pool.shape[2], 0), (0, 0)))
    cache_rt = jnp.swapaxes(cache_k_rope, 2, 3)
    wr_t = w_router.T
    rb = router_bias.reshape(N_EXPERTS, 1)

    x_p, x_s, x_s_off = x_prompt.reshape(n_p, d), x_sample.reshape(n_s, d), 0
    outs = [[] for _ in range(6)]
    for l in range(depth):
        wq = w_q_up[l]
        wq_nope = jnp.transpose(wq[:, :, :QK_NOPE_DIM], (1, 0, 2))
        wqr = (wq[:, :, QK_NOPE_DIM:] * Q_SCALE).reshape(Q_LORA_RANK, N_HEADS * QK_ROPE_DIM).astype(BF16)
        wk = jnp.transpose(w_k_up[l], (1, 0, 2))
        wv = jnp.transpose(w_v_up[l], (1, 0, 2))
        wba = w_branch_a[l].reshape(N_HEADS, -1, d)
        wlat, wlat_t, wva = _fold_weights(wq_nope, wk, wv, wba)
        w1 = w_in[l, :, :s1].astype(BF16)
        w2 = w_in[l, :, s1:].astype(BF16)
        gq = g_q_norm[l].reshape(1, -1)
        gkv = g_kv_norm[l].reshape(1, -1)
        mix_w = (w2, wva, w_pool_mix[l].astype(BF16), pool_scale[l].reshape(1, pw), w_branch_b[l].astype(BF16),
                 w_out[l].astype(BF16), ln1_g[l].reshape(1, d), ln1_b[l].reshape(1, d), wr_t, rb)
        wg, wu, wd = w_exp_gate[l].astype(BF16), w_exp_up[l].astype(BF16), w_exp_down[l].astype(BF16)
        g2, b2 = ln2_g[l].reshape(1, d), ln2_b[l].reshape(1, d)

        qt, kvcat, ckv_t, ckv, krope = _proj_prompt(x_p, n_p, w1, wlat_t, wqr.T, gq, gkv, cos_p, sin_p, cos_pt,
                                                    sin_pt, tm=tm_p)
        o = _attn_prompt(qt, kvcat, ckv_t, batch=bp, seq=tp, tq=tq, tk=tk)
        x1, meta, tail = _mix_prompt(x_p, 0, o, mix_w, n_all=n_all, tm=tm_mix_p, alpha=alpha, seq=tp)
        outs[0].append(ckv.reshape(bp, tp, -1))
        outs[1].append(krope.reshape(bp, tp, -1))
        outs[2].append(tail[:, 1:, :])

        q, kvcat, ckv, krope = _proj_sample(x_s, x_s_off, n_s, w1, wlat, wqr, gq, gkv, cos_s, sin_s, tm=tm_s)
        o = _attn_sample(page_table, q, kvcat, cache_kv_latent, cache_rt, layer=l, t_new=ts)
        x1, meta, tail = _mix_sample(x_s, x_s_off, o, prefix[l], x1, meta, mix_w, row0=n_p, tm=tm_s, alpha=alpha,
                                     t_new=ts, past=past)
        outs[3].append(ckv.reshape(bs, ts, -1))
        outs[4].append(krope.reshape(bs, ts, -1))
        outs[5].append(tail[:, 1:, :])

        row_src, row_dst, tile_ea, tile_eb, n_used = _routing_plan(meta, tm=tm_moe)
        x_all = _moe(x1, row_src, row_dst, tile_ea, tile_eb, n_used, wg, wu, wd, g2, b2, tm=tm_moe, alpha=alpha)
        x_p, x_s, x_s_off = x_all, x_all, n_p // tm_s

    return (x_all[:n_p].reshape(bp, tp, d), x_all[n_p:n_all].reshape(bs, ts, d), jnp.stack(outs[0]),
            jnp.stack(outs[1]), jnp.stack(outs[2]), jnp.stack(outs[3]), jnp.stack(outs[4]), jnp.stack(outs[5]))
```

```python
import functools
from typing import NamedTuple

import jax
import jax.numpy as jnp
from jax import lax
from jax.experimental import pallas as pl
from jax.experimental.pallas import tpu as pltpu

F32 = jnp.float32
BF16 = jnp.bfloat16

N_HEADS = 8
QK_NOPE_DIM = 64
QK_ROPE_DIM = 32
Q_LORA_RANK = 256
KV_LORA_RANK = 128
QK_CAT = KV_LORA_RANK + QK_ROPE_DIM
ROPE_BASE = 10000.0
SM_SCALE = (QK_NOPE_DIM + QK_ROPE_DIM) ** -0.5
LOG2_E = 1.4426950408889634
Q_SCALE = SM_SCALE * LOG2_E
POOL_WINDOWS = (2, 4, 8, 16)
POOL_PAD = 16
N_EXPERTS = 16
N_EXPERT_GROUPS = 4
EXPERTS_PER_GROUP = 4
PAIRS_PER_GROUP = 6
PAIR_LO = (0, 0, 0, 1, 1, 2)
PAIR_HI = (1, 2, 3, 2, 3, 3)
N_BUCKETS = N_EXPERT_GROUPS * PAIRS_PER_GROUP
META_ROWS = 8
LN_EPS = 1e-5
RMS_EPS = 1e-6

LANES = 128
VMEM_LIMIT_BYTES = 56 * 1024 * 1024


def _params(*semantics):
    return pltpu.CompilerParams(dimension_semantics=semantics, vmem_limit_bytes=VMEM_LIMIT_BYTES)


def _const_spec(shape):
    zeros = (0,) * len(shape)
    return pl.BlockSpec(shape, lambda *_: zeros, pipeline_mode=pl.Buffered(1))


def _split_bf16(a):
    hi = a.astype(BF16)
    lo = (a - hi.astype(F32)).astype(BF16)
    return hi, lo


def _dot3(a, b, dims):
    a_hi, a_lo = _split_bf16(a)
    b_hi, b_lo = _split_bf16(b)
    d = functools.partial(lax.dot_general, dimension_numbers=dims, preferred_element_type=F32)
    return d(a_hi, b_hi) + (d(a_hi, b_lo) + d(a_lo, b_hi))


_NN = (((1,), (0,)), ((), ()))
_NT = (((1,), (1,)), ((), ()))


def _layer_norm(y, g, b):
    mu = jnp.mean(y, axis=-1, keepdims=True)
    d = y - mu
    var = jnp.mean(d * d, axis=-1, keepdims=True)
    return d * lax.rsqrt(var + LN_EPS) * g + b


def _rms_norm(y, g):
    return y * lax.rsqrt(jnp.mean(y * y, axis=-1, keepdims=True) + RMS_EPS) * g


def _fold_kernel(wqn_ref, wqn_t_ref, wk_ref, wk_t_ref, wv_ref, wba_ref, wlat_ref, wlat_t_ref, wva_ref):
    wlat_ref[...] = (_dot3(wqn_ref[0], wk_t_ref[0], _NN) * Q_SCALE).astype(BF16)
    wlat_t_ref[...] = (_dot3(wk_ref[0], wqn_t_ref[0], _NN) * Q_SCALE).astype(BF16)
    wva_ref[...] = _dot3(wv_ref[0], wba_ref[0], _NN).astype(BF16)


def _fold_weights(wq_nope, wk, wv, wba):
    h, r, dn = wq_nope.shape
    d = wba.shape[-1]
    head = lambda *blk: pl.BlockSpec((1,) + blk, lambda i: (i, 0, 0))
    return pl.pallas_call(
        _fold_kernel,
        grid=(h,),
        in_specs=[head(r, dn), head(dn, r), head(KV_LORA_RANK, dn), head(dn, KV_LORA_RANK),
                  head(KV_LORA_RANK, wv.shape[-1]), head(wba.shape[1], d)],
        out_specs=[pl.BlockSpec((r, KV_LORA_RANK), lambda i: (0, i)),
                   pl.BlockSpec((KV_LORA_RANK, r), lambda i: (i, 0)),
                   pl.BlockSpec((KV_LORA_RANK, d), lambda i: (i, 0))],
        out_shape=[jax.ShapeDtypeStruct((r, h * KV_LORA_RANK), BF16),
                   jax.ShapeDtypeStruct((h * KV_LORA_RANK, r), BF16),
                   jax.ShapeDtypeStruct((h * KV_LORA_RANK, d), BF16)],
        compiler_params=_params("arbitrary"),
        name="fold",
    )(wq_nope, jnp.swapaxes(wq_nope, 1, 2), wk, jnp.swapaxes(wk, 1, 2), wv, wba)


def _proj_common(x_ref, w1_ref, gq_ref, gkv_ref, cos_ref, sin_ref, ckv_ref, krope_ref):
    x = x_ref[...].astype(BF16)
    h = jnp.dot(x, w1_ref[...], preferred_element_type=F32)
    qa = h[:, :Q_LORA_RANK]
    kva = h[:, Q_LORA_RANK:Q_LORA_RANK + KV_LORA_RANK]
    kr = h[:, Q_LORA_RANK + KV_LORA_RANK:]
    qn = _rms_norm(qa, gq_ref[...])
    ckv = _rms_norm(kva, gkv_ref[...])
    half = QK_ROPE_DIM // 2
    kr_swapped = jnp.concatenate([kr[:, half:], kr[:, :half]], axis=1)
    krope = kr * cos_ref[:, :QK_ROPE_DIM] + kr_swapped * sin_ref[:, :QK_ROPE_DIM]
    ckv_ref[...] = ckv
    krope_ref[...] = krope
    return qn, ckv, krope


def _proj_sample_kernel(x_ref, w1_ref, wlat_ref, wqr_ref, gq_ref, gkv_ref, cos_ref, sin_ref,
                        q_ref, kvcat_ref, ckv_ref, krope_ref):
    qn, ckv, krope = _proj_common(x_ref, w1_ref, gq_ref, gkv_ref, cos_ref, sin_ref, ckv_ref, krope_ref)
    qn = qn.astype(BF16)
    ql = jnp.dot(qn, wlat_ref[...], preferred_element_type=F32)
    qr = jnp.dot(qn, wqr_ref[...], preferred_element_type=F32)
    half = QK_ROPE_DIM // 2
    width = qr.shape[1]
    lane = lax.broadcasted_iota(jnp.int32, qr.shape, 1)
    swapped = jnp.where((lane % QK_ROPE_DIM) < half,
                        pltpu.roll(qr, width - half, 1), pltpu.roll(qr, half, 1))
    qrr = qr * cos_ref[...] + swapped * sin_ref[...]
    for hh in range(N_HEADS):
        q_ref[hh, :, :KV_LORA_RANK] = ql[:, hh * KV_LORA_RANK:(hh + 1) * KV_LORA_RANK]
        q_ref[hh, :, KV_LORA_RANK:] = qrr[:, hh * QK_ROPE_DIM:(hh + 1) * QK_ROPE_DIM]
    kvcat_ref[:, :KV_LORA_RANK] = ckv
    kvcat_ref[:, KV_LORA_RANK:] = krope


def _proj_prompt_kernel(x_ref, w1_ref, wlat_t_ref, wqr_t_ref, gq_ref, gkv_ref, cos_ref, sin_ref,
                        cos_t_ref, sin_t_ref, qt_ref, kvcat_ref, ckv_t_ref, ckv_ref, krope_ref):
    qn, ckv, krope = _proj_common(x_ref, w1_ref, gq_ref, gkv_ref, cos_ref, sin_ref, ckv_ref, krope_ref)
    qn_t = qn.T.astype(BF16)
    ql_t = jnp.dot(wlat_t_ref[...], qn_t, preferred_element_type=F32)
    qr_t = jnp.dot(wqr_t_ref[...], qn_t, preferred_element_type=F32)
    half = QK_ROPE_DIM // 2
    pieces = []
    for hh in range(N_HEADS):
        base = hh * QK_ROPE_DIM
        pieces += [qr_t[base + half:base + QK_ROPE_DIM], qr_t[base:base + half]]
    qrr_t = qr_t * cos_t_ref[...] + jnp.concatenate(pieces, axis=0) * sin_t_ref[...]
    for hh in range(N_HEADS):
        qt_ref[hh, :KV_LORA_RANK, :] = ql_t[hh * KV_LORA_RANK:(hh + 1) * KV_LORA_RANK].astype(BF16)
        qt_ref[hh, KV_LORA_RANK:, :] = qrr_t[hh * QK_ROPE_DIM:(hh + 1) * QK_ROPE_DIM].astype(BF16)
    kvcat_ref[:, :KV_LORA_RANK] = ckv.astype(BF16)
    kvcat_ref[:, KV_LORA_RANK:] = krope.astype(BF16)
    ckv_t_ref[...] = ckv.T.astype(BF16)


def _proj_sample(x, x_off, n, w1, wlat, wqr, gq, gkv, cos_t, sin_t, *, tm):
    d = x.shape[1]
    row = lambda width: pl.BlockSpec((tm, width), lambda i: (i, 0))
    return pl.pallas_call(
        _proj_sample_kernel,
        grid=(n // tm,),
        in_specs=[pl.BlockSpec((tm, d), lambda i: (i + x_off, 0)), _const_spec(w1.shape), _const_spec(wlat.shape), _const_spec(wqr.shape),
                  _const_spec(gq.shape), _const_spec(gkv.shape),
                  _const_spec(cos_t.shape), _const_spec(sin_t.shape)],
        out_specs=[pl.BlockSpec((N_HEADS, tm, QK_CAT), lambda i: (0, i, 0)),
                   row(QK_CAT), row(KV_LORA_RANK), row(QK_ROPE_DIM)],
        out_shape=[jax.ShapeDtypeStruct((N_HEADS, n, QK_CAT), F32),
                   jax.ShapeDtypeStruct((n, QK_CAT), F32),
                   jax.ShapeDtypeStruct((n, KV_LORA_RANK), F32),
                   jax.ShapeDtypeStruct((n, QK_ROPE_DIM), F32)],
        compiler_params=_params("arbitrary"),
        name="proj_sample",
    )(x, w1, wlat, wqr, gq, gkv, cos_t, sin_t)


def _proj_prompt(x, n, w1, wlat_t, wqr_t, gq, gkv, cos, sin, cos_t, sin_t, *, tm):
    d = x.shape[1]
    tiles = cos.shape[0] // tm
    row = lambda width: pl.BlockSpec((tm, width), lambda i: (i, 0))
    return pl.pallas_call(
        _proj_prompt_kernel,
        grid=(n // tm,),
        in_specs=[row(d), _const_spec(w1.shape), _const_spec(wlat_t.shape), _const_spec(wqr_t.shape),
                  _const_spec(gq.shape), _const_spec(gkv.shape),
                  pl.BlockSpec((tm, cos.shape[1]), lambda i: (i % tiles, 0)),
                  pl.BlockSpec((tm, sin.shape[1]), lambda i: (i % tiles, 0)),
                  pl.BlockSpec((cos_t.shape[0], tm), lambda i: (0, i % tiles)),
                  pl.BlockSpec((sin_t.shape[0], tm), lambda i: (0, i % tiles))],
        out_specs=[pl.BlockSpec((N_HEADS, QK_CAT, tm), lambda i: (0, 0, i)),
                   row(QK_CAT),
                   pl.BlockSpec((KV_LORA_RANK, tm), lambda i: (0, i)),
                   row(KV_LORA_RANK), row(QK_ROPE_DIM)],
        out_shape=[jax.ShapeDtypeStruct((N_HEADS, QK_CAT, n), BF16),
                   jax.ShapeDtypeStruct((n, QK_CAT), BF16),
                   jax.ShapeDtypeStruct((KV_LORA_RANK, n), BF16),
                   jax.ShapeDtypeStruct((n, KV_LORA_RANK), F32),
                   jax.ShapeDtypeStruct((n, QK_ROPE_DIM), F32)],
        compiler_params=_params("arbitrary"),
        name="proj_prompt",
    )(x, w1, wlat_t, wqr_t, gq, gkv, cos, sin, cos_t, sin_t)


SCORE_AHEAD = 2
SCORE_SLOTS = 4


def _col_reduce(x, op, reduce_rows):
    rows = x.shape[0]
    while rows >= 32:
        x = x.reshape(4, rows // 4, x.shape[1])
        x = op(op(x[0], x[1]), op(x[2], x[3]))
        rows //= 4
    return reduce_rows(x, axis=0, keepdims=True)


def _attn_prompt_kernel(qt_ref, k_ref, vt_ref, o_ref, m_ref, l_ref, acc_ref, s_ref, *, tq, tk):
    i = pl.program_id(1)
    n_full = (i * tq) // tk
    m_ref[...] = jnp.full(m_ref.shape, -jnp.inf, F32)
    l_ref[...] = jnp.zeros(l_ref.shape, F32)
    acc_ref[...] = jnp.zeros(acc_ref.shape, F32)

    def scores(j, hh):
        k = k_ref[pl.ds(pl.multiple_of(j * tk, tk), tk), :]
        s_ref[hh % SCORE_SLOTS] = jnp.dot(k, qt_ref[hh], preferred_element_type=F32)

    def step(j, masked):
        vt = vt_ref[:, pl.ds(pl.multiple_of(j * tk, tk), tk)]
        if masked:
            key = j * tk + lax.broadcasted_iota(jnp.int32, (tk, tq), 0)
            tok = i * tq + lax.broadcasted_iota(jnp.int32, (tk, tq), 1)
            visible = key <= tok
        for hh in range(N_HEADS):
            ahead = hh + SCORE_AHEAD
            if ahead < N_HEADS:
                scores(j, ahead)
            elif not masked:
                scores(j + 1, ahead - N_HEADS)
            s = s_ref[hh % SCORE_SLOTS]
            if masked:
                s = jnp.where(visible, s, -jnp.inf)
            m_prev = m_ref[hh]
            m_new = jnp.maximum(m_prev, _col_reduce(s, jnp.maximum, jnp.max))
            alpha = jnp.exp2(m_prev - m_new)
            p = jnp.exp2(s - m_new)
            l_ref[hh] = alpha * l_ref[hh] + _col_reduce(p, jnp.add, jnp.sum)
            pv = jnp.dot(vt, p.astype(BF16), preferred_element_type=F32)
            acc_ref[hh] = alpha * acc_ref[hh] + pv
            m_ref[hh] = m_new

    def body(j, carry):
        step(j, False)
        return carry

    for hh in range(SCORE_AHEAD):
        scores(0, hh)
    lax.fori_loop(0, n_full, body, 0)
    step(n_full, True)
    for hh in range(N_HEADS):
        o_t = acc_ref[hh] / l_ref[hh]
        o_ref[:, hh * KV_LORA_RANK:(hh + 1) * KV_LORA_RANK] = o_t.T.astype(o_ref.dtype)


def _attn_prompt(qt, kvcat, ckv_t, *, batch, seq, tq, tk):
    assert tk % tq == 0 and seq % tk == 0, (seq, tq, tk)
    nq = seq // tq
    n = batch * seq
    return pl.pallas_call(
        functools.partial(_attn_prompt_kernel, tq=tq, tk=tk),
        grid=(batch, nq),
        in_specs=[pl.BlockSpec((N_HEADS, QK_CAT, tq), lambda b, i: (0, 0, b * nq + i)),
                  pl.BlockSpec((seq, QK_CAT), lambda b, i: (b, 0)),
                  pl.BlockSpec((KV_LORA_RANK, seq), lambda b, i: (0, b))],
        out_specs=pl.BlockSpec((tq, N_HEADS * KV_LORA_RANK), lambda b, i: (b * nq + i, 0)),
        out_shape=jax.ShapeDtypeStruct((n, N_HEADS * KV_LORA_RANK), BF16),
        scratch_shapes=[pltpu.VMEM((N_HEADS, 1, tq), F32), pltpu.VMEM((N_HEADS, 1, tq), F32),
                        pltpu.VMEM((N_HEADS, KV_LORA_RANK, tq), F32), pltpu.VMEM((SCORE_SLOTS, tk, tq), F32)],
        compiler_params=_params("arbitrary", "arbitrary"),
        name="attn_prompt",
    )(qt, kvcat, ckv_t)


def _attn_sample_kernel(pt_ref, q_ref, knew_ref, cache_c_ref, cache_rt_ref, o_ref, cbuf, rbuf, sem,
                        *, layer, n_pages, page, t_new):
    b = pl.program_id(0)
    nb = pl.num_programs(0)

    def page_copies(seq_idx, slot):
        copies = []
        for p in range(n_pages):
            pid = pt_ref[seq_idx, p]
            copies.append(pltpu.make_async_copy(
                cache_c_ref.at[layer, pid], cbuf.at[slot, pl.ds(p * page, page), :], sem.at[slot, 0]))
            copies.append(pltpu.make_async_copy(
                cache_rt_ref.at[layer, pid], rbuf.at[slot, :, pl.ds(p * page, page)], sem.at[slot, 1]))
        return copies

    slot = b % 2

    @pl.when(b == 0)
    def _():
        for c in page_copies(0, 0):
            c.start()

    @pl.when(b + 1 < nb)
    def _():
        for c in page_copies(b + 1, 1 - slot):
            c.start()

    for c in page_copies(b, slot):
        c.wait()

    rows = N_HEADS * t_new
    q = q_ref[...].reshape(rows, QK_CAT).astype(BF16)
    kc = cbuf[slot].astype(BF16)
    krt = rbuf[slot].astype(BF16)
    kn = knew_ref[...].astype(BF16)
    s = (lax.dot_general(q[:, :KV_LORA_RANK], kc, _NT, preferred_element_type=F32)
         + jnp.dot(q[:, KV_LORA_RANK:], krt, preferred_element_type=F32))
    sn = lax.dot_general(q, kn, _NT, preferred_element_type=F32)
    tok = lax.broadcasted_iota(jnp.int32, (N_HEADS, t_new, t_new), 1).reshape(rows, t_new)
    key = lax.broadcasted_iota(jnp.int32, (rows, t_new), 1)
    sn = jnp.where(key <= tok, sn, -jnp.inf)
    m = jnp.maximum(jnp.max(s, axis=-1, keepdims=True), jnp.max(sn, axis=-1, keepdims=True))
    p = jnp.exp2(s - m)
    pn = jnp.exp2(sn - m)
    l = jnp.sum(p, axis=-1, keepdims=True) + jnp.sum(pn, axis=-1, keepdims=True)
    o = (jnp.dot(p.astype(BF16), kc, preferred_element_type=F32)
         + jnp.dot(pn.astype(BF16), kn[:, :KV_LORA_RANK], preferred_element_type=F32)) / l
    for hh in range(N_HEADS):
        o_ref[:, hh * KV_LORA_RANK:(hh + 1) * KV_LORA_RANK] = o[hh * t_new:(hh + 1) * t_new]


def _attn_sample(page_table, q, knew, cache_c, cache_rt, *, layer, t_new):
    n_seq, n_pages = page_table.shape
    page = cache_c.shape[2]
    past = n_pages * page
    grid_spec = pltpu.PrefetchScalarGridSpec(
        num_scalar_prefetch=1,
        grid=(n_seq,),
        in_specs=[pl.BlockSpec((N_HEADS, t_new, QK_CAT), lambda b, pt: (0, b, 0)),
                  pl.BlockSpec((t_new, QK_CAT), lambda b, pt: (b, 0)),
                  pl.BlockSpec(memory_space=pl.ANY),
                  pl.BlockSpec(memory_space=pl.ANY)],
        out_specs=pl.BlockSpec((t_new, N_HEADS * KV_LORA_RANK), lambda b, pt: (b, 0)),
        scratch_shapes=[pltpu.VMEM((2, past, KV_LORA_RANK), F32), pltpu.VMEM((2, QK_ROPE_DIM, past), F32),
                        pltpu.SemaphoreType.DMA((2, 2))],
    )
    return pl.pallas_call(
        functools.partial(_attn_sample_kernel, layer=layer, n_pages=n_pages, page=page, t_new=t_new),
        grid_spec=grid_spec,
        out_shape=jax.ShapeDtypeStruct((n_seq * t_new, N_HEADS * KV_LORA_RANK), F32),
        compiler_params=_params("arbitrary"),
        name="attn_sample",
    )(page_table, q, knew, cache_c, cache_rt)


def _router_meta(x1, wr, rb):
    logits = _dot3(wr, x1, _NT)
    score = jax.nn.sigmoid(logits)
    biased = score + rb
    row = lambda a, e: a[e:e + 1, :]
    group_score = []
    for g in range(N_EXPERT_GROUPS):
        v = [row(biased, g * EXPERTS_PER_GROUP + k) for k in range(EXPERTS_PER_GROUP)]
        best = None
        for a in range(EXPERTS_PER_GROUP):
            for c in range(a + 1, EXPERTS_PER_GROUP):
                pair = v[a] + v[c]
                best = pair if best is None else jnp.maximum(best, pair)
        group_score.append(best)
    top = group_score[0]
    grp = jnp.zeros(top.shape, jnp.int32)
    for g in range(1, N_EXPERT_GROUPS):
        better = group_score[g] > top
        grp = jnp.where(better, g, grp)
        top = jnp.maximum(top, group_score[g])

    def pick(a, k):
        out = row(a, (N_EXPERT_GROUPS - 1) * EXPERTS_PER_GROUP + k)
        for g in range(N_EXPERT_GROUPS - 2, -1, -1):
            out = jnp.where(grp == g, row(a, g * EXPERTS_PER_GROUP + k), out)
        return out

    cand = [pick(biased, k) for k in range(EXPERTS_PER_GROUP)]
    aff = [pick(score, k) for k in range(EXPERTS_PER_GROUP)]
    first = jnp.zeros(top.shape, jnp.int32)
    best = cand[0]
    for k in range(1, EXPERTS_PER_GROUP):
        better = cand[k] > best
        first = jnp.where(better, k, first)
        best = jnp.maximum(best, cand[k])
    second = jnp.full(top.shape, -1, jnp.int32)
    best2 = jnp.full(top.shape, -jnp.inf, F32)
    for k in range(EXPERTS_PER_GROUP):
        better = (first != k) & ((second < 0) | (cand[k] > best2))
        second = jnp.where(better, k, second)
        best2 = jnp.where(better, cand[k], best2)
    lo = jnp.minimum(first, second)
    hi = jnp.maximum(first, second)

    def take(vals, k_idx):
        out = vals[EXPERTS_PER_GROUP - 1]
        for k in range(EXPERTS_PER_GROUP - 2, -1, -1):
            out = jnp.where(k_idx == k, vals[k], out)
        return out

    aff_lo, aff_hi = take(aff, lo), take(aff, hi)
    denom = aff_lo + aff_hi
    pair = jnp.where(lo == 0, 0, jnp.where(lo == 1, 3, 5)) + (hi - lo - 1)
    bucket = (grp * PAIRS_PER_GROUP + pair).astype(F32)
    pad = jnp.zeros((META_ROWS - 3,) + top.shape[1:], F32)
    return jnp.concatenate([bucket, aff_lo / denom, aff_hi / denom, pad], axis=0)


def _mix_tail(x, o, pooled, gate_a, gate_b, wva_ref, wmix_ref, ps_ref, wbb_ref, wout_ref, g_ref, b_ref,
              wr_ref, rb_ref, x1_ref, gate_ref, *, alpha):
    gd = wmix_ref.shape[1]
    yb = jnp.concatenate(
        [jnp.dot(pooled[:, g * gd:(g + 1) * gd].astype(BF16), wmix_ref[g], preferred_element_type=F32)
         for g in range(len(POOL_WINDOWS))], axis=1) * ps_ref[...]
    ya = jnp.dot(o.astype(BF16), wva_ref[...], preferred_element_type=F32)
    yb = jnp.dot(yb.astype(BF16), wbb_ref[...], preferred_element_type=F32)
    merged = jax.nn.sigmoid(gate_a) * ya + jax.nn.sigmoid(gate_b) * yb
    mix = jnp.dot(merged.astype(BF16), wout_ref[...], preferred_element_type=F32)
    x1 = _layer_norm(alpha * x + mix, g_ref[...], b_ref[...])
    meta = _router_meta(x1, wr_ref[...], rb_ref[...])
    gate_ref[...] = meta
    d = x1.shape[1]
    x1_ref[:, :d] = x1
    wide = jnp.concatenate([meta[1:3], jnp.zeros((LANES - 2, meta.shape[1]), F32)], axis=0)
    x1_ref[:, d:] = wide.T


def _mix_prompt_kernel(*refs, alpha, tiles_per_seq, n_tiles):
    x1_ref, gate_ref = refs[-4], refs[-3]

    @pl.when(pl.program_id(0) < n_tiles)
    def _():
        _mix_prompt_tile(*refs, alpha=alpha, tiles_per_seq=tiles_per_seq)

    @pl.when(pl.program_id(0) >= n_tiles)
    def _():
        x1_ref[...] = jnp.zeros(x1_ref.shape, F32)
        gate_ref[...] = jnp.zeros(gate_ref.shape, F32)


def _mix_prompt_tile(x_ref, o_ref, w2_ref, wva_ref, wmix_ref, ps_ref, wbb_ref, wout_ref, g_ref, b_ref,
                     wr_ref, rb_ref, x1_ref, gate_ref, tail_ref, ext_ref, *, alpha, tiles_per_seq):
    tm = x_ref.shape[0]
    pw = ps_ref.shape[1]
    gd = pw // len(POOL_WINDOWS)
    x = x_ref[...]
    h2 = jnp.dot(x.astype(BF16), w2_ref[...], preferred_element_type=F32)
    u = h2[:, :pw]
    t = pl.program_id(0) % tiles_per_seq

    @pl.when(t == 0)
    def _():
        ext_ref[0:POOL_PAD, :] = jnp.zeros((POOL_PAD, pw), F32)

    @pl.when(t != 0)
    def _():
        ext_ref[0:POOL_PAD, :] = ext_ref[tm:tm + POOL_PAD, :]

    ext_ref[POOL_PAD:POOL_PAD + tm, :] = u
    tail_ref[0] = u[tm - POOL_PAD:, :]
    pos = t * tm + lax.broadcasted_iota(jnp.int32, (tm, 1), 0)
    pooled = []
    for g, w in enumerate(POOL_WINDOWS):
        acc = ext_ref[POOL_PAD:POOL_PAD + tm, g * gd:(g + 1) * gd]
        for j in range(1, w):
            acc = acc + ext_ref[POOL_PAD - j:POOL_PAD - j + tm, g * gd:(g + 1) * gd]
        cnt = jnp.minimum(w, pos + 1).astype(F32)
        pooled.append(acc / cnt - u[:, g * gd:(g + 1) * gd])
    pooled = jnp.concatenate(pooled, axis=1)
    d = x.shape[1]
    _mix_tail(x, o_ref[...], pooled, h2[:, pw:pw + d], h2[:, pw + d:], wva_ref, wmix_ref, ps_ref, wbb_ref,
              wout_ref, g_ref, b_ref, wr_ref, rb_ref, x1_ref, gate_ref, alpha=alpha)


def _mix_sample_kernel(x_ref, o_ref, pre_ref, x1_all_ref, meta_all_ref, w2_ref, wva_ref, wmix_ref, ps_ref, wbb_ref,
                       wout_ref, g_ref, b_ref, wr_ref, rb_ref, x1_ref, gate_ref, tail_ref, ext_ref,
                       *, alpha, t_new, past):
    del x1_all_ref, meta_all_ref
    tm = x_ref.shape[0]
    sb = tm // t_new
    pw = ps_ref.shape[1]
    gd = pw // len(POOL_WINDOWS)
    x = x_ref[...]
    h2 = jnp.dot(x.astype(BF16), w2_ref[...], preferred_element_type=F32)
    u = h2[:, :pw]
    ext_ref[:, 0:POOL_PAD, :] = pre_ref[...]
    ext_ref[:, POOL_PAD:POOL_PAD + t_new, :] = u.reshape(sb, t_new, pw)
    tail_ref[...] = ext_ref[:, t_new:t_new + POOL_PAD, :]
    pos = past + lax.broadcasted_iota(jnp.int32, (sb, t_new, 1), 1)
    pooled = []
    for g, w in enumerate(POOL_WINDOWS):
        acc = ext_ref[:, POOL_PAD:POOL_PAD + t_new, g * gd:(g + 1) * gd]
        for j in range(1, w):
            acc = acc + ext_ref[:, POOL_PAD - j:POOL_PAD - j + t_new, g * gd:(g + 1) * gd]
        cnt = jnp.minimum(w, pos + 1).astype(F32)
        pooled.append((acc / cnt).reshape(tm, gd) - u[:, g * gd:(g + 1) * gd])
    pooled = jnp.concatenate(pooled, axis=1)
    d = x.shape[1]
    _mix_tail(x, o_ref[...], pooled, h2[:, pw:pw + d], h2[:, pw + d:], wva_ref, wmix_ref, ps_ref, wbb_ref,
              wout_ref, g_ref, b_ref, wr_ref, rb_ref, x1_ref, gate_ref, alpha=alpha)


def _mix_prompt(x, x_off, o, weights, *, n_all, tm, alpha, seq):
    n = o.shape[0]
    d = x.shape[1]
    pw = weights[3].shape[1]
    tiles_per_seq = seq // tm
    n_tiles = n // tm
    last = n_tiles - 1
    return pl.pallas_call(
        functools.partial(_mix_prompt_kernel, alpha=alpha, tiles_per_seq=tiles_per_seq, n_tiles=n_tiles),
        grid=(n_all // tm,),
        in_specs=[pl.BlockSpec((tm, d), lambda i: (jnp.minimum(i, last) + x_off, 0)),
                  pl.BlockSpec((tm, o.shape[1]), lambda i: (jnp.minimum(i, last), 0))]
        + [_const_spec(a.shape) for a in weights],
        out_specs=[pl.BlockSpec((tm, d + LANES), lambda i: (i, 0)),
                   pl.BlockSpec((META_ROWS, tm), lambda i: (0, i)),
                   pl.BlockSpec((1, POOL_PAD, pw), lambda i: (jnp.minimum(i, last) // tiles_per_seq, 0, 0))],
        out_shape=[jax.ShapeDtypeStruct((n_all, d + LANES), F32), jax.ShapeDtypeStruct((META_ROWS, n_all), F32),
                   jax.ShapeDtypeStruct((n // seq, POOL_PAD, pw), F32)],
        scratch_shapes=[pltpu.VMEM((tm + POOL_PAD, pw), F32)],
        compiler_params=_params("arbitrary"),
        name="mix_prompt",
    )(x, o, *weights)


def _mix_sample(x, x_off, o, prefix, x1_all, meta_all, weights, *, row0, tm, alpha, t_new, past):
    n = o.shape[0]
    d = x.shape[1]
    pw = weights[3].shape[1]
    sb = tm // t_new
    off = row0 // tm
    any_spec = pl.BlockSpec(memory_space=pl.ANY)
    return pl.pallas_call(
        functools.partial(_mix_sample_kernel, alpha=alpha, t_new=t_new, past=past),
        grid=(n // tm,),
        in_specs=[pl.BlockSpec((tm, d), lambda i: (i + x_off, 0)), pl.BlockSpec((tm, o.shape[1]), lambda i: (i, 0)),
                  pl.BlockSpec((sb, POOL_PAD, pw), lambda i: (i, 0, 0)), any_spec, any_spec]
        + [_const_spec(a.shape) for a in weights],
        out_specs=[pl.BlockSpec((tm, d + LANES), lambda i: (i + off, 0)),
                   pl.BlockSpec((META_ROWS, tm), lambda i: (0, i + off)),
                   pl.BlockSpec((sb, POOL_PAD, pw), lambda i: (i, 0, 0))],
        out_shape=[jax.ShapeDtypeStruct(x1_all.shape, F32), jax.ShapeDtypeStruct(meta_all.shape, F32),
                   jax.ShapeDtypeStruct((n // t_new, POOL_PAD, pw), F32)],
        input_output_aliases={3: 0, 4: 1},
        scratch_shapes=[pltpu.VMEM((sb, POOL_PAD + t_new, pw), F32)],
        compiler_params=_params("arbitrary"),
        name="mix_sample",
    )(x, o, prefix, x1_all, meta_all, *weights)


def _moe_kernel(src_ref, dst_ref, ea_ref, eb_ref, used_ref, x1_hbm, wga_ref, wua_ref, wda_ref, wgb_ref,
                wub_ref, wdb_ref, g_ref, b_ref, out_hbm, xbuf, ybuf, gather_sem, scatter_sem, *, alpha, tm):
    del ea_ref, eb_ref
    i = pl.program_id(0)
    used = used_ref[0]
    buf = i % 2
    d = out_hbm.shape[1]

    def gather_row(tile, b, r):
        src = x1_hbm.at[pl.ds(src_ref[tile * tm + r], 1), :]
        return pltpu.make_async_copy(src, xbuf.at[b, pl.ds(r, 1), :], gather_sem.at[b])

    def scatter_row(tile, b, r):
        dst = out_hbm.at[pl.ds(dst_ref[tile * tm + r], 1), :]
        return pltpu.make_async_copy(ybuf.at[b, pl.ds(r, 1), :], dst, scatter_sem.at[b])

    def start_gather(tile, b):
        for r in range(tm):
            gather_row(tile, b, r).start()

    def wait_gather(b):
        for r in range(tm):
            gather_row(0, b, 0).wait()

    def wait_scatter(b):
        for r in range(tm):
            scatter_row(0, b, 0).wait()

    @pl.when(i == 0)
    def _():
        start_gather(0, 0)
        n_rows, n_out = x1_hbm.shape[0], out_hbm.shape[0]
        ybuf[1] = jnp.zeros(ybuf.shape[1:], F32)
        fills = [pltpu.make_async_copy(ybuf.at[1], out_hbm.at[pl.ds(row, tm), :], scatter_sem.at[1])
                 for row in range(n_rows, n_out, tm)]
        for c in fills:
            c.start()
        for c in fills:
            c.wait()

    for b in range(2):
        @pl.when((i < used) & (buf == b))
        def _():
            @pl.when(i + 1 < used)
            def _():
                start_gather(i + 1, 1 - b)

            wait_gather(b)

            @pl.when(i >= 2)
            def _():
                wait_scatter(b)

    @pl.when(i < used)
    def _():
        row = xbuf[buf]
        x = row[:, :d]
        wt = row[:, d:]
        xb = x.astype(BF16)

        def expert(wg_ref, wu_ref, wd_ref, col):
            hg = jnp.dot(xb, wg_ref[0], preferred_element_type=F32)
            hu = jnp.dot(xb, wu_ref[0], preferred_element_type=F32)
            hid = (jax.nn.silu(hg) * hu * col).astype(BF16)
            return jnp.dot(hid, wd_ref[0], preferred_element_type=F32)

        y = expert(wga_ref, wua_ref, wda_ref, wt[:, 0:1]) + expert(wgb_ref, wub_ref, wdb_ref, wt[:, 1:2])
        ybuf[buf] = _layer_norm(alpha * x + y, g_ref[...], b_ref[...])

    for b in range(2):
        @pl.when((i < used) & (buf == b))
        def _():
            for r in range(tm):
                scatter_row(i, b, r).start()

    @pl.when(i == pl.num_programs(0) - 1)
    def _():
        wait_scatter((used - 1) % 2)

        @pl.when(used >= 2)
        def _():
            wait_scatter(used % 2)


def _moe(x1, row_src, row_dst, tile_ea, tile_eb, n_used, wg, wu, wd, g2, b2, *, tm, alpha):
    p = row_src.shape[0]
    d = g2.shape[1]
    f = wg.shape[2]
    first = lambda i, src, dst, ea, eb, used: (ea[i], 0, 0)
    second = lambda i, src, dst, ea, eb, used: (eb[i], 0, 0)
    const = lambda shape: pl.BlockSpec(shape, lambda i, *pre: (0,) * len(shape), pipeline_mode=pl.Buffered(1))
    grid_spec = pltpu.PrefetchScalarGridSpec(
        num_scalar_prefetch=5,
        grid=(p // tm,),
        in_specs=[pl.BlockSpec(memory_space=pl.ANY),
                  pl.BlockSpec((1, d, f), first), pl.BlockSpec((1, d, f), first), pl.BlockSpec((1, f, d), first),
                  pl.BlockSpec((1, d, f), second), pl.BlockSpec((1, d, f), second), pl.BlockSpec((1, f, d), second),
                  const(g2.shape), const(b2.shape)],
        out_specs=pl.BlockSpec(memory_space=pl.ANY),
        scratch_shapes=[pltpu.VMEM((2, tm, x1.shape[1]), F32), pltpu.VMEM((2, tm, d), F32),
                        pltpu.SemaphoreType.DMA((2,)), pltpu.SemaphoreType.DMA((2,))],
    )
    return pl.pallas_call(
        functools.partial(_moe_kernel, alpha=alpha, tm=tm),
        grid_spec=grid_spec,
        out_shape=jax.ShapeDtypeStruct((p, d), F32),
        compiler_params=_params("arbitrary"),
        name="moe",
    )(row_src, row_dst, tile_ea, tile_eb, n_used, x1, wg, wu, wd, wg, wu, wd, g2, b2)


def _routing_plan(meta, *, tm):
    n = meta.shape[1]
    n_tiles = (n + N_BUCKETS * (tm - 1) + tm - 1) // tm
    p = n_tiles * tm
    bucket = meta[0].astype(jnp.int32)
    onehot = (bucket[:, None] == jnp.arange(N_BUCKETS, dtype=jnp.int32)[None, :]).astype(jnp.int32)
    csum = jnp.cumsum(onehot, axis=0)
    rank = jnp.sum(csum * onehot, axis=1) - 1
    tiles_per_bucket = (csum[-1] + tm - 1) // tm
    tile_end = jnp.cumsum(tiles_per_bucket)
    n_used = tile_end[-1:]
    row_start = (tile_end - tiles_per_bucket) * tm
    slot = jnp.sum(onehot * row_start[None, :], axis=1) + rank
    token = jnp.full((p,), -1, jnp.int32).at[slot].set(jnp.arange(n, dtype=jnp.int32), unique_indices=True)
    is_pad = token < 0
    row_src = jnp.where(is_pad, 0, token)
    row_dst = jnp.where(is_pad, n - 1 + jnp.cumsum(is_pad.astype(jnp.int32)), token)
    tile_id = jnp.minimum(jnp.arange(n_tiles, dtype=jnp.int32), n_used - 1)
    tile_bucket = jnp.minimum(jnp.sum((tile_id[:, None] >= tile_end[None, :]).astype(jnp.int32), axis=1),
                              N_BUCKETS - 1)
    group, pair = tile_bucket // PAIRS_PER_GROUP, tile_bucket % PAIRS_PER_GROUP
    lo = sum((pair == k).astype(jnp.int32) * PAIR_LO[k] for k in range(PAIRS_PER_GROUP))
    hi = sum((pair == k).astype(jnp.int32) * PAIR_HI[k] for k in range(PAIRS_PER_GROUP))
    tile_ea = group * EXPERTS_PER_GROUP + lo
    tile_eb = group * EXPERTS_PER_GROUP + hi
    return row_src, row_dst, tile_ea, tile_eb, n_used.astype(jnp.int32)


class _Tiles(NamedTuple):
    proj: int
    attn_q: int
    attn_k: int
    mix: int
    sample: int
    moe: int


def _tile_plan(prompt_len, n_sample):
    big = min(512, prompt_len)
    return _Tiles(proj=big, attn_q=big, attn_k=big, mix=big, sample=min(256, n_sample), moe=256)


def _rope_tables(pos, reps):
    half = QK_ROPE_DIM // 2
    inv = ROPE_BASE ** (-jnp.arange(half, dtype=F32) / half)
    ang = pos.astype(F32)[:, None] * inv
    cos, sin = jnp.cos(ang), jnp.sin(ang)
    return (jnp.tile(jnp.concatenate([cos, cos], axis=1), (1, reps)),
            jnp.tile(jnp.concatenate([-sin, sin], axis=1), (1, reps)))


def kernel(x_prompt, x_sample, cache_kv_latent, cache_k_rope, state_pool, page_table, w_in, g_q_norm, g_kv_norm,
           w_q_up, w_k_up, w_v_up, w_pool_mix, pool_scale, w_branch_a, w_branch_b, w_out, ln1_g, ln1_b,
           w_router, router_bias, w_exp_gate, w_exp_up, w_exp_down, ln2_g, ln2_b):
    bp, tp, d = x_prompt.shape
    bs, ts, _ = x_sample.shape
    depth = w_in.shape[0]
    n_pages = page_table.shape[1]
    past = n_pages * cache_kv_latent.shape[2]
    alpha = (2 * depth) ** 0.25
    pw = pool_scale.shape[1]
    n_p, n_s = bp * tp, bs * ts
    s1 = Q_LORA_RANK + KV_LORA_RANK + QK_ROPE_DIM

    tiles = _tile_plan(tp, n_s)
    n_all = n_p + n_s

    cos_p, sin_p = _rope_tables(jnp.arange(tp, dtype=jnp.int32), N_HEADS)
    cos_pt, sin_pt = cos_p.T, sin_p.T
    cos_p, sin_p = cos_p[:, :QK_ROPE_DIM], sin_p[:, :QK_ROPE_DIM]
    cos_s, sin_s = _rope_tables(past + jnp.arange(ts, dtype=jnp.int32), N_HEADS)
    cos_s, sin_s = jnp.tile(cos_s, (tiles.sample // ts, 1)), jnp.tile(sin_s, (tiles.sample // ts, 1))
    prefix = jnp.pad(state_pool, ((0, 0), (0, 0), (POOL_PAD - state_pool.shape[2], 0), (0, 0)))
    cache_rt = jnp.swapaxes(cache_k_rope, 2, 3)
    wr_t = w_router.T
    rb = router_bias.reshape(N_EXPERTS, 1)

    x_p, x_s, x_s_off = x_prompt.reshape(n_p, d), x_sample.reshape(n_s, d), 0
    outs = [[] for _ in range(6)]
    for l in range(depth):
        wq = w_q_up[l]
        wq_nope = jnp.transpose(wq[:, :, :QK_NOPE_DIM], (1, 0, 2))
        wqr = (wq[:, :, QK_NOPE_DIM:] * Q_SCALE).reshape(Q_LORA_RANK, N_HEADS * QK_ROPE_DIM).astype(BF16)
        wk = jnp.transpose(w_k_up[l], (1, 0, 2))
        wv = jnp.transpose(w_v_up[l], (1, 0, 2))
        wba = w_branch_a[l].reshape(N_HEADS, -1, d)
        wlat, wlat_t, wva = _fold_weights(wq_nope, wk, wv, wba)
        w1 = w_in[l, :, :s1].astype(BF16)
        w2 = w_in[l, :, s1:].astype(BF16)
        gq = g_q_norm[l].reshape(1, -1)
        gkv = g_kv_norm[l].reshape(1, -1)
        mix_w = (w2, wva, w_pool_mix[l].astype(BF16), pool_scale[l].reshape(1, pw), w_branch_b[l].astype(BF16),
                 w_out[l].astype(BF16), ln1_g[l].reshape(1, d), ln1_b[l].reshape(1, d), wr_t, rb)
        wg, wu, wd = w_exp_gate[l].astype(BF16), w_exp_up[l].astype(BF16), w_exp_down[l].astype(BF16)
        g2, b2 = ln2_g[l].reshape(1, d), ln2_b[l].reshape(1, d)

        qt, kvcat, ckv_t, ckv, krope = _proj_prompt(x_p, n_p, w1, wlat_t, wqr.T, gq, gkv, cos_p, sin_p, cos_pt,
                                                    sin_pt, tm=tiles.proj)
        o = _attn_prompt(qt, kvcat, ckv_t, batch=bp, seq=tp, tq=tiles.attn_q, tk=tiles.attn_k)
        x1, meta, tail = _mix_prompt(x_p, 0, o, mix_w, n_all=n_all, tm=tiles.mix, alpha=alpha, seq=tp)
        outs[0].append(ckv.reshape(bp, tp, -1))
        outs[1].append(krope.reshape(bp, tp, -1))
        outs[2].append(tail[:, 1:, :])

        q, kvcat, ckv, krope = _proj_sample(x_s, x_s_off, n_s, w1, wlat, wqr, gq, gkv, cos_s, sin_s, tm=tiles.sample)
        o = _attn_sample(page_table, q, kvcat, cache_kv_latent, cache_rt, layer=l, t_new=ts)
        x1, meta, tail = _mix_sample(x_s, x_s_off, o, prefix[l], x1, meta, mix_w, row0=n_p, tm=tiles.sample, alpha=alpha,
                                     t_new=ts, past=past)
        outs[3].append(ckv.reshape(bs, ts, -1))
        outs[4].append(krope.reshape(bs, ts, -1))
        outs[5].append(tail[:, 1:, :])

        row_src, row_dst, tile_ea, tile_eb, n_used = _routing_plan(meta, tm=tiles.moe)
        x_all = _moe(x1, row_src, row_dst, tile_ea, tile_eb, n_used, wg, wu, wd, g2, b2, tm=tiles.moe, alpha=alpha)
        x_p, x_s, x_s_off = x_all, x_all, n_p // tiles.sample

    return (x_all[:n_p].reshape(bp, tp, d), x_all[n_p:n_all].reshape(bs, ts, d), jnp.stack(outs[0]),
            jnp.stack(outs[1]), jnp.stack(outs[2]), jnp.stack(outs[3]), jnp.stack(outs[4]), jnp.stack(outs[5]))
```

```python
import functools
from typing import NamedTuple

import jax
import jax.numpy as jnp
from jax import lax
from jax.experimental import pallas as pl
from jax.experimental.pallas import tpu as pltpu

F32 = jnp.float32
BF16 = jnp.bfloat16

N_HEADS = 8
QK_NOPE_DIM = 64
QK_ROPE_DIM = 32
Q_LORA_RANK = 256
KV_LORA_RANK = 128
QK_CAT = KV_LORA_RANK + QK_ROPE_DIM
ROPE_BASE = 10000.0
SM_SCALE = (QK_NOPE_DIM + QK_ROPE_DIM) ** -0.5
LOG2_E = 1.4426950408889634
Q_SCALE = SM_SCALE * LOG2_E
POOL_WINDOWS = (2, 4, 8, 16)
POOL_PAD = 16
N_EXPERTS = 16
N_EXPERT_GROUPS = 4
EXPERTS_PER_GROUP = 4
PAIRS_PER_GROUP = 6
PAIR_LO = (0, 0, 0, 1, 1, 2)
PAIR_HI = (1, 2, 3, 2, 3, 3)
N_BUCKETS = N_EXPERT_GROUPS * PAIRS_PER_GROUP
META_ROWS = 8
LN_EPS = 1e-5
RMS_EPS = 1e-6

LANES = 128
VMEM_LIMIT_BYTES = 56 * 1024 * 1024


def _params(*semantics):
    return pltpu.CompilerParams(dimension_semantics=semantics, vmem_limit_bytes=VMEM_LIMIT_BYTES)


def _const_spec(shape):
    zeros = (0,) * len(shape)
    return pl.BlockSpec(shape, lambda *_: zeros, pipeline_mode=pl.Buffered(1))


def _split_bf16(a):
    hi = a.astype(BF16)
    lo = (a - hi.astype(F32)).astype(BF16)
    return hi, lo


def _dot3(a, b, dims):
    a_hi, a_lo = _split_bf16(a)
    b_hi, b_lo = _split_bf16(b)
    d = functools.partial(lax.dot_general, dimension_numbers=dims, preferred_element_type=F32)
    return d(a_hi, b_hi) + (d(a_hi, b_lo) + d(a_lo, b_hi))


_NN = (((1,), (0,)), ((), ()))
_NT = (((1,), (1,)), ((), ()))


def _layer_norm(y, g, b):
    mu = jnp.mean(y, axis=-1, keepdims=True)
    d = y - mu
    var = jnp.mean(d * d, axis=-1, keepdims=True)
    return d * lax.rsqrt(var + LN_EPS) * g + b


def _rms_norm(y, g):
    return y * lax.rsqrt(jnp.mean(y * y, axis=-1, keepdims=True) + RMS_EPS) * g


def _fold_kernel(wqn_ref, wqn_t_ref, wk_ref, wk_t_ref, wv_ref, wba_ref, wlat_ref, wlat_t_ref, wva_ref):
    wlat_ref[...] = (_dot3(wqn_ref[0], wk_t_ref[0], _NN) * Q_SCALE).astype(BF16)
    wlat_t_ref[...] = (_dot3(wk_ref[0], wqn_t_ref[0], _NN) * Q_SCALE).astype(BF16)
    wva_ref[...] = _dot3(wv_ref[0], wba_ref[0], _NN).astype(BF16)


def _fold_weights(wq_nope, wk, wv, wba):
    h, r, dn = wq_nope.shape
    d = wba.shape[-1]
    head = lambda *blk: pl.BlockSpec((1,) + blk, lambda i: (i, 0, 0))
    return pl.pallas_call(
        _fold_kernel,
        grid=(h,),
        in_specs=[head(r, dn), head(dn, r), head(KV_LORA_RANK, dn), head(dn, KV_LORA_RANK),
                  head(KV_LORA_RANK, wv.shape[-1]), head(wba.shape[1], d)],
        out_specs=[pl.BlockSpec((r, KV_LORA_RANK), lambda i: (0, i)),
                   pl.BlockSpec((KV_LORA_RANK, r), lambda i: (i, 0)),
                   pl.BlockSpec((KV_LORA_RANK, d), lambda i: (i, 0))],
        out_shape=[jax.ShapeDtypeStruct((r, h * KV_LORA_RANK), BF16),
                   jax.ShapeDtypeStruct((h * KV_LORA_RANK, r), BF16),
                   jax.ShapeDtypeStruct((h * KV_LORA_RANK, d), BF16)],
        compiler_params=_params("arbitrary"),
        name="fold",
    )(wq_nope, jnp.swapaxes(wq_nope, 1, 2), wk, jnp.swapaxes(wk, 1, 2), wv, wba)


def _proj_common(x_ref, w1_ref, gq_ref, gkv_ref, cos_ref, sin_ref, ckv_ref, krope_ref):
    x = x_ref[...].astype(BF16)
    h = jnp.dot(x, w1_ref[...], preferred_element_type=F32)
    qa = h[:, :Q_LORA_RANK]
    kva = h[:, Q_LORA_RANK:Q_LORA_RANK + KV_LORA_RANK]
    kr = h[:, Q_LORA_RANK + KV_LORA_RANK:]
    qn = _rms_norm(qa, gq_ref[...])
    ckv = _rms_norm(kva, gkv_ref[...])
    half = QK_ROPE_DIM // 2
    kr_swapped = jnp.concatenate([kr[:, half:], kr[:, :half]], axis=1)
    krope = kr * cos_ref[:, :QK_ROPE_DIM] + kr_swapped * sin_ref[:, :QK_ROPE_DIM]
    ckv_ref[...] = ckv
    krope_ref[...] = krope
    return qn, ckv, krope


def _proj_sample_kernel(x_ref, w1_ref, wlat_ref, wqr_ref, gq_ref, gkv_ref, cos_ref, sin_ref,
                        q_ref, kvcat_ref, ckv_ref, krope_ref):
    qn, ckv, krope = _proj_common(x_ref, w1_ref, gq_ref, gkv_ref, cos_ref, sin_ref, ckv_ref, krope_ref)
    qn = qn.astype(BF16)
    ql = jnp.dot(qn, wlat_ref[...], preferred_element_type=F32)
    qr = jnp.dot(qn, wqr_ref[...], preferred_element_type=F32)
    half = QK_ROPE_DIM // 2
    width = qr.shape[1]
    lane = lax.broadcasted_iota(jnp.int32, qr.shape, 1)
    swapped = jnp.where((lane % QK_ROPE_DIM) < half,
                        pltpu.roll(qr, width - half, 1), pltpu.roll(qr, half, 1))
    qrr = qr * cos_ref[...] + swapped * sin_ref[...]
    for hh in range(N_HEADS):
        q_ref[hh, :, :KV_LORA_RANK] = ql[:, hh * KV_LORA_RANK:(hh + 1) * KV_LORA_RANK]
        q_ref[hh, :, KV_LORA_RANK:] = qrr[:, hh * QK_ROPE_DIM:(hh + 1) * QK_ROPE_DIM]
    kvcat_ref[:, :KV_LORA_RANK] = ckv
    kvcat_ref[:, KV_LORA_RANK:] = krope


def _proj_prompt_kernel(x_ref, w1_ref, wlat_t_ref, wqr_t_ref, gq_ref, gkv_ref, cos_ref, sin_ref,
                        cos_t_ref, sin_t_ref, qt_ref, kvcat_ref, ckv_t_ref, ckv_ref, krope_ref):
    qn, ckv, krope = _proj_common(x_ref, w1_ref, gq_ref, gkv_ref, cos_ref, sin_ref, ckv_ref, krope_ref)
    qn_t = qn.T.astype(BF16)
    ql_t = jnp.dot(wlat_t_ref[...], qn_t, preferred_element_type=F32)
    qr_t = jnp.dot(wqr_t_ref[...], qn_t, preferred_element_type=F32)
    half = QK_ROPE_DIM // 2
    pieces = []
    for hh in range(N_HEADS):
        base = hh * QK_ROPE_DIM
        pieces += [qr_t[base + half:base + QK_ROPE_DIM], qr_t[base:base + half]]
    qrr_t = qr_t * cos_t_ref[...] + jnp.concatenate(pieces, axis=0) * sin_t_ref[...]
    for hh in range(N_HEADS):
        qt_ref[hh, :KV_LORA_RANK, :] = ql_t[hh * KV_LORA_RANK:(hh + 1) * KV_LORA_RANK].astype(BF16)
        qt_ref[hh, KV_LORA_RANK:, :] = qrr_t[hh * QK_ROPE_DIM:(hh + 1) * QK_ROPE_DIM].astype(BF16)
    kvcat_ref[:, :KV_LORA_RANK] = ckv.astype(BF16)
    kvcat_ref[:, KV_LORA_RANK:] = krope.astype(BF16)
    ckv_t_ref[...] = ckv.T.astype(BF16)


def _proj_sample(x, x_off, n, w1, wlat, wqr, gq, gkv, cos_t, sin_t, *, tm):
    d = x.shape[1]
    row = lambda width: pl.BlockSpec((tm, width), lambda i: (i, 0))
    return pl.pallas_call(
        _proj_sample_kernel,
        grid=(n // tm,),
        in_specs=[pl.BlockSpec((tm, d), lambda i: (i + x_off, 0)), _const_spec(w1.shape), _const_spec(wlat.shape), _const_spec(wqr.shape),
                  _const_spec(gq.shape), _const_spec(gkv.shape),
                  _const_spec(cos_t.shape), _const_spec(sin_t.shape)],
        out_specs=[pl.BlockSpec((N_HEADS, tm, QK_CAT), lambda i: (0, i, 0)),
                   row(QK_CAT), row(KV_LORA_RANK), row(QK_ROPE_DIM)],
        out_shape=[jax.ShapeDtypeStruct((N_HEADS, n, QK_CAT), F32),
                   jax.ShapeDtypeStruct((n, QK_CAT), F32),
                   jax.ShapeDtypeStruct((n, KV_LORA_RANK), F32),
                   jax.ShapeDtypeStruct((n, QK_ROPE_DIM), F32)],
        compiler_params=_params("arbitrary"),
        name="proj_sample",
    )(x, w1, wlat, wqr, gq, gkv, cos_t, sin_t)


def _proj_prompt(x, n, w1, wlat_t, wqr_t, gq, gkv, cos, sin, cos_t, sin_t, *, tm):
    d = x.shape[1]
    tiles = cos.shape[0] // tm
    row = lambda width: pl.BlockSpec((tm, width), lambda i: (i, 0))
    return pl.pallas_call(
        _proj_prompt_kernel,
        grid=(n // tm,),
        in_specs=[row(d), _const_spec(w1.shape), _const_spec(wlat_t.shape), _const_spec(wqr_t.shape),
                  _const_spec(gq.shape), _const_spec(gkv.shape),
                  pl.BlockSpec((tm, cos.shape[1]), lambda i: (i % tiles, 0)),
                  pl.BlockSpec((tm, sin.shape[1]), lambda i: (i % tiles, 0)),
                  pl.BlockSpec((cos_t.shape[0], tm), lambda i: (0, i % tiles)),
                  pl.BlockSpec((sin_t.shape[0], tm), lambda i: (0, i % tiles))],
        out_specs=[pl.BlockSpec((N_HEADS, QK_CAT, tm), lambda i: (0, 0, i)),
                   row(QK_CAT),
                   pl.BlockSpec((KV_LORA_RANK, tm), lambda i: (0, i)),
                   row(KV_LORA_RANK), row(QK_ROPE_DIM)],
        out_shape=[jax.ShapeDtypeStruct((N_HEADS, QK_CAT, n), BF16),
                   jax.ShapeDtypeStruct((n, QK_CAT), BF16),
                   jax.ShapeDtypeStruct((KV_LORA_RANK, n), BF16),
                   jax.ShapeDtypeStruct((n, KV_LORA_RANK), F32),
                   jax.ShapeDtypeStruct((n, QK_ROPE_DIM), F32)],
        compiler_params=_params("arbitrary"),
        name="proj_prompt",
    )(x, w1, wlat_t, wqr_t, gq, gkv, cos, sin, cos_t, sin_t)


SCORE_AHEAD = 2
SCORE_SLOTS = 4


def _col_reduce(x, op, reduce_rows):
    rows = x.shape[0]
    while rows >= 32:
        x = x.reshape(4, rows // 4, x.shape[1])
        x = op(op(x[0], x[1]), op(x[2], x[3]))
        rows //= 4
    return reduce_rows(x, axis=0, keepdims=True)


def _attn_prompt_kernel(qt_ref, k_ref, vt_ref, o_ref, m_ref, l_ref, acc_ref, s_ref, *, tq, tk):
    i = pl.program_id(1)
    n_full = (i * tq) // tk
    m_ref[...] = jnp.full(m_ref.shape, -jnp.inf, F32)
    l_ref[...] = jnp.zeros(l_ref.shape, F32)
    acc_ref[...] = jnp.zeros(acc_ref.shape, F32)

    def scores(j, hh):
        k = k_ref[pl.ds(pl.multiple_of(j * tk, tk), tk), :]
        s_ref[hh % SCORE_SLOTS] = jnp.dot(k, qt_ref[hh], preferred_element_type=F32)

    def step(j, masked):
        vt = vt_ref[:, pl.ds(pl.multiple_of(j * tk, tk), tk)]
        if masked:
            key = j * tk + lax.broadcasted_iota(jnp.int32, (tk, tq), 0)
            tok = i * tq + lax.broadcasted_iota(jnp.int32, (tk, tq), 1)
            visible = key <= tok
        for hh in range(N_HEADS):
            ahead = hh + SCORE_AHEAD
            if ahead < N_HEADS:
                scores(j, ahead)
            elif not masked:
                scores(j + 1, ahead - N_HEADS)
            s = s_ref[hh % SCORE_SLOTS]
            if masked:
                s = jnp.where(visible, s, -jnp.inf)
            m_prev = m_ref[hh]
            m_new = jnp.maximum(m_prev, _col_reduce(s, jnp.maximum, jnp.max))
            alpha = jnp.exp2(m_prev - m_new)
            p = jnp.exp2(s - m_new)
            l_ref[hh] = alpha * l_ref[hh] + _col_reduce(p, jnp.add, jnp.sum)
            pv = jnp.dot(vt, p.astype(BF16), preferred_element_type=F32)
            acc_ref[hh] = alpha * acc_ref[hh] + pv
            m_ref[hh] = m_new

    def body(j, carry):
        step(j, False)
        return carry

    for hh in range(SCORE_AHEAD):
        scores(0, hh)
    lax.fori_loop(0, n_full, body, 0)
    step(n_full, True)
    for hh in range(N_HEADS):
        o_t = acc_ref[hh] / l_ref[hh]
        o_ref[:, hh * KV_LORA_RANK:(hh + 1) * KV_LORA_RANK] = o_t.T.astype(o_ref.dtype)


def _attn_prompt(qt, kvcat, ckv_t, *, batch, seq, tq, tk):
    assert tk % tq == 0 and seq % tk == 0, (seq, tq, tk)
    nq = seq // tq
    n = batch * seq
    return pl.pallas_call(
        functools.partial(_attn_prompt_kernel, tq=tq, tk=tk),
        grid=(batch, nq),
        in_specs=[pl.BlockSpec((N_HEADS, QK_CAT, tq), lambda b, i: (0, 0, b * nq + i)),
                  pl.BlockSpec((seq, QK_CAT), lambda b, i: (b, 0)),
                  pl.BlockSpec((KV_LORA_RANK, seq), lambda b, i: (0, b))],
        out_specs=pl.BlockSpec((tq, N_HEADS * KV_LORA_RANK), lambda b, i: (b * nq + i, 0)),
        out_shape=jax.ShapeDtypeStruct((n, N_HEADS * KV_LORA_RANK), BF16),
        scratch_shapes=[pltpu.VMEM((N_HEADS, 1, tq), F32), pltpu.VMEM((N_HEADS, 1, tq), F32),
                        pltpu.VMEM((N_HEADS, KV_LORA_RANK, tq), F32), pltpu.VMEM((SCORE_SLOTS, tk, tq), F32)],
        compiler_params=_params("arbitrary", "arbitrary"),
        name="attn_prompt",
    )(qt, kvcat, ckv_t)


def _attn_sample_kernel(pt_ref, q_ref, knew_ref, cache_c_ref, cache_rt_ref, o_ref, cbuf, rbuf, sem,
                        *, layer, n_pages, page, t_new):
    b = pl.program_id(0)
    nb = pl.num_programs(0)

    def page_copies(seq_idx, slot):
        copies = []
        for p in range(n_pages):
            pid = pt_ref[seq_idx, p]
            copies.append(pltpu.make_async_copy(
                cache_c_ref.at[layer, pid], cbuf.at[slot, pl.ds(p * page, page), :], sem.at[slot, 0]))
            copies.append(pltpu.make_async_copy(
                cache_rt_ref.at[layer, pid], rbuf.at[slot, :, pl.ds(p * page, page)], sem.at[slot, 1]))
        return copies

    slot = b % 2

    @pl.when(b == 0)
    def _():
        for c in page_copies(0, 0):
            c.start()

    @pl.when(b + 1 < nb)
    def _():
        for c in page_copies(b + 1, 1 - slot):
            c.start()

    for c in page_copies(b, slot):
        c.wait()

    rows = N_HEADS * t_new
    q = q_ref[...].reshape(rows, QK_CAT).astype(BF16)
    kc = cbuf[slot].astype(BF16)
    krt = rbuf[slot].astype(BF16)
    kn = knew_ref[...].astype(BF16)
    s = (lax.dot_general(q[:, :KV_LORA_RANK], kc, _NT, preferred_element_type=F32)
         + jnp.dot(q[:, KV_LORA_RANK:], krt, preferred_element_type=F32))
    sn = lax.dot_general(q, kn, _NT, preferred_element_type=F32)
    tok = lax.broadcasted_iota(jnp.int32, (N_HEADS, t_new, t_new), 1).reshape(rows, t_new)
    key = lax.broadcasted_iota(jnp.int32, (rows, t_new), 1)
    sn = jnp.where(key <= tok, sn, -jnp.inf)
    m = jnp.maximum(jnp.max(s, axis=-1, keepdims=True), jnp.max(sn, axis=-1, keepdims=True))
    p = jnp.exp2(s - m)
    pn = jnp.exp2(sn - m)
    l = jnp.sum(p, axis=-1, keepdims=True) + jnp.sum(pn, axis=-1, keepdims=True)
    o = (jnp.dot(p.astype(BF16), kc, preferred_element_type=F32)
         + jnp.dot(pn.astype(BF16), kn[:, :KV_LORA_RANK], preferred_element_type=F32)) / l
    for hh in range(N_HEADS):
        o_ref[:, hh * KV_LORA_RANK:(hh + 1) * KV_LORA_RANK] = o[hh * t_new:(hh + 1) * t_new]


def _attn_sample(page_table, q, knew, cache_c, cache_rt, *, layer, t_new):
    n_seq, n_pages = page_table.shape
    page = cache_c.shape[2]
    past = n_pages * page
    grid_spec = pltpu.PrefetchScalarGridSpec(
        num_scalar_prefetch=1,
        grid=(n_seq,),
        in_specs=[pl.BlockSpec((N_HEADS, t_new, QK_CAT), lambda b, pt: (0, b, 0)),
                  pl.BlockSpec((t_new, QK_CAT), lambda b, pt: (b, 0)),
                  pl.BlockSpec(memory_space=pl.ANY),
                  pl.BlockSpec(memory_space=pl.ANY)],
        out_specs=pl.BlockSpec((t_new, N_HEADS * KV_LORA_RANK), lambda b, pt: (b, 0)),
        scratch_shapes=[pltpu.VMEM((2, past, KV_LORA_RANK), F32), pltpu.VMEM((2, QK_ROPE_DIM, past), F32),
                        pltpu.SemaphoreType.DMA((2, 2))],
    )
    return pl.pallas_call(
        functools.partial(_attn_sample_kernel, layer=layer, n_pages=n_pages, page=page, t_new=t_new),
        grid_spec=grid_spec,
        out_shape=jax.ShapeDtypeStruct((n_seq * t_new, N_HEADS * KV_LORA_RANK), F32),
        compiler_params=_params("arbitrary"),
        name="attn_sample",
    )(page_table, q, knew, cache_c, cache_rt)


def _router_meta(x1, wr, rb):
    logits = _dot3(wr, x1, _NT)
    score = jax.nn.sigmoid(logits)
    biased = score + rb
    row = lambda a, e: a[e:e + 1, :]
    group_score = []
    for g in range(N_EXPERT_GROUPS):
        v = [row(biased, g * EXPERTS_PER_GROUP + k) for k in range(EXPERTS_PER_GROUP)]
        best = None
        for a in range(EXPERTS_PER_GROUP):
            for c in range(a + 1, EXPERTS_PER_GROUP):
                pair = v[a] + v[c]
                best = pair if best is None else jnp.maximum(best, pair)
        group_score.append(best)
    top = group_score[0]
    grp = jnp.zeros(top.shape, jnp.int32)
    for g in range(1, N_EXPERT_GROUPS):
        better = group_score[g] > top
        grp = jnp.where(better, g, grp)
        top = jnp.maximum(top, group_score[g])

    def pick(a, k):
        out = row(a, (N_EXPERT_GROUPS - 1) * EXPERTS_PER_GROUP + k)
        for g in range(N_EXPERT_GROUPS - 2, -1, -1):
            out = jnp.where(grp == g, row(a, g * EXPERTS_PER_GROUP + k), out)
        return out

    cand = [pick(biased, k) for k in range(EXPERTS_PER_GROUP)]
    aff = [pick(score, k) for k in range(EXPERTS_PER_GROUP)]
    first = jnp.zeros(top.shape, jnp.int32)
    best = cand[0]
    for k in range(1, EXPERTS_PER_GROUP):
        better = cand[k] > best
        first = jnp.where(better, k, first)
        best = jnp.maximum(best, cand[k])
    second = jnp.full(top.shape, -1, jnp.int32)
    best2 = jnp.full(top.shape, -jnp.inf, F32)
    for k in range(EXPERTS_PER_GROUP):
        better = (first != k) & ((second < 0) | (cand[k] > best2))
        second = jnp.where(better, k, second)
        best2 = jnp.where(better, cand[k], best2)
    lo = jnp.minimum(first, second)
    hi = jnp.maximum(first, second)

    def take(vals, k_idx):
        out = vals[EXPERTS_PER_GROUP - 1]
        for k in range(EXPERTS_PER_GROUP - 2, -1, -1):
            out = jnp.where(k_idx == k, vals[k], out)
        return out

    aff_lo, aff_hi = take(aff, lo), take(aff, hi)
    denom = aff_lo + aff_hi
    pair = jnp.where(lo == 0, 0, jnp.where(lo == 1, 3, 5)) + (hi - lo - 1)
    bucket = (grp * PAIRS_PER_GROUP + pair).astype(F32)
    pad = jnp.zeros((META_ROWS - 3,) + top.shape[1:], F32)
    return jnp.concatenate([bucket, aff_lo / denom, aff_hi / denom, pad], axis=0)


def _mix_tail(x, o, pooled, gate_a, gate_b, wva_ref, wmix_ref, ps_ref, wbb_ref, wout_ref, g_ref, b_ref,
              wr_ref, rb_ref, x1_ref, gate_ref, *, alpha):
    gd = wmix_ref.shape[1]
    yb = jnp.concatenate(
        [jnp.dot(pooled[:, g * gd:(g + 1) * gd].astype(BF16), wmix_ref[g], preferred_element_type=F32)
         for g in range(len(POOL_WINDOWS))], axis=1) * ps_ref[...]
    ya = jnp.dot(o.astype(BF16), wva_ref[...], preferred_element_type=F32)
    yb = jnp.dot(yb.astype(BF16), wbb_ref[...], preferred_element_type=F32)
    merged = jax.nn.sigmoid(gate_a) * ya + jax.nn.sigmoid(gate_b) * yb
    mix = jnp.dot(merged.astype(BF16), wout_ref[...], preferred_element_type=F32)
    x1 = _layer_norm(alpha * x + mix, g_ref[...], b_ref[...])
    meta = _router_meta(x1, wr_ref[...], rb_ref[...])
    gate_ref[...] = meta
    d = x1.shape[1]
    x1_ref[:, :d] = x1
    wide = jnp.concatenate([meta[1:3], jnp.zeros((LANES - 2, meta.shape[1]), F32)], axis=0)
    x1_ref[:, d:] = wide.T


def _mix_prompt_kernel(*refs, alpha, tiles_per_seq, n_tiles):
    x1_ref, gate_ref = refs[-4], refs[-3]

    @pl.when(pl.program_id(0) < n_tiles)
    def _():
        _mix_prompt_tile(*refs, alpha=alpha, tiles_per_seq=tiles_per_seq)

    @pl.when(pl.program_id(0) >= n_tiles)
    def _():
        x1_ref[...] = jnp.zeros(x1_ref.shape, F32)
        gate_ref[...] = jnp.zeros(gate_ref.shape, F32)


def _mix_prompt_tile(x_ref, o_ref, w2_ref, wva_ref, wmix_ref, ps_ref, wbb_ref, wout_ref, g_ref, b_ref,
                     wr_ref, rb_ref, x1_ref, gate_ref, tail_ref, ext_ref, *, alpha, tiles_per_seq):
    tm = x_ref.shape[0]
    pw = ps_ref.shape[1]
    gd = pw // len(POOL_WINDOWS)
    x = x_ref[...]
    h2 = jnp.dot(x.astype(BF16), w2_ref[...], preferred_element_type=F32)
    u = h2[:, :pw]
    t = pl.program_id(0) % tiles_per_seq

    @pl.when(t == 0)
    def _():
        ext_ref[0:POOL_PAD, :] = jnp.zeros((POOL_PAD, pw), F32)

    @pl.when(t != 0)
    def _():
        ext_ref[0:POOL_PAD, :] = ext_ref[tm:tm + POOL_PAD, :]

    ext_ref[POOL_PAD:POOL_PAD + tm, :] = u
    tail_ref[0] = u[tm - POOL_PAD:, :]
    pos = t * tm + lax.broadcasted_iota(jnp.int32, (tm, 1), 0)
    pooled = []
    for g, w in enumerate(POOL_WINDOWS):
        acc = ext_ref[POOL_PAD:POOL_PAD + tm, g * gd:(g + 1) * gd]
        for j in range(1, w):
            acc = acc + ext_ref[POOL_PAD - j:POOL_PAD - j + tm, g * gd:(g + 1) * gd]
        cnt = jnp.minimum(w, pos + 1).astype(F32)
        pooled.append(acc / cnt - u[:, g * gd:(g + 1) * gd])
    pooled = jnp.concatenate(pooled, axis=1)
    d = x.shape[1]
    _mix_tail(x, o_ref[...], pooled, h2[:, pw:pw + d], h2[:, pw + d:], wva_ref, wmix_ref, ps_ref, wbb_ref,
              wout_ref, g_ref, b_ref, wr_ref, rb_ref, x1_ref, gate_ref, alpha=alpha)


def _mix_sample_kernel(x_ref, o_ref, pre_ref, x1_all_ref, meta_all_ref, w2_ref, wva_ref, wmix_ref, ps_ref, wbb_ref,
                       wout_ref, g_ref, b_ref, wr_ref, rb_ref, x1_ref, gate_ref, tail_ref, ext_ref,
                       *, alpha, t_new, past):
    del x1_all_ref, meta_all_ref
    tm = x_ref.shape[0]
    sb = tm // t_new
    pw = ps_ref.shape[1]
    gd = pw // len(POOL_WINDOWS)
    x = x_ref[...]
    h2 = jnp.dot(x.astype(BF16), w2_ref[...], preferred_element_type=F32)
    u = h2[:, :pw]
    ext_ref[:, 0:POOL_PAD, :] = pre_ref[...]
    ext_ref[:, POOL_PAD:POOL_PAD + t_new, :] = u.reshape(sb, t_new, pw)
    tail_ref[...] = ext_ref[:, t_new:t_new + POOL_PAD, :]
    pos = past + lax.broadcasted_iota(jnp.int32, (sb, t_new, 1), 1)
    pooled = []
    for g, w in enumerate(POOL_WINDOWS):
        acc = ext_ref[:, POOL_PAD:POOL_PAD + t_new, g * gd:(g + 1) * gd]
        for j in range(1, w):
            acc = acc + ext_ref[:, POOL_PAD - j:POOL_PAD - j + t_new, g * gd:(g + 1) * gd]
        cnt = jnp.minimum(w, pos + 1).astype(F32)
        pooled.append((acc / cnt).reshape(tm, gd) - u[:, g * gd:(g + 1) * gd])
    pooled = jnp.concatenate(pooled, axis=1)
    d = x.shape[1]
    _mix_tail(x, o_ref[...], pooled, h2[:, pw:pw + d], h2[:, pw + d:], wva_ref, wmix_ref, ps_ref, wbb_ref,
              wout_ref, g_ref, b_ref, wr_ref, rb_ref, x1_ref, gate_ref, alpha=alpha)


def _mix_prompt(x, x_off, o, weights, *, n_all, tm, alpha, seq):
    n = o.shape[0]
    d = x.shape[1]
    pw = weights[3].shape[1]
    tiles_per_seq = seq // tm
    n_tiles = n // tm
    last = n_tiles - 1
    return pl.pallas_call(
        functools.partial(_mix_prompt_kernel, alpha=alpha, tiles_per_seq=tiles_per_seq, n_tiles=n_tiles),
        grid=(n_all // tm,),
        in_specs=[pl.BlockSpec((tm, d), lambda i: (jnp.minimum(i, last) + x_off, 0)),
                  pl.BlockSpec((tm, o.shape[1]), lambda i: (jnp.minimum(i, last), 0))]
        + [_const_spec(a.shape) for a in weights],
        out_specs=[pl.BlockSpec((tm, d + LANES), lambda i: (i, 0)),
                   pl.BlockSpec((META_ROWS, tm), lambda i: (0, i)),
                   pl.BlockSpec((1, POOL_PAD, pw), lambda i: (jnp.minimum(i, last) // tiles_per_seq, 0, 0))],
        out_shape=[jax.ShapeDtypeStruct((n_all, d + LANES), F32), jax.ShapeDtypeStruct((META_ROWS, n_all), F32),
                   jax.ShapeDtypeStruct((n // seq, POOL_PAD, pw), F32)],
        scratch_shapes=[pltpu.VMEM((tm + POOL_PAD, pw), F32)],
        compiler_params=_params("arbitrary"),
        name="mix_prompt",
    )(x, o, *weights)


def _mix_sample(x, x_off, o, prefix, x1_all, meta_all, weights, *, row0, tm, alpha, t_new, past):
    n = o.shape[0]
    d = x.shape[1]
    pw = weights[3].shape[1]
    sb = tm // t_new
    off = row0 // tm
    any_spec = pl.BlockSpec(memory_space=pl.ANY)
    return pl.pallas_call(
        functools.partial(_mix_sample_kernel, alpha=alpha, t_new=t_new, past=past),
        grid=(n // tm,),
        in_specs=[pl.BlockSpec((tm, d), lambda i: (i + x_off, 0)), pl.BlockSpec((tm, o.shape[1]), lambda i: (i, 0)),
                  pl.BlockSpec((sb, POOL_PAD, pw), lambda i: (i, 0, 0)), any_spec, any_spec]
        + [_const_spec(a.shape) for a in weights],
        out_specs=[pl.BlockSpec((tm, d + LANES), lambda i: (i + off, 0)),
                   pl.BlockSpec((META_ROWS, tm), lambda i: (0, i + off)),
                   pl.BlockSpec((sb, POOL_PAD, pw), lambda i: (i, 0, 0))],
        out_shape=[jax.ShapeDtypeStruct(x1_all.shape, F32), jax.ShapeDtypeStruct(meta_all.shape, F32),
                   jax.ShapeDtypeStruct((n // t_new, POOL_PAD, pw), F32)],
        input_output_aliases={3: 0, 4: 1},
        scratch_shapes=[pltpu.VMEM((sb, POOL_PAD + t_new, pw), F32)],
        compiler_params=_params("arbitrary"),
        name="mix_sample",
    )(x, o, prefix, x1_all, meta_all, *weights)


ROW_CHUNK = 32


def _moe_kernel(src_ref, dst_ref, ea_ref, eb_ref, chunks_ref, used_ref, x1_hbm, wga_ref, wua_ref, wda_ref, wgb_ref,
                wub_ref, wdb_ref, g_ref, b_ref, out_hbm, xbuf, ybuf, gather_sem, scatter_sem, *, alpha, tm):
    del ea_ref, eb_ref
    i = pl.program_id(0)
    used = used_ref[0]
    buf = i % 2
    d = out_hbm.shape[1]

    def gather_row(tile, b, r):
        src = x1_hbm.at[pl.ds(src_ref[tile * tm + r], 1), :]
        return pltpu.make_async_copy(src, xbuf.at[b, pl.ds(r, 1), :], gather_sem.at[b])

    def scatter_row(tile, b, r):
        dst = out_hbm.at[pl.ds(dst_ref[tile * tm + r], 1), :]
        return pltpu.make_async_copy(ybuf.at[b, pl.ds(r, 1), :], dst, scatter_sem.at[b])

    def for_chunks(tile, fn):
        for c in range(tm // ROW_CHUNK):
            @pl.when(c < chunks_ref[tile])
            def _():
                for r in range(c * ROW_CHUNK, (c + 1) * ROW_CHUNK):
                    fn(r)

    def start_gather(tile, b):
        for_chunks(tile, lambda r: gather_row(tile, b, r).start())

    def wait_gather(tile, b):
        for_chunks(tile, lambda r: gather_row(0, b, 0).wait())

    def start_scatter(tile, b):
        for_chunks(tile, lambda r: scatter_row(tile, b, r).start())

    def wait_scatter(tile, b):
        for_chunks(tile, lambda r: scatter_row(0, b, 0).wait())

    @pl.when(i == 0)
    def _():
        xbuf[...] = jnp.zeros(xbuf.shape, F32)
        start_gather(0, 0)
        n_rows, n_out = x1_hbm.shape[0], out_hbm.shape[0]
        ybuf[1] = jnp.zeros(ybuf.shape[1:], F32)
        fills = [pltpu.make_async_copy(ybuf.at[1], out_hbm.at[pl.ds(row, tm), :], scatter_sem.at[1])
                 for row in range(n_rows, n_out, tm)]
        for c in fills:
            c.start()
        for c in fills:
            c.wait()

    for b in range(2):
        @pl.when((i < used) & (buf == b))
        def _():
            @pl.when(i + 1 < used)
            def _():
                start_gather(i + 1, 1 - b)

            wait_gather(i, b)

            @pl.when(i >= 2)
            def _():
                wait_scatter(i - 2, b)

    @pl.when(i < used)
    def _():
        row = xbuf[buf]
        x = row[:, :d]
        wt = row[:, d:]
        xb = x.astype(BF16)

        def expert(wg_ref, wu_ref, wd_ref, col):
            hg = jnp.dot(xb, wg_ref[0], preferred_element_type=F32)
            hu = jnp.dot(xb, wu_ref[0], preferred_element_type=F32)
            hid = (jax.nn.silu(hg) * hu * col).astype(BF16)
            return jnp.dot(hid, wd_ref[0], preferred_element_type=F32)

        y = expert(wga_ref, wua_ref, wda_ref, wt[:, 0:1]) + expert(wgb_ref, wub_ref, wdb_ref, wt[:, 1:2])
        ybuf[buf] = _layer_norm(alpha * x + y, g_ref[...], b_ref[...])

    for b in range(2):
        @pl.when((i < used) & (buf == b))
        def _():
            start_scatter(i, b)

    @pl.when(i == pl.num_programs(0) - 1)
    def _():
        wait_scatter(used - 1, (used - 1) % 2)

        @pl.when(used >= 2)
        def _():
            wait_scatter(used - 2, used % 2)


def _moe(x1, row_src, row_dst, tile_ea, tile_eb, tile_chunks, n_used, wg, wu, wd, g2, b2, *, tm, alpha):
    p = row_src.shape[0]
    d = g2.shape[1]
    f = wg.shape[2]
    first = lambda i, src, dst, ea, eb, chunks, used: (ea[i], 0, 0)
    second = lambda i, src, dst, ea, eb, chunks, used: (eb[i], 0, 0)
    const = lambda shape: pl.BlockSpec(shape, lambda i, *pre: (0,) * len(shape), pipeline_mode=pl.Buffered(1))
    grid_spec = pltpu.PrefetchScalarGridSpec(
        num_scalar_prefetch=6,
        grid=(p // tm,),
        in_specs=[pl.BlockSpec(memory_space=pl.ANY),
                  pl.BlockSpec((1, d, f), first), pl.BlockSpec((1, d, f), first), pl.BlockSpec((1, f, d), first),
                  pl.BlockSpec((1, d, f), second), pl.BlockSpec((1, d, f), second), pl.BlockSpec((1, f, d), second),
                  const(g2.shape), const(b2.shape)],
        out_specs=pl.BlockSpec(memory_space=pl.ANY),
        scratch_shapes=[pltpu.VMEM((2, tm, x1.shape[1]), F32), pltpu.VMEM((2, tm, d), F32),
                        pltpu.SemaphoreType.DMA((2,)), pltpu.SemaphoreType.DMA((2,))],
    )
    return pl.pallas_call(
        functools.partial(_moe_kernel, alpha=alpha, tm=tm),
        grid_spec=grid_spec,
        out_shape=jax.ShapeDtypeStruct((p, d), F32),
        compiler_params=_params("arbitrary"),
        name="moe",
    )(row_src, row_dst, tile_ea, tile_eb, tile_chunks, n_used, x1, wg, wu, wd, wg, wu, wd, g2, b2)


def _routing_plan(meta, *, tm):
    n = meta.shape[1]
    n_tiles = (n + N_BUCKETS * (tm - 1) + tm - 1) // tm
    p = n_tiles * tm
    bucket = meta[0].astype(jnp.int32)
    onehot = (bucket[:, None] == jnp.arange(N_BUCKETS, dtype=jnp.int32)[None, :]).astype(jnp.int32)
    csum = jnp.cumsum(onehot, axis=0)
    rank = jnp.sum(csum * onehot, axis=1) - 1
    tiles_per_bucket = (csum[-1] + tm - 1) // tm
    tile_end = jnp.cumsum(tiles_per_bucket)
    n_used = tile_end[-1:]
    row_start = (tile_end - tiles_per_bucket) * tm
    slot = jnp.sum(onehot * row_start[None, :], axis=1) + rank
    token = jnp.full((p,), -1, jnp.int32).at[slot].set(jnp.arange(n, dtype=jnp.int32), unique_indices=True)
    is_pad = token < 0
    row_src = jnp.where(is_pad, 0, token)
    row_dst = jnp.where(is_pad, n - 1 + jnp.cumsum(is_pad.astype(jnp.int32)), token)
    tile_id = jnp.minimum(jnp.arange(n_tiles, dtype=jnp.int32), n_used - 1)
    tile_bucket = jnp.minimum(jnp.sum((tile_id[:, None] >= tile_end[None, :]).astype(jnp.int32), axis=1),
                              N_BUCKETS - 1)
    group, pair = tile_bucket // PAIRS_PER_GROUP, tile_bucket % PAIRS_PER_GROUP
    lo = sum((pair == k).astype(jnp.int32) * PAIR_LO[k] for k in range(PAIRS_PER_GROUP))
    hi = sum((pair == k).astype(jnp.int32) * PAIR_HI[k] for k in range(PAIRS_PER_GROUP))
    tile_ea = group * EXPERTS_PER_GROUP + lo
    tile_eb = group * EXPERTS_PER_GROUP + hi
    in_bucket = (tile_bucket[:, None] == jnp.arange(N_BUCKETS, dtype=jnp.int32)[None, :]).astype(jnp.int32)
    tokens_left = jnp.sum(in_bucket * (csum[-1] - (tile_id[:, None] - (tile_end - tiles_per_bucket)) * tm), axis=1)
    tile_chunks = (jnp.clip(tokens_left, 0, tm) + ROW_CHUNK - 1) // ROW_CHUNK
    return row_src, row_dst, tile_ea, tile_eb, tile_chunks.astype(jnp.int32), n_used.astype(jnp.int32)


class _Tiles(NamedTuple):
    proj: int
    attn_q: int
    attn_k: int
    mix: int
    sample: int
    moe: int


def _tile_plan(prompt_len, n_sample):
    big = min(512, prompt_len)
    return _Tiles(proj=big, attn_q=big, attn_k=big, mix=big, sample=min(256, n_sample), moe=256)


def _rope_tables(pos, reps):
    half = QK_ROPE_DIM // 2
    inv = ROPE_BASE ** (-jnp.arange(half, dtype=F32) / half)
    ang = pos.astype(F32)[:, None] * inv
    cos, sin = jnp.cos(ang), jnp.sin(ang)
    return (jnp.tile(jnp.concatenate([cos, cos], axis=1), (1, reps)),
            jnp.tile(jnp.concatenate([-sin, sin], axis=1), (1, reps)))


def kernel(x_prompt, x_sample, cache_kv_latent, cache_k_rope, state_pool, page_table, w_in, g_q_norm, g_kv_norm,
           w_q_up, w_k_up, w_v_up, w_pool_mix, pool_scale, w_branch_a, w_branch_b, w_out, ln1_g, ln1_b,
           w_router, router_bias, w_exp_gate, w_exp_up, w_exp_down, ln2_g, ln2_b):
    bp, tp, d = x_prompt.shape
    bs, ts, _ = x_sample.shape
    depth = w_in.shape[0]
    n_pages = page_table.shape[1]
    past = n_pages * cache_kv_latent.shape[2]
    alpha = (2 * depth) ** 0.25
    pw = pool_scale.shape[1]
    n_p, n_s = bp * tp, bs * ts
    s1 = Q_LORA_RANK + KV_LORA_RANK + QK_ROPE_DIM

    tiles = _tile_plan(tp, n_s)
    n_all = n_p + n_s

    cos_p, sin_p = _rope_tables(jnp.arange(tp, dtype=jnp.int32), N_HEADS)
    cos_pt, sin_pt = cos_p.T, sin_p.T
    cos_p, sin_p = cos_p[:, :QK_ROPE_DIM], sin_p[:, :QK_ROPE_DIM]
    cos_s, sin_s = _rope_tables(past + jnp.arange(ts, dtype=jnp.int32), N_HEADS)
    cos_s, sin_s = jnp.tile(cos_s, (tiles.sample // ts, 1)), jnp.tile(sin_s, (tiles.sample // ts, 1))
    prefix = jnp.pad(state_pool, ((0, 0), (0, 0), (POOL_PAD - state_pool.shape[2], 0), (0, 0)))
    cache_rt = jnp.swapaxes(cache_k_rope, 2, 3)
    wr_t = w_router.T
    rb = router_bias.reshape(N_EXPERTS, 1)

    x_p, x_s, x_s_off = x_prompt.reshape(n_p, d), x_sample.reshape(n_s, d), 0
    outs = [[] for _ in range(6)]
    for l in range(depth):
        wq = w_q_up[l]
        wq_nope = jnp.transpose(wq[:, :, :QK_NOPE_DIM], (1, 0, 2))
        wqr = (wq[:, :, QK_NOPE_DIM:] * Q_SCALE).reshape(Q_LORA_RANK, N_HEADS * QK_ROPE_DIM).astype(BF16)
        wk = jnp.transpose(w_k_up[l], (1, 0, 2))
        wv = jnp.transpose(w_v_up[l], (1, 0, 2))
        wba = w_branch_a[l].reshape(N_HEADS, -1, d)
        wlat, wlat_t, wva = _fold_weights(wq_nope, wk, wv, wba)
        w1 = w_in[l, :, :s1].astype(BF16)
        w2 = w_in[l, :, s1:].astype(BF16)
        gq = g_q_norm[l].reshape(1, -1)
        gkv = g_kv_norm[l].reshape(1, -1)
        mix_w = (w2, wva, w_pool_mix[l].astype(BF16), pool_scale[l].reshape(1, pw), w_branch_b[l].astype(BF16),
                 w_out[l].astype(BF16), ln1_g[l].reshape(1, d), ln1_b[l].reshape(1, d), wr_t, rb)
        wg, wu, wd = w_exp_gate[l].astype(BF16), w_exp_up[l].astype(BF16), w_exp_down[l].astype(BF16)
        g2, b2 = ln2_g[l].reshape(1, d), ln2_b[l].reshape(1, d)

        qt, kvcat, ckv_t, ckv, krope = _proj_prompt(x_p, n_p, w1, wlat_t, wqr.T, gq, gkv, cos_p, sin_p, cos_pt,
                                                    sin_pt, tm=tiles.proj)
        o = _attn_prompt(qt, kvcat, ckv_t, batch=bp, seq=tp, tq=tiles.attn_q, tk=tiles.attn_k)
        x1, meta, tail = _mix_prompt(x_p, 0, o, mix_w, n_all=n_all, tm=tiles.mix, alpha=alpha, seq=tp)
        outs[0].append(ckv.reshape(bp, tp, -1))
        outs[1].append(krope.reshape(bp, tp, -1))
        outs[2].append(tail[:, 1:, :])

        q, kvcat, ckv, krope = _proj_sample(x_s, x_s_off, n_s, w1, wlat, wqr, gq, gkv, cos_s, sin_s, tm=tiles.sample)
        o = _attn_sample(page_table, q, kvcat, cache_kv_latent, cache_rt, layer=l, t_new=ts)
        x1, meta, tail = _mix_sample(x_s, x_s_off, o, prefix[l], x1, meta, mix_w, row0=n_p, tm=tiles.sample, alpha=alpha,
                                     t_new=ts, past=past)
        outs[3].append(ckv.reshape(bs, ts, -1))
        outs[4].append(krope.reshape(bs, ts, -1))
        outs[5].append(tail[:, 1:, :])

        row_src, row_dst, tile_ea, tile_eb, tile_chunks, n_used = _routing_plan(meta, tm=tiles.moe)
        x_all = _moe(x1, row_src, row_dst, tile_ea, tile_eb, tile_chunks, n_used, wg, wu, wd, g2, b2, tm=tiles.moe,
                     alpha=alpha)
        x_p, x_s, x_s_off = x_all, x_all, n_p // tiles.sample

    return (x_all[:n_p].reshape(bp, tp, d), x_all[n_p:n_all].reshape(bs, ts, d), jnp.stack(outs[0]),
            jnp.stack(outs[1]), jnp.stack(outs[2]), jnp.stack(outs[3]), jnp.stack(outs[4]), jnp.stack(outs[5]))
```

```python
import functools
from typing import NamedTuple

import jax
import jax.numpy as jnp
from jax import lax
from jax.experimental import pallas as pl
from jax.experimental.pallas import tpu as pltpu

F32 = jnp.float32
BF16 = jnp.bfloat16

N_HEADS = 8
QK_NOPE_DIM = 64
QK_ROPE_DIM = 32
Q_LORA_RANK = 256
KV_LORA_RANK = 128
QK_CAT = KV_LORA_RANK + QK_ROPE_DIM
ROPE_BASE = 10000.0
SM_SCALE = (QK_NOPE_DIM + QK_ROPE_DIM) ** -0.5
LOG2_E = 1.4426950408889634
Q_SCALE = SM_SCALE * LOG2_E
POOL_WINDOWS = (2, 4, 8, 16)
POOL_PAD = 16
N_EXPERTS = 16
N_EXPERT_GROUPS = 4
EXPERTS_PER_GROUP = 4
PAIRS_PER_GROUP = 6
PAIR_LO = (0, 0, 0, 1, 1, 2)
PAIR_HI = (1, 2, 3, 2, 3, 3)
N_BUCKETS = N_EXPERT_GROUPS * PAIRS_PER_GROUP
META_ROWS = 8
LN_EPS = 1e-5
RMS_EPS = 1e-6

LANES = 128
VMEM_LIMIT_BYTES = 56 * 1024 * 1024


def _params(*semantics):
    return pltpu.CompilerParams(dimension_semantics=semantics, vmem_limit_bytes=VMEM_LIMIT_BYTES)


def _const_spec(shape):
    zeros = (0,) * len(shape)
    return pl.BlockSpec(shape, lambda *_: zeros, pipeline_mode=pl.Buffered(1))


def _split_bf16(a):
    hi = a.astype(BF16)
    lo = (a - hi.astype(F32)).astype(BF16)
    return hi, lo


def _dot3(a, b, dims):
    a_hi, a_lo = _split_bf16(a)
    b_hi, b_lo = _split_bf16(b)
    d = functools.partial(lax.dot_general, dimension_numbers=dims, preferred_element_type=F32)
    return d(a_hi, b_hi) + (d(a_hi, b_lo) + d(a_lo, b_hi))


_NN = (((1,), (0,)), ((), ()))
_NT = (((1,), (1,)), ((), ()))


def _layer_norm(y, g, b):
    mu = jnp.mean(y, axis=-1, keepdims=True)
    d = y - mu
    var = jnp.mean(d * d, axis=-1, keepdims=True)
    return d * lax.rsqrt(var + LN_EPS) * g + b


def _rms_norm(y, g):
    return y * lax.rsqrt(jnp.mean(y * y, axis=-1, keepdims=True) + RMS_EPS) * g


def _fold_kernel(wqn_ref, wqn_t_ref, wk_ref, wk_t_ref, wv_ref, wba_ref, wlat_ref, wlat_t_ref, wva_ref):
    wlat_ref[...] = (_dot3(wqn_ref[0], wk_t_ref[0], _NN) * Q_SCALE).astype(BF16)
    wlat_t_ref[...] = (_dot3(wk_ref[0], wqn_t_ref[0], _NN) * Q_SCALE).astype(BF16)
    wva_ref[...] = _dot3(wv_ref[0], wba_ref[0], _NN).astype(BF16)


def _fold_weights(wq_nope, wk, wv, wba):
    h, r, dn = wq_nope.shape
    d = wba.shape[-1]
    head = lambda *blk: pl.BlockSpec((1,) + blk, lambda i: (i, 0, 0))
    return pl.pallas_call(
        _fold_kernel,
        grid=(h,),
        in_specs=[head(r, dn), head(dn, r), head(KV_LORA_RANK, dn), head(dn, KV_LORA_RANK),
                  head(KV_LORA_RANK, wv.shape[-1]), head(wba.shape[1], d)],
        out_specs=[pl.BlockSpec((r, KV_LORA_RANK), lambda i: (0, i)),
                   pl.BlockSpec((KV_LORA_RANK, r), lambda i: (i, 0)),
                   pl.BlockSpec((KV_LORA_RANK, d), lambda i: (i, 0))],
        out_shape=[jax.ShapeDtypeStruct((r, h * KV_LORA_RANK), BF16),
                   jax.ShapeDtypeStruct((h * KV_LORA_RANK, r), BF16),
                   jax.ShapeDtypeStruct((h * KV_LORA_RANK, d), BF16)],
        compiler_params=_params("arbitrary"),
        name="fold",
    )(wq_nope, jnp.swapaxes(wq_nope, 1, 2), wk, jnp.swapaxes(wk, 1, 2), wv, wba)


def _proj_common(x_ref, w1_ref, gq_ref, gkv_ref, cos_ref, sin_ref, ckv_ref, krope_ref):
    x = x_ref[...].astype(BF16)
    h = jnp.dot(x, w1_ref[...], preferred_element_type=F32)
    qa = h[:, :Q_LORA_RANK]
    kva = h[:, Q_LORA_RANK:Q_LORA_RANK + KV_LORA_RANK]
    kr = h[:, Q_LORA_RANK + KV_LORA_RANK:]
    qn = _rms_norm(qa, gq_ref[...])
    ckv = _rms_norm(kva, gkv_ref[...])
    half = QK_ROPE_DIM // 2
    kr_swapped = jnp.concatenate([kr[:, half:], kr[:, :half]], axis=1)
    krope = kr * cos_ref[:, :QK_ROPE_DIM] + kr_swapped * sin_ref[:, :QK_ROPE_DIM]
    ckv_ref[...] = ckv
    krope_ref[...] = krope
    return qn, ckv, krope


def _proj_sample_kernel(x_ref, w1_ref, wlat_ref, wqr_ref, gq_ref, gkv_ref, cos_ref, sin_ref,
                        q_ref, kvcat_ref, ckv_ref, krope_ref):
    qn, ckv, krope = _proj_common(x_ref, w1_ref, gq_ref, gkv_ref, cos_ref, sin_ref, ckv_ref, krope_ref)
    qn = qn.astype(BF16)
    ql = jnp.dot(qn, wlat_ref[...], preferred_element_type=F32)
    qr = jnp.dot(qn, wqr_ref[...], preferred_element_type=F32)
    half = QK_ROPE_DIM // 2
    width = qr.shape[1]
    lane = lax.broadcasted_iota(jnp.int32, qr.shape, 1)
    swapped = jnp.where((lane % QK_ROPE_DIM) < half,
                        pltpu.roll(qr, width - half, 1), pltpu.roll(qr, half, 1))
    qrr = qr * cos_ref[...] + swapped * sin_ref[...]
    for hh in range(N_HEADS):
        q_ref[hh, :, :KV_LORA_RANK] = ql[:, hh * KV_LORA_RANK:(hh + 1) * KV_LORA_RANK]
        q_ref[hh, :, KV_LORA_RANK:] = qrr[:, hh * QK_ROPE_DIM:(hh + 1) * QK_ROPE_DIM]
    kvcat_ref[:, :KV_LORA_RANK] = ckv
    kvcat_ref[:, KV_LORA_RANK:] = krope


def _proj_prompt_kernel(x_ref, w1_ref, wlat_t_ref, wqr_t_ref, gq_ref, gkv_ref, cos_ref, sin_ref,
                        cos_t_ref, sin_t_ref, qt_ref, kvcat_ref, ckv_t_ref, ckv_ref, krope_ref):
    qn, ckv, krope = _proj_common(x_ref, w1_ref, gq_ref, gkv_ref, cos_ref, sin_ref, ckv_ref, krope_ref)
    qn_t = qn.T.astype(BF16)
    ql_t = jnp.dot(wlat_t_ref[...], qn_t, preferred_element_type=F32)
    qr_t = jnp.dot(wqr_t_ref[...], qn_t, preferred_element_type=F32)
    half = QK_ROPE_DIM // 2
    pieces = []
    for hh in range(N_HEADS):
        base = hh * QK_ROPE_DIM
        pieces += [qr_t[base + half:base + QK_ROPE_DIM], qr_t[base:base + half]]
    qrr_t = qr_t * cos_t_ref[...] + jnp.concatenate(pieces, axis=0) * sin_t_ref[...]
    for hh in range(N_HEADS):
        qt_ref[hh, :KV_LORA_RANK, :] = ql_t[hh * KV_LORA_RANK:(hh + 1) * KV_LORA_RANK].astype(BF16)
        qt_ref[hh, KV_LORA_RANK:, :] = qrr_t[hh * QK_ROPE_DIM:(hh + 1) * QK_ROPE_DIM].astype(BF16)
    kvcat_ref[:, :KV_LORA_RANK] = ckv.astype(BF16)
    kvcat_ref[:, KV_LORA_RANK:] = krope.astype(BF16)
    ckv_t_ref[...] = ckv.T.astype(BF16)


def _proj_sample(x, x_off, n, w1, wlat, wqr, gq, gkv, cos_t, sin_t, *, tm):
    d = x.shape[1]
    row = lambda width: pl.BlockSpec((tm, width), lambda i: (i, 0))
    return pl.pallas_call(
        _proj_sample_kernel,
        grid=(n // tm,),
        in_specs=[pl.BlockSpec((tm, d), lambda i: (i + x_off, 0)), _const_spec(w1.shape), _const_spec(wlat.shape), _const_spec(wqr.shape),
                  _const_spec(gq.shape), _const_spec(gkv.shape),
                  _const_spec(cos_t.shape), _const_spec(sin_t.shape)],
        out_specs=[pl.BlockSpec((N_HEADS, tm, QK_CAT), lambda i: (0, i, 0)),
                   row(QK_CAT), row(KV_LORA_RANK), row(QK_ROPE_DIM)],
        out_shape=[jax.ShapeDtypeStruct((N_HEADS, n, QK_CAT), F32),
                   jax.ShapeDtypeStruct((n, QK_CAT), F32),
                   jax.ShapeDtypeStruct((n, KV_LORA_RANK), F32),
                   jax.ShapeDtypeStruct((n, QK_ROPE_DIM), F32)],
        compiler_params=_params("arbitrary"),
        name="proj_sample",
    )(x, w1, wlat, wqr, gq, gkv, cos_t, sin_t)


def _proj_prompt(x, n, w1, wlat_t, wqr_t, gq, gkv, cos, sin, cos_t, sin_t, *, tm):
    d = x.shape[1]
    tiles = cos.shape[0] // tm
    row = lambda width: pl.BlockSpec((tm, width), lambda i: (i, 0))
    return pl.pallas_call(
        _proj_prompt_kernel,
        grid=(n // tm,),
        in_specs=[row(d), _const_spec(w1.shape), _const_spec(wlat_t.shape), _const_spec(wqr_t.shape),
                  _const_spec(gq.shape), _const_spec(gkv.shape),
                  pl.BlockSpec((tm, cos.shape[1]), lambda i: (i % tiles, 0)),
                  pl.BlockSpec((tm, sin.shape[1]), lambda i: (i % tiles, 0)),
                  pl.BlockSpec((cos_t.shape[0], tm), lambda i: (0, i % tiles)),
                  pl.BlockSpec((sin_t.shape[0], tm), lambda i: (0, i % tiles))],
        out_specs=[pl.BlockSpec((N_HEADS, QK_CAT, tm), lambda i: (0, 0, i)),
                   row(QK_CAT),
                   pl.BlockSpec((KV_LORA_RANK, tm), lambda i: (0, i)),
                   row(KV_LORA_RANK), row(QK_ROPE_DIM)],
        out_shape=[jax.ShapeDtypeStruct((N_HEADS, QK_CAT, n), BF16),
                   jax.ShapeDtypeStruct((n, QK_CAT), BF16),
                   jax.ShapeDtypeStruct((KV_LORA_RANK, n), BF16),
                   jax.ShapeDtypeStruct((n, KV_LORA_RANK), F32),
                   jax.ShapeDtypeStruct((n, QK_ROPE_DIM), F32)],
        compiler_params=_params("arbitrary"),
        name="proj_prompt",
    )(x, w1, wlat_t, wqr_t, gq, gkv, cos, sin, cos_t, sin_t)


SCORE_AHEAD = 2
SCORE_SLOTS = 4


def _col_reduce(x, op, reduce_rows):
    rows = x.shape[0]
    while rows >= 32:
        x = x.reshape(4, rows // 4, x.shape[1])
        x = op(op(x[0], x[1]), op(x[2], x[3]))
        rows //= 4
    return reduce_rows(x, axis=0, keepdims=True)


def _attn_prompt_kernel(qt_ref, k_ref, vt_ref, o_ref, m_ref, l_ref, acc_ref, s_ref, *, tq, tk):
    i = pl.program_id(1)
    n_full = (i * tq) // tk
    m_ref[...] = jnp.full(m_ref.shape, -jnp.inf, F32)
    l_ref[...] = jnp.zeros(l_ref.shape, F32)
    acc_ref[...] = jnp.zeros(acc_ref.shape, F32)

    def scores(j, hh):
        k = k_ref[pl.ds(pl.multiple_of(j * tk, tk), tk), :]
        s_ref[hh % SCORE_SLOTS] = jnp.dot(k, qt_ref[hh], preferred_element_type=F32)

    def step(j, masked):
        vt = vt_ref[:, pl.ds(pl.multiple_of(j * tk, tk), tk)]
        if masked:
            key = j * tk + lax.broadcasted_iota(jnp.int32, (tk, tq), 0)
            tok = i * tq + lax.broadcasted_iota(jnp.int32, (tk, tq), 1)
            visible = key <= tok
        for hh in range(N_HEADS):
            ahead = hh + SCORE_AHEAD
            if ahead < N_HEADS:
                scores(j, ahead)
            elif not masked:
                scores(j + 1, ahead - N_HEADS)
            s = s_ref[hh % SCORE_SLOTS]
            if masked:
                s = jnp.where(visible, s, -jnp.inf)
            m_prev = m_ref[hh]
            m_new = jnp.maximum(m_prev, _col_reduce(s, jnp.maximum, jnp.max))
            alpha = jnp.exp2(m_prev - m_new)
            p = jnp.exp2(s - m_new)
            l_ref[hh] = alpha * l_ref[hh] + _col_reduce(p, jnp.add, jnp.sum)
            pv = jnp.dot(vt, p.astype(BF16), preferred_element_type=F32)
            acc_ref[hh] = alpha * acc_ref[hh] + pv
            m_ref[hh] = m_new

    def body(j, carry):
        step(j, False)
        return carry

    for hh in range(SCORE_AHEAD):
        scores(0, hh)
    lax.fori_loop(0, n_full, body, 0)
    step(n_full, True)
    for hh in range(N_HEADS):
        o_t = acc_ref[hh] / l_ref[hh]
        o_ref[:, hh * KV_LORA_RANK:(hh + 1) * KV_LORA_RANK] = o_t.T.astype(o_ref.dtype)


def _attn_prompt(qt, kvcat, ckv_t, *, batch, seq, tq, tk):
    assert tk % tq == 0 and seq % tk == 0, (seq, tq, tk)
    nq = seq // tq
    n = batch * seq
    return pl.pallas_call(
        functools.partial(_attn_prompt_kernel, tq=tq, tk=tk),
        grid=(batch, nq),
        in_specs=[pl.BlockSpec((N_HEADS, QK_CAT, tq), lambda b, i: (0, 0, b * nq + i)),
                  pl.BlockSpec((seq, QK_CAT), lambda b, i: (b, 0)),
                  pl.BlockSpec((KV_LORA_RANK, seq), lambda b, i: (0, b))],
        out_specs=pl.BlockSpec((tq, N_HEADS * KV_LORA_RANK), lambda b, i: (b * nq + i, 0)),
        out_shape=jax.ShapeDtypeStruct((n, N_HEADS * KV_LORA_RANK), BF16),
        scratch_shapes=[pltpu.VMEM((N_HEADS, 1, tq), F32), pltpu.VMEM((N_HEADS, 1, tq), F32),
                        pltpu.VMEM((N_HEADS, KV_LORA_RANK, tq), F32), pltpu.VMEM((SCORE_SLOTS, tk, tq), F32)],
        compiler_params=_params("arbitrary", "arbitrary"),
        name="attn_prompt",
    )(qt, kvcat, ckv_t)


def _attn_sample_kernel(pt_ref, q_ref, knew_ref, cache_c_ref, cache_rt_ref, o_ref, cbuf, rbuf, sem,
                        *, layer, n_pages, page, t_new):
    b = pl.program_id(0)
    nb = pl.num_programs(0)

    def page_copies(seq_idx, slot):
        copies = []
        for p in range(n_pages):
            pid = pt_ref[seq_idx, p]
            copies.append(pltpu.make_async_copy(
                cache_c_ref.at[layer, pid], cbuf.at[slot, pl.ds(p * page, page), :], sem.at[slot, 0]))
            copies.append(pltpu.make_async_copy(
                cache_rt_ref.at[layer, pid], rbuf.at[slot, :, pl.ds(p * page, page)], sem.at[slot, 1]))
        return copies

    slot = b % 2

    @pl.when(b == 0)
    def _():
        for c in page_copies(0, 0):
            c.start()

    @pl.when(b + 1 < nb)
    def _():
        for c in page_copies(b + 1, 1 - slot):
            c.start()

    for c in page_copies(b, slot):
        c.wait()

    rows = N_HEADS * t_new
    q = q_ref[...].reshape(rows, QK_CAT).astype(BF16)
    kc = cbuf[slot].astype(BF16)
    krt = rbuf[slot].astype(BF16)
    kn = knew_ref[...].astype(BF16)
    s = (lax.dot_general(q[:, :KV_LORA_RANK], kc, _NT, preferred_element_type=F32)
         + jnp.dot(q[:, KV_LORA_RANK:], krt, preferred_element_type=F32))
    sn = lax.dot_general(q, kn, _NT, preferred_element_type=F32)
    tok = lax.broadcasted_iota(jnp.int32, (N_HEADS, t_new, t_new), 1).reshape(rows, t_new)
    key = lax.broadcasted_iota(jnp.int32, (rows, t_new), 1)
    sn = jnp.where(key <= tok, sn, -jnp.inf)
    m = jnp.maximum(jnp.max(s, axis=-1, keepdims=True), jnp.max(sn, axis=-1, keepdims=True))
    p = jnp.exp2(s - m)
    pn = jnp.exp2(sn - m)
    l = jnp.sum(p, axis=-1, keepdims=True) + jnp.sum(pn, axis=-1, keepdims=True)
    o = (jnp.dot(p.astype(BF16), kc, preferred_element_type=F32)
         + jnp.dot(pn.astype(BF16), kn[:, :KV_LORA_RANK], preferred_element_type=F32)) / l
    for hh in range(N_HEADS):
        o_ref[:, hh * KV_LORA_RANK:(hh + 1) * KV_LORA_RANK] = o[hh * t_new:(hh + 1) * t_new]


def _attn_sample(page_table, q, knew, cache_c, cache_rt, *, layer, t_new):
    n_seq, n_pages = page_table.shape
    page = cache_c.shape[2]
    past = n_pages * page
    grid_spec = pltpu.PrefetchScalarGridSpec(
        num_scalar_prefetch=1,
        grid=(n_seq,),
        in_specs=[pl.BlockSpec((N_HEADS, t_new, QK_CAT), lambda b, pt: (0, b, 0)),
                  pl.BlockSpec((t_new, QK_CAT), lambda b, pt: (b, 0)),
                  pl.BlockSpec(memory_space=pl.ANY),
                  pl.BlockSpec(memory_space=pl.ANY)],
        out_specs=pl.BlockSpec((t_new, N_HEADS * KV_LORA_RANK), lambda b, pt: (b, 0)),
        scratch_shapes=[pltpu.VMEM((2, past, KV_LORA_RANK), F32), pltpu.VMEM((2, QK_ROPE_DIM, past), F32),
                        pltpu.SemaphoreType.DMA((2, 2))],
    )
    return pl.pallas_call(
        functools.partial(_attn_sample_kernel, layer=layer, n_pages=n_pages, page=page, t_new=t_new),
        grid_spec=grid_spec,
        out_shape=jax.ShapeDtypeStruct((n_seq * t_new, N_HEADS * KV_LORA_RANK), F32),
        compiler_params=_params("arbitrary"),
        name="attn_sample",
    )(page_table, q, knew, cache_c, cache_rt)


def _router_meta(x1, wr, rb):
    logits = _dot3(wr, x1, _NT)
    score = jax.nn.sigmoid(logits)
    biased = score + rb
    row = lambda a, e: a[e:e + 1, :]
    group_score = []
    for g in range(N_EXPERT_GROUPS):
        v = [row(biased, g * EXPERTS_PER_GROUP + k) for k in range(EXPERTS_PER_GROUP)]
        best = None
        for a in range(EXPERTS_PER_GROUP):
            for c in range(a + 1, EXPERTS_PER_GROUP):
                pair = v[a] + v[c]
                best = pair if best is None else jnp.maximum(best, pair)
        group_score.append(best)
    top = group_score[0]
    grp = jnp.zeros(top.shape, jnp.int32)
    for g in range(1, N_EXPERT_GROUPS):
        better = group_score[g] > top
        grp = jnp.where(better, g, grp)
        top = jnp.maximum(top, group_score[g])

    def pick(a, k):
        out = row(a, (N_EXPERT_GROUPS - 1) * EXPERTS_PER_GROUP + k)
        for g in range(N_EXPERT_GROUPS - 2, -1, -1):
            out = jnp.where(grp == g, row(a, g * EXPERTS_PER_GROUP + k), out)
        return out

    cand = [pick(biased, k) for k in range(EXPERTS_PER_GROUP)]
    aff = [pick(score, k) for k in range(EXPERTS_PER_GROUP)]
    first = jnp.zeros(top.shape, jnp.int32)
    best = cand[0]
    for k in range(1, EXPERTS_PER_GROUP):
        better = cand[k] > best
        first = jnp.where(better, k, first)
        best = jnp.maximum(best, cand[k])
    second = jnp.full(top.shape, -1, jnp.int32)
    best2 = jnp.full(top.shape, -jnp.inf, F32)
    for k in range(EXPERTS_PER_GROUP):
        better = (first != k) & ((second < 0) | (cand[k] > best2))
        second = jnp.where(better, k, second)
        best2 = jnp.where(better, cand[k], best2)
    lo = jnp.minimum(first, second)
    hi = jnp.maximum(first, second)

    def take(vals, k_idx):
        out = vals[EXPERTS_PER_GROUP - 1]
        for k in range(EXPERTS_PER_GROUP - 2, -1, -1):
            out = jnp.where(k_idx == k, vals[k], out)
        return out

    aff_lo, aff_hi = take(aff, lo), take(aff, hi)
    denom = aff_lo + aff_hi
    pair = jnp.where(lo == 0, 0, jnp.where(lo == 1, 3, 5)) + (hi - lo - 1)
    bucket = (grp * PAIRS_PER_GROUP + pair).astype(F32)
    pad = jnp.zeros((META_ROWS - 3,) + top.shape[1:], F32)
    return jnp.concatenate([bucket, aff_lo / denom, aff_hi / denom, pad], axis=0)


def _mix_tail(x, o, pooled, gate_a, gate_b, wva_ref, wmix_ref, ps_ref, wbb_ref, wout_ref, g_ref, b_ref,
              wr_ref, rb_ref, x1_ref, gate_ref, *, alpha):
    gd = wmix_ref.shape[1]
    yb = jnp.concatenate(
        [jnp.dot(pooled[:, g * gd:(g + 1) * gd].astype(BF16), wmix_ref[g], preferred_element_type=F32)
         for g in range(len(POOL_WINDOWS))], axis=1) * ps_ref[...]
    ya = jnp.dot(o.astype(BF16), wva_ref[...], preferred_element_type=F32)
    yb = jnp.dot(yb.astype(BF16), wbb_ref[...], preferred_element_type=F32)
    merged = jax.nn.sigmoid(gate_a) * ya + jax.nn.sigmoid(gate_b) * yb
    mix = jnp.dot(merged.astype(BF16), wout_ref[...], preferred_element_type=F32)
    x1 = _layer_norm(alpha * x + mix, g_ref[...], b_ref[...])
    meta = _router_meta(x1, wr_ref[...], rb_ref[...])
    gate_ref[...] = meta
    d = x1.shape[1]
    x1_ref[:, :d] = x1
    wide = jnp.concatenate([meta[1:3], jnp.zeros((LANES - 2, meta.shape[1]), F32)], axis=0)
    x1_ref[:, d:] = wide.T


def _mix_prompt_kernel(*refs, alpha, tiles_per_seq, n_tiles):
    x1_ref, gate_ref = refs[-4], refs[-3]

    @pl.when(pl.program_id(0) < n_tiles)
    def _():
        _mix_prompt_tile(*refs, alpha=alpha, tiles_per_seq=tiles_per_seq)

    @pl.when(pl.program_id(0) >= n_tiles)
    def _():
        x1_ref[...] = jnp.zeros(x1_ref.shape, F32)
        gate_ref[...] = jnp.zeros(gate_ref.shape, F32)


def _mix_prompt_tile(x_ref, o_ref, w2_ref, wva_ref, wmix_ref, ps_ref, wbb_ref, wout_ref, g_ref, b_ref,
                     wr_ref, rb_ref, x1_ref, gate_ref, tail_ref, ext_ref, *, alpha, tiles_per_seq):
    tm = x_ref.shape[0]
    pw = ps_ref.shape[1]
    gd = pw // len(POOL_WINDOWS)
    x = x_ref[...]
    h2 = jnp.dot(x.astype(BF16), w2_ref[...], preferred_element_type=F32)
    u = h2[:, :pw]
    t = pl.program_id(0) % tiles_per_seq

    @pl.when(t == 0)
    def _():
        ext_ref[0:POOL_PAD, :] = jnp.zeros((POOL_PAD, pw), F32)

    @pl.when(t != 0)
    def _():
        ext_ref[0:POOL_PAD, :] = ext_ref[tm:tm + POOL_PAD, :]

    ext_ref[POOL_PAD:POOL_PAD + tm, :] = u
    tail_ref[0] = u[tm - POOL_PAD:, :]
    pos = t * tm + lax.broadcasted_iota(jnp.int32, (tm, 1), 0)
    pooled = []
    for g, w in enumerate(POOL_WINDOWS):
        acc = ext_ref[POOL_PAD:POOL_PAD + tm, g * gd:(g + 1) * gd]
        for j in range(1, w):
            acc = acc + ext_ref[POOL_PAD - j:POOL_PAD - j + tm, g * gd:(g + 1) * gd]
        cnt = jnp.minimum(w, pos + 1).astype(F32)
        pooled.append(acc / cnt - u[:, g * gd:(g + 1) * gd])
    pooled = jnp.concatenate(pooled, axis=1)
    d = x.shape[1]
    _mix_tail(x, o_ref[...], pooled, h2[:, pw:pw + d], h2[:, pw + d:], wva_ref, wmix_ref, ps_ref, wbb_ref,
              wout_ref, g_ref, b_ref, wr_ref, rb_ref, x1_ref, gate_ref, alpha=alpha)


def _mix_sample_kernel(x_ref, o_ref, pre_ref, x1_all_ref, meta_all_ref, w2_ref, wva_ref, wmix_ref, ps_ref, wbb_ref,
                       wout_ref, g_ref, b_ref, wr_ref, rb_ref, x1_ref, gate_ref, tail_ref, ext_ref,
                       *, alpha, t_new, past):
    del x1_all_ref, meta_all_ref
    tm = x_ref.shape[0]
    sb = tm // t_new
    pw = ps_ref.shape[1]
    gd = pw // len(POOL_WINDOWS)
    x = x_ref[...]
    h2 = jnp.dot(x.astype(BF16), w2_ref[...], preferred_element_type=F32)
    u = h2[:, :pw]
    ext_ref[:, 0:POOL_PAD, :] = pre_ref[...]
    ext_ref[:, POOL_PAD:POOL_PAD + t_new, :] = u.reshape(sb, t_new, pw)
    tail_ref[...] = ext_ref[:, t_new:t_new + POOL_PAD, :]
    pos = past + lax.broadcasted_iota(jnp.int32, (sb, t_new, 1), 1)
    pooled = []
    for g, w in enumerate(POOL_WINDOWS):
        acc = ext_ref[:, POOL_PAD:POOL_PAD + t_new, g * gd:(g + 1) * gd]
        for j in range(1, w):
            acc = acc + ext_ref[:, POOL_PAD - j:POOL_PAD - j + t_new, g * gd:(g + 1) * gd]
        cnt = jnp.minimum(w, pos + 1).astype(F32)
        pooled.append((acc / cnt).reshape(tm, gd) - u[:, g * gd:(g + 1) * gd])
    pooled = jnp.concatenate(pooled, axis=1)
    d = x.shape[1]
    _mix_tail(x, o_ref[...], pooled, h2[:, pw:pw + d], h2[:, pw + d:], wva_ref, wmix_ref, ps_ref, wbb_ref,
              wout_ref, g_ref, b_ref, wr_ref, rb_ref, x1_ref, gate_ref, alpha=alpha)


def _mix_prompt(x, x_off, o, weights, *, n_all, tm, alpha, seq):
    n = o.shape[0]
    d = x.shape[1]
    pw = weights[3].shape[1]
    tiles_per_seq = seq // tm
    n_tiles = n // tm
    last = n_tiles - 1
    return pl.pallas_call(
        functools.partial(_mix_prompt_kernel, alpha=alpha, tiles_per_seq=tiles_per_seq, n_tiles=n_tiles),
        grid=(n_all // tm,),
        in_specs=[pl.BlockSpec((tm, d), lambda i: (jnp.minimum(i, last) + x_off, 0)),
                  pl.BlockSpec((tm, o.shape[1]), lambda i: (jnp.minimum(i, last), 0))]
        + [_const_spec(a.shape) for a in weights],
        out_specs=[pl.BlockSpec((tm, d + LANES), lambda i: (i, 0)),
                   pl.BlockSpec((META_ROWS, tm), lambda i: (0, i)),
                   pl.BlockSpec((1, POOL_PAD, pw), lambda i: (jnp.minimum(i, last) // tiles_per_seq, 0, 0))],
        out_shape=[jax.ShapeDtypeStruct((n_all, d + LANES), F32), jax.ShapeDtypeStruct((META_ROWS, n_all), F32),
                   jax.ShapeDtypeStruct((n // seq, POOL_PAD, pw), F32)],
        scratch_shapes=[pltpu.VMEM((tm + POOL_PAD, pw), F32)],
        compiler_params=_params("arbitrary"),
        name="mix_prompt",
    )(x, o, *weights)


def _mix_sample(x, x_off, o, prefix, x1_all, meta_all, weights, *, row0, tm, alpha, t_new, past):
    n = o.shape[0]
    d = x.shape[1]
    pw = weights[3].shape[1]
    sb = tm // t_new
    off = row0 // tm
    any_spec = pl.BlockSpec(memory_space=pl.ANY)
    return pl.pallas_call(
        functools.partial(_mix_sample_kernel, alpha=alpha, t_new=t_new, past=past),
        grid=(n // tm,),
        in_specs=[pl.BlockSpec((tm, d), lambda i: (i + x_off, 0)), pl.BlockSpec((tm, o.shape[1]), lambda i: (i, 0)),
                  pl.BlockSpec((sb, POOL_PAD, pw), lambda i: (i, 0, 0)), any_spec, any_spec]
        + [_const_spec(a.shape) for a in weights],
        out_specs=[pl.BlockSpec((tm, d + LANES), lambda i: (i + off, 0)),
                   pl.BlockSpec((META_ROWS, tm), lambda i: (0, i + off)),
                   pl.BlockSpec((sb, POOL_PAD, pw), lambda i: (i, 0, 0))],
        out_shape=[jax.ShapeDtypeStruct(x1_all.shape, F32), jax.ShapeDtypeStruct(meta_all.shape, F32),
                   jax.ShapeDtypeStruct((n // t_new, POOL_PAD, pw), F32)],
        input_output_aliases={3: 0, 4: 1},
        scratch_shapes=[pltpu.VMEM((sb, POOL_PAD + t_new, pw), F32)],
        compiler_params=_params("arbitrary"),
        name="mix_sample",
    )(x, o, prefix, x1_all, meta_all, *weights)


ROW_CHUNK = 32


def _moe_kernel(src_ref, dst_ref, ea_ref, eb_ref, chunks_ref, used_ref, x1_hbm, wga_ref, wua_ref, wda_ref, wgb_ref,
                wub_ref, wdb_ref, g_ref, b_ref, out_hbm, xbuf, ybuf, gather_sem, scatter_sem, *, alpha, tm):
    del ea_ref, eb_ref
    i = pl.program_id(0)
    used = used_ref[0]
    buf = i % 2
    d = out_hbm.shape[1]

    def gather_row(tile, b, r):
        src = x1_hbm.at[pl.ds(src_ref[tile * tm + r], 1), :]
        return pltpu.make_async_copy(src, xbuf.at[b, pl.ds(r, 1), :], gather_sem.at[b])

    def scatter_row(tile, b, r):
        dst = out_hbm.at[pl.ds(dst_ref[tile * tm + r], 1), :]
        return pltpu.make_async_copy(ybuf.at[b, pl.ds(r, 1), :], dst, scatter_sem.at[b])

    def for_chunks(tile, fn):
        for c in range(tm // ROW_CHUNK):
            @pl.when(c < chunks_ref[tile])
            def _():
                for r in range(c * ROW_CHUNK, (c + 1) * ROW_CHUNK):
                    fn(r)

    def start_gather(tile, b):
        for_chunks(tile, lambda r: gather_row(tile, b, r).start())

    def wait_gather(tile, b):
        for_chunks(tile, lambda r: gather_row(0, b, 0).wait())

    def start_scatter(tile, b):
        for_chunks(tile, lambda r: scatter_row(tile, b, r).start(priority=r % 2))

    def wait_scatter(tile, b):
        for_chunks(tile, lambda r: scatter_row(0, b, 0).wait())

    @pl.when(i == 0)
    def _():
        xbuf[...] = jnp.zeros(xbuf.shape, F32)
        start_gather(0, 0)
        n_rows, n_out = x1_hbm.shape[0], out_hbm.shape[0]
        ybuf[1] = jnp.zeros(ybuf.shape[1:], F32)
        fills = [pltpu.make_async_copy(ybuf.at[1], out_hbm.at[pl.ds(row, tm), :], scatter_sem.at[1])
                 for row in range(n_rows, n_out, tm)]
        for c in fills:
            c.start()
        for c in fills:
            c.wait()

    for b in range(2):
        @pl.when((i < used) & (buf == b))
        def _():
            @pl.when(i + 1 < used)
            def _():
                start_gather(i + 1, 1 - b)

            wait_gather(i, b)

            @pl.when(i >= 2)
            def _():
                wait_scatter(i - 2, b)

    @pl.when(i < used)
    def _():
        row = xbuf[buf]
        x = row[:, :d]
        wt = row[:, d:]
        xb = x.astype(BF16)

        def expert(wg_ref, wu_ref, wd_ref, col):
            hg = jnp.dot(xb, wg_ref[0], preferred_element_type=F32)
            hu = jnp.dot(xb, wu_ref[0], preferred_element_type=F32)
            hid = (jax.nn.silu(hg) * hu * col).astype(BF16)
            return jnp.dot(hid, wd_ref[0], preferred_element_type=F32)

        y = expert(wga_ref, wua_ref, wda_ref, wt[:, 0:1]) + expert(wgb_ref, wub_ref, wdb_ref, wt[:, 1:2])
        ybuf[buf] = _layer_norm(alpha * x + y, g_ref[...], b_ref[...])

    for b in range(2):
        @pl.when((i < used) & (buf == b))
        def _():
            start_scatter(i, b)

    @pl.when(i == pl.num_programs(0) - 1)
    def _():
        wait_scatter(used - 1, (used - 1) % 2)

        @pl.when(used >= 2)
        def _():
            wait_scatter(used - 2, used % 2)


def _moe(x1, row_src, row_dst, tile_ea, tile_eb, tile_chunks, n_used, wg, wu, wd, g2, b2, *, tm, alpha):
    p = row_src.shape[0]
    d = g2.shape[1]
    f = wg.shape[2]
    first = lambda i, src, dst, ea, eb, chunks, used: (ea[i], 0, 0)
    second = lambda i, src, dst, ea, eb, chunks, used: (eb[i], 0, 0)
    const = lambda shape: pl.BlockSpec(shape, lambda i, *pre: (0,) * len(shape), pipeline_mode=pl.Buffered(1))
    grid_spec = pltpu.PrefetchScalarGridSpec(
        num_scalar_prefetch=6,
        grid=(p // tm,),
        in_specs=[pl.BlockSpec(memory_space=pl.ANY),
                  pl.BlockSpec((1, d, f), first), pl.BlockSpec((1, d, f), first), pl.BlockSpec((1, f, d), first),
                  pl.BlockSpec((1, d, f), second), pl.BlockSpec((1, d, f), second), pl.BlockSpec((1, f, d), second),
                  const(g2.shape), const(b2.shape)],
        out_specs=pl.BlockSpec(memory_space=pl.ANY),
        scratch_shapes=[pltpu.VMEM((2, tm, x1.shape[1]), F32), pltpu.VMEM((2, tm, d), F32),
                        pltpu.SemaphoreType.DMA((2,)), pltpu.SemaphoreType.DMA((2,))],
    )
    return pl.pallas_call(
        functools.partial(_moe_kernel, alpha=alpha, tm=tm),
        grid_spec=grid_spec,
        out_shape=jax.ShapeDtypeStruct((p, d), F32),
        compiler_params=_params("arbitrary"),
        name="moe",
    )(row_src, row_dst, tile_ea, tile_eb, tile_chunks, n_used, x1, wg, wu, wd, wg, wu, wd, g2, b2)


def _routing_plan(meta, *, tm):
    n = meta.shape[1]
    n_tiles = (n + N_BUCKETS * (tm - 1) + tm - 1) // tm
    p = n_tiles * tm
    bucket = meta[0].astype(jnp.int32)
    onehot = (bucket[:, None] == jnp.arange(N_BUCKETS, dtype=jnp.int32)[None, :]).astype(jnp.int32)
    csum = jnp.cumsum(onehot, axis=0)
    rank = jnp.sum(csum * onehot, axis=1) - 1
    tiles_per_bucket = (csum[-1] + tm - 1) // tm
    tile_end = jnp.cumsum(tiles_per_bucket)
    n_used = tile_end[-1:]
    row_start = (tile_end - tiles_per_bucket) * tm
    slot = jnp.sum(onehot * row_start[None, :], axis=1) + rank
    token = jnp.full((p,), -1, jnp.int32).at[slot].set(jnp.arange(n, dtype=jnp.int32), unique_indices=True)
    is_pad = token < 0
    row_src = jnp.where(is_pad, 0, token)
    row_dst = jnp.where(is_pad, n - 1 + jnp.cumsum(is_pad.astype(jnp.int32)), token)
    tile_id = jnp.minimum(jnp.arange(n_tiles, dtype=jnp.int32), n_used - 1)
    tile_bucket = jnp.minimum(jnp.sum((tile_id[:, None] >= tile_end[None, :]).astype(jnp.int32), axis=1),
                              N_BUCKETS - 1)
    group, pair = tile_bucket // PAIRS_PER_GROUP, tile_bucket % PAIRS_PER_GROUP
    lo = sum((pair == k).astype(jnp.int32) * PAIR_LO[k] for k in range(PAIRS_PER_GROUP))
    hi = sum((pair == k).astype(jnp.int32) * PAIR_HI[k] for k in range(PAIRS_PER_GROUP))
    tile_ea = group * EXPERTS_PER_GROUP + lo
    tile_eb = group * EXPERTS_PER_GROUP + hi
    in_bucket = (tile_bucket[:, None] == jnp.arange(N_BUCKETS, dtype=jnp.int32)[None, :]).astype(jnp.int32)
    tokens_left = jnp.sum(in_bucket * (csum[-1] - (tile_id[:, None] - (tile_end - tiles_per_bucket)) * tm), axis=1)
    tile_chunks = (jnp.clip(tokens_left, 0, tm) + ROW_CHUNK - 1) // ROW_CHUNK
    return row_src, row_dst, tile_ea, tile_eb, tile_chunks.astype(jnp.int32), n_used.astype(jnp.int32)


class _Tiles(NamedTuple):
    proj: int
    attn_q: int
    attn_k: int
    mix: int
    sample: int
    moe: int


def _tile_plan(prompt_len, n_sample):
    big = min(512, prompt_len)
    return _Tiles(proj=big, attn_q=big, attn_k=big, mix=big, sample=min(256, n_sample), moe=256)


def _rope_tables(pos, reps):
    half = QK_ROPE_DIM // 2
    inv = ROPE_BASE ** (-jnp.arange(half, dtype=F32) / half)
    ang = pos.astype(F32)[:, None] * inv
    cos, sin = jnp.cos(ang), jnp.sin(ang)
    return (jnp.tile(jnp.concatenate([cos, cos], axis=1), (1, reps)),
            jnp.tile(jnp.concatenate([-sin, sin], axis=1), (1, reps)))


def kernel(x_prompt, x_sample, cache_kv_latent, cache_k_rope, state_pool, page_table, w_in, g_q_norm, g_kv_norm,
           w_q_up, w_k_up, w_v_up, w_pool_mix, pool_scale, w_branch_a, w_branch_b, w_out, ln1_g, ln1_b,
           w_router, router_bias, w_exp_gate, w_exp_up, w_exp_down, ln2_g, ln2_b):
    bp, tp, d = x_prompt.shape
    bs, ts, _ = x_sample.shape
    depth = w_in.shape[0]
    n_pages = page_table.shape[1]
    past = n_pages * cache_kv_latent.shape[2]
    alpha = (2 * depth) ** 0.25
    pw = pool_scale.shape[1]
    n_p, n_s = bp * tp, bs * ts
    s1 = Q_LORA_RANK + KV_LORA_RANK + QK_ROPE_DIM

    tiles = _tile_plan(tp, n_s)
    n_all = n_p + n_s

    cos_p, sin_p = _rope_tables(jnp.arange(tp, dtype=jnp.int32), N_HEADS)
    cos_pt, sin_pt = cos_p.T, sin_p.T
    cos_p, sin_p = cos_p[:, :QK_ROPE_DIM], sin_p[:, :QK_ROPE_DIM]
    cos_s, sin_s = _rope_tables(past + jnp.arange(ts, dtype=jnp.int32), N_HEADS)
    cos_s, sin_s = jnp.tile(cos_s, (tiles.sample // ts, 1)), jnp.tile(sin_s, (tiles.sample // ts, 1))
    prefix = jnp.pad(state_pool, ((0, 0), (0, 0), (POOL_PAD - state_pool.shape[2], 0), (0, 0)))
    cache_rt = jnp.swapaxes(cache_k_rope, 2, 3)
    wr_t = w_router.T
    rb = router_bias.reshape(N_EXPERTS, 1)

    x_p, x_s, x_s_off = x_prompt.reshape(n_p, d), x_sample.reshape(n_s, d), 0
    outs = [[] for _ in range(6)]
    for l in range(depth):
        wq = w_q_up[l]
        wq_nope = jnp.transpose(wq[:, :, :QK_NOPE_DIM], (1, 0, 2))
        wqr = (wq[:, :, QK_NOPE_DIM:] * Q_SCALE).reshape(Q_LORA_RANK, N_HEADS * QK_ROPE_DIM).astype(BF16)
        wk = jnp.transpose(w_k_up[l], (1, 0, 2))
        wv = jnp.transpose(w_v_up[l], (1, 0, 2))
        wba = w_branch_a[l].reshape(N_HEADS, -1, d)
        wlat, wlat_t, wva = _fold_weights(wq_nope, wk, wv, wba)
        w1 = w_in[l, :, :s1].astype(BF16)
        w2 = w_in[l, :, s1:].astype(BF16)
        gq = g_q_norm[l].reshape(1, -1)
        gkv = g_kv_norm[l].reshape(1, -1)
        mix_w = (w2, wva, w_pool_mix[l].astype(BF16), pool_scale[l].reshape(1, pw), w_branch_b[l].astype(BF16),
                 w_out[l].astype(BF16), ln1_g[l].reshape(1, d), ln1_b[l].reshape(1, d), wr_t, rb)
        wg, wu, wd = w_exp_gate[l].astype(BF16), w_exp_up[l].astype(BF16), w_exp_down[l].astype(BF16)
        g2, b2 = ln2_g[l].reshape(1, d), ln2_b[l].reshape(1, d)

        qt, kvcat, ckv_t, ckv, krope = _proj_prompt(x_p, n_p, w1, wlat_t, wqr.T, gq, gkv, cos_p, sin_p, cos_pt,
                                                    sin_pt, tm=tiles.proj)
        o = _attn_prompt(qt, kvcat, ckv_t, batch=bp, seq=tp, tq=tiles.attn_q, tk=tiles.attn_k)
        x1, meta, tail = _mix_prompt(x_p, 0, o, mix_w, n_all=n_all, tm=tiles.mix, alpha=alpha, seq=tp)
        outs[0].append(ckv.reshape(bp, tp, -1))
        outs[1].append(krope.reshape(bp, tp, -1))
        outs[2].append(tail[:, 1:, :])

        q, kvcat, ckv, krope = _proj_sample(x_s, x_s_off, n_s, w1, wlat, wqr, gq, gkv, cos_s, sin_s, tm=tiles.sample)
        o = _attn_sample(page_table, q, kvcat, cache_kv_latent, cache_rt, layer=l, t_new=ts)
        x1, meta, tail = _mix_sample(x_s, x_s_off, o, prefix[l], x1, meta, mix_w, row0=n_p, tm=tiles.sample, alpha=alpha,
                                     t_new=ts, past=past)
        outs[3].append(ckv.reshape(bs, ts, -1))
        outs[4].append(krope.reshape(bs, ts, -1))
        outs[5].append(tail[:, 1:, :])

        row_src, row_dst, tile_ea, tile_eb, tile_chunks, n_used = _routing_plan(meta, tm=tiles.moe)
        x_all = _moe(x1, row_src, row_dst, tile_ea, tile_eb, tile_chunks, n_used, wg, wu, wd, g2, b2, tm=tiles.moe,
                     alpha=alpha)
        x_p, x_s, x_s_off = x_all, x_all, n_p // tiles.sample

    return (x_all[:n_p].reshape(bp, tp, d), x_all[n_p:n_all].reshape(bs, ts, d), jnp.stack(outs[0]),
            jnp.stack(outs[1]), jnp.stack(outs[2]), jnp.stack(outs[3]), jnp.stack(outs[4]), jnp.stack(outs[5]))
```
